```python
import jax, jax.numpy as jnp
from jax import lax
import numpy as np

D_MODEL = 1024
BATCH = 4
SEQ = 4096
DEPTH = 2

N_A_LAYERS = DEPTH // 2
N_B_LAYERS = DEPTH - N_A_LAYERS
CHUNK = 128
A_EXPAND = 2
A_WIDTH = A_EXPAND * D_MODEL
A_GROUPS = 16
A_GROUP_DIM = A_WIDTH // A_GROUPS
HEAD_DIM = 64
N_Q_HEADS = D_MODEL // HEAD_DIM
N_KV_HEADS = max(1, N_Q_HEADS // 8)
Q_PER_KV = N_Q_HEADS // N_KV_HEADS
B_WIDTH = N_Q_HEADS * HEAD_DIM
KV_WIDTH = N_KV_HEADS * HEAD_DIM
WINDOW = 128
ROPE_THETA = 10000.0
EPS = 1e-5

kernel_name = "yoco_gmlp_swa_sink_hybrid"


def rms_norm(x, g):
    xf = x.astype(jnp.float32)
    y = xf * lax.rsqrt(jnp.mean(xf * xf, axis=-1, keepdims=True) + EPS) * g.astype(jnp.float32)
    return y.astype(x.dtype)


def layer_norm(x, g, b):
    xf = x.astype(jnp.float32)
    mu = jnp.mean(xf, axis=-1, keepdims=True)
    xc = xf - mu
    var = jnp.mean(xc * xc, axis=-1, keepdims=True)
    y = xc * lax.rsqrt(var + EPS) * g.astype(jnp.float32) + b.astype(jnp.float32)
    return y.astype(x.dtype)


def rotary(x, pos):
    dh = x.shape[-1]
    inv_freq = ROPE_THETA ** (-jnp.arange(0, dh, 2, dtype=jnp.float32) / dh)
    ang = pos[:, None] * inv_freq[None, :]
    cos = jnp.cos(ang)[None, :, None, :].astype(x.dtype)
    sin = jnp.sin(ang)[None, :, None, :].astype(x.dtype)
    x1, x2 = jnp.split(x, 2, axis=-1)
    return jnp.concatenate([x1 * cos - x2 * sin, x2 * cos + x1 * sin], axis=-1)


def band(t):
    b, s, h, d = t.shape
    blk = t.reshape(b, s // CHUNK, CHUNK, h, d)
    prev = jnp.concatenate([jnp.zeros_like(blk[:, :1]), blk[:, :-1]], axis=1)
    return jnp.concatenate([prev, blk], axis=2)


def gmlp_mixer(h, w_in, ln_g, ln_b, ws, bs, w_out):
    b, s, _ = h.shape
    nc = s // CHUNK
    z = h @ w_in
    u, v, g = jnp.split(z, 3, axis=-1)
    v = layer_norm(v, ln_g, ln_b)
    v = v.reshape(b, nc, CHUNK, A_GROUPS, A_GROUP_DIM)
    causal = jnp.tril(jnp.ones((CHUNK, CHUNK), dtype=bool))
    wsm = jnp.where(causal[None], ws, jnp.zeros_like(ws)).astype(v.dtype)
    sv = jnp.einsum('gts,bcsgd->bctgd', wsm, v) + bs.T[:, :, None].astype(v.dtype)
    sv = sv.reshape(b, s, A_WIDTH)
    y = u * sv * jax.nn.silu(g)
    return y @ w_out


def swa_mixer(h, k_band, v_band, pos, w_in, b_q, sinks, w_out):
    b, s, _ = h.shape
    nb = s // CHUNK
    z = h @ w_in
    q, g = jnp.split(z, 2, axis=-1)
    q = (q + b_q).reshape(b, s, N_Q_HEADS, HEAD_DIM)
    q = rotary(q, pos).reshape(b, nb, CHUNK, N_KV_HEADS, Q_PER_KV, HEAD_DIM)
    scores = jnp.einsum('bnqhrd,bnkhd->bnhrqk', q, k_band).astype(jnp.float32) * (HEAD_DIM ** -0.5)
    qi = jnp.arange(CHUNK)[:, None]
    kj = jnp.arange(2 * CHUNK)[None, :]
    rel = kj - CHUNK - qi
    in_window = (rel <= 0) & (rel > -WINDOW)
    key_valid = (jnp.arange(nb)[:, None] * CHUNK + jnp.arange(2 * CHUNK)[None, :] - CHUNK) >= 0
    mask = in_window[None] & key_valid[:, None, :]
    scores = jnp.where(mask[None, :, None, None], scores, -jnp.inf)
    sink = sinks.astype(jnp.float32).reshape(N_KV_HEADS, Q_PER_KV)[None, None, :, :, None, None]
    m = jnp.maximum(jnp.max(scores, axis=-1, keepdims=True), sink)
    p = jnp.exp(scores - m)
    denom = jnp.sum(p, axis=-1, keepdims=True) + jnp.exp(sink - m)
    p = (p / denom).astype(v_band.dtype)
    o = jnp.einsum('bnhrqk,bnkhd->bnqhrd', p, v_band).reshape(b, s, B_WIDTH)
    y = o * jax.nn.silu(g)
    return y @ w_out


def setup_inputs(seed: int = 0) -> dict:
    key = jax.random.key(seed)
    ks = jax.random.split(key, 20)
    f32 = jnp.float32
    nrm = lambda k, shp, sc: jax.random.normal(k, shp, f32) * sc
    return {
        "x": nrm(ks[0], (BATCH, SEQ, D_MODEL), 1.0),
        "a_norm_g": 1.0 + nrm(ks[1], (N_A_LAYERS, D_MODEL), 0.02),
        "a_w_in": nrm(ks[2], (N_A_LAYERS, D_MODEL, 3 * A_WIDTH), D_MODEL ** -0.5),
        "a_ln_g": 1.0 + nrm(ks[3], (N_A_LAYERS, A_WIDTH), 0.02),
        "a_ln_b": nrm(ks[4], (N_A_LAYERS, A_WIDTH), 0.02),
        "a_ws": nrm(ks[5], (N_A_LAYERS, A_GROUPS, CHUNK, CHUNK), 0.5 * CHUNK ** -0.5),
        "a_bs": 1.0 + nrm(ks[6], (N_A_LAYERS, A_GROUPS, CHUNK), 0.02),
        "a_w_out": nrm(ks[7], (N_A_LAYERS, A_WIDTH, D_MODEL), 0.5 * A_WIDTH ** -0.5),
        "kv_norm_g": 1.0 + nrm(ks[8], (D_MODEL,), 0.02),
        "w_kv": nrm(ks[9], (D_MODEL, 2 * KV_WIDTH), D_MODEL ** -0.5),
        "b_kv": nrm(ks[10], (2 * KV_WIDTH,), 0.02),
        "b_norm_g": 1.0 + nrm(ks[11], (N_B_LAYERS, D_MODEL), 0.02),
        "b_w_in": nrm(ks[12], (N_B_LAYERS, D_MODEL, 2 * B_WIDTH), D_MODEL ** -0.5),
        "b_bq": nrm(ks[13], (N_B_LAYERS, B_WIDTH), 0.02),
        "b_sinks": nrm(ks[14], (N_B_LAYERS, N_Q_HEADS), 1.0),
        "b_w_out": nrm(ks[15], (N_B_LAYERS, B_WIDTH, D_MODEL), B_WIDTH ** -0.5),
        "final_norm_g": 1.0 + nrm(ks[16], (D_MODEL,), 0.02),
    }


def reference(x, a_norm_g, a_w_in, a_ln_g, a_ln_b, a_ws, a_bs, a_w_out, kv_norm_g, w_kv, b_kv,
              b_norm_g, b_w_in, b_bq, b_sinks, b_w_out, final_norm_g):
    b, s, _ = x.shape
    pos = jnp.arange(s, dtype=jnp.float32)
    h = x
    k_band = None
    v_band = None
    for l in range(DEPTH):
        if l < N_A_LAYERS:
            i = l
            h = h + gmlp_mixer(rms_norm(h, a_norm_g[i]), a_w_in[i], a_ln_g[i], a_ln_b[i],
                               a_ws[i], a_bs[i], a_w_out[i])
        else:
            if l == N_A_LAYERS:
                kv = rms_norm(h, kv_norm_g) @ w_kv + b_kv
                k, v = jnp.split(kv, 2, axis=-1)
                k = rotary(k.reshape(b, s, N_KV_HEADS, HEAD_DIM), pos)
                v = v.reshape(b, s, N_KV_HEADS, HEAD_DIM)
                k_band = band(k)
                v_band = band(v)
            i = l - N_A_LAYERS
            h = h + swa_mixer(rms_norm(h, b_norm_g[i]), k_band, v_band, pos,
                              b_w_in[i], b_bq[i], b_sinks[i], b_w_out[i])
    return rms_norm(h, final_norm_g)
```

```python
import functools

import jax
import jax.numpy as jnp
from jax import lax
from jax.experimental import pallas as pl
from jax.experimental.pallas import tpu as pltpu

D_MODEL = 1024
CHUNK = 128
A_WIDTH = 2048
A_GROUPS = 16
HEAD_DIM = 64
N_Q_HEADS = 16
N_KV_HEADS = 2
B_WIDTH = 1024
KV_WIDTH = 128
ROPE_THETA = 10000.0
EPS = 1e-5

LANES = 128
HEADS_PER_SLAB = LANES // HEAD_DIM
N_SLABS = B_WIDTH // LANES
SLABS_PER_KV = N_SLABS // N_KV_HEADS

TOKENS_PER_STEP = 256
A_COL_BLOCK = 512
VMEM_LIMIT_BYTES = 56 * 1024 * 1024

_F32 = jnp.float32
_BF16 = jnp.bfloat16


def _dot(a, b):
    return jnp.dot(a, b, preferred_element_type=_F32)


def _dot_nt(a, b):
    return lax.dot_general(a, b, (((1,), (1,)), ((), ())), preferred_element_type=_F32)


def _silu(g):
    return g / (1.0 + jnp.exp(-g))


def _layer_a_kernel(x_ref, ng_ref, win_ref, lng_ref, lnb_ref, ws_ref, bst_ref, wout_ref,
                    o_ref, vn_ref, y_ref):
    tm = x_ref.shape[0]
    x = x_ref[...]
    h1 = (x * lax.rsqrt(jnp.mean(x * x, axis=-1, keepdims=True) + EPS) * ng_ref[...]).astype(_BF16)

    v = _dot(h1, win_ref[:, A_WIDTH:2 * A_WIDTH])
    mu = jnp.mean(v, axis=-1, keepdims=True)
    vc = v - mu
    var = jnp.mean(vc * vc, axis=-1, keepdims=True)
    vn_ref[...] = (vc * lax.rsqrt(var + EPS) * lng_ref[...] + lnb_ref[...]).astype(_BF16)

    row = lax.broadcasted_iota(jnp.int32, (CHUNK, CHUNK), 0)
    col = lax.broadcasted_iota(jnp.int32, (CHUNK, CHUNK), 1)
    causal = col <= row

    groups_per_block = A_COL_BLOCK // CHUNK
    for j in range(A_WIDTH // A_COL_BLOCK):
        c0 = j * A_COL_BLOCK
        u = _dot(h1, win_ref[:, c0:c0 + A_COL_BLOCK])
        gt = _dot(h1, win_ref[:, 2 * A_WIDTH + c0:2 * A_WIDTH + c0 + A_COL_BLOCK])
        for gi in range(groups_per_block):
            g = j * groups_per_block + gi
            wsm = jnp.where(causal, ws_ref[g], 0.0).astype(_BF16)
            bias = jnp.broadcast_to(bst_ref[:, g:g + 1], (CHUNK, CHUNK))
            for c in range(tm // CHUNK):
                rows = slice(c * CHUNK, (c + 1) * CHUNK)
                cols = slice(gi * CHUNK, (gi + 1) * CHUNK)
                sv = _dot(wsm, vn_ref[rows, g * CHUNK:(g + 1) * CHUNK]) + bias
                y_ref[rows, g * CHUNK:(g + 1) * CHUNK] = (
                    u[rows, cols] * sv * _silu(gt[rows, cols])).astype(_BF16)

    o_ref[...] = x + _dot(y_ref[...], wout_ref[...])


def _layer_a(x2, norm_g, w_in, ln_g, ln_b, ws, bs_t, w_out):
    t = x2.shape[0]
    tm = TOKENS_PER_STEP
    const = lambda i: (0, 0)
    return pl.pallas_call(
        _layer_a_kernel,
        out_shape=jax.ShapeDtypeStruct((t, D_MODEL), _F32),
        grid=(t // tm,),
        in_specs=[
            pl.BlockSpec((tm, D_MODEL), lambda i: (i, 0)),
            pl.BlockSpec((1, D_MODEL), const),
            pl.BlockSpec((D_MODEL, 3 * A_WIDTH), const, pipeline_mode=pl.Buffered(1)),
            pl.BlockSpec((1, A_WIDTH), const),
            pl.BlockSpec((1, A_WIDTH), const),
            pl.BlockSpec((A_GROUPS, CHUNK, CHUNK), lambda i: (0, 0, 0)),
            pl.BlockSpec((CHUNK, A_GROUPS), const),
            pl.BlockSpec((A_WIDTH, D_MODEL), const, pipeline_mode=pl.Buffered(1)),
        ],
        out_specs=pl.BlockSpec((tm, D_MODEL), lambda i: (i, 0)),
        scratch_shapes=[
            pltpu.VMEM((tm, A_WIDTH), _BF16),
            pltpu.VMEM((tm, A_WIDTH), _BF16),
        ],
        compiler_params=pltpu.CompilerParams(
            dimension_semantics=("arbitrary",), vmem_limit_bytes=VMEM_LIMIT_BYTES),
        name="gmlp_layer",
    )(x2, norm_g, w_in, ln_g, ln_b, ws, bs_t, w_out)


def _layer_b_kernel(sinks_ref, h_ref, kvg_ref, bg_ref, wkv_ref, bkv_ref, win_ref, bq_ref,
                    wout_ref, fg_ref, cos_ref, sin_ref,
                    o_ref, ka_ref, kb_ref, va_ref, vb_ref, y_ref):
    tm = h_ref.shape[0]
    step = pl.program_id(1)
    bands = (ka_ref, kb_ref, va_ref, vb_ref)

    @pl.when(step == 0)
    def _():
        for band in bands:
            band[:, 0:CHUNK, :] = jnp.zeros((N_KV_HEADS, CHUNK, LANES), _BF16)

    h = h_ref[...]
    xn = h * lax.rsqrt(jnp.mean(h * h, axis=-1, keepdims=True) + EPS)
    kvn = (xn * kvg_ref[...]).astype(_BF16)
    h2 = (xn * bg_ref[...]).astype(_BF16)

    cos = cos_ref[...]
    sin = sin_ref[...]
    lane = lax.broadcasted_iota(jnp.int32, (tm, LANES), 1)
    first_half = (lane % HEAD_DIM) < (HEAD_DIM // 2)
    low_head = lane < HEAD_DIM

    def rope(t):
        swapped = jnp.where(first_half,
                            pltpu.roll(t, LANES - HEAD_DIM // 2, 1),
                            pltpu.roll(t, HEAD_DIM // 2, 1))
        return t * cos + swapped * sin

    kv = _dot(kvn, wkv_ref[...]) + bkv_ref[...]
    k = rope(kv[:, :KV_WIDTH])
    v = kv[:, KV_WIDTH:]
    k_sw = pltpu.roll(k, HEAD_DIM, 1)
    v_sw = pltpu.roll(v, HEAD_DIM, 1)
    new = slice(CHUNK, CHUNK + tm)
    zero = jnp.zeros_like(k)
    ka_ref[0, new, :] = jnp.where(low_head, k, zero).astype(_BF16)
    kb_ref[0, new, :] = jnp.where(low_head, zero, k_sw).astype(_BF16)
    ka_ref[1, new, :] = jnp.where(low_head, k_sw, zero).astype(_BF16)
    kb_ref[1, new, :] = jnp.where(low_head, zero, k).astype(_BF16)
    va_ref[0, new, :] = jnp.where(low_head, v, zero).astype(_BF16)
    vb_ref[0, new, :] = jnp.where(low_head, zero, v_sw).astype(_BF16)
    va_ref[1, new, :] = jnp.where(low_head, v_sw, zero).astype(_BF16)
    vb_ref[1, new, :] = jnp.where(low_head, zero, v).astype(_BF16)

    z = _dot(h2, win_ref[...])

    qi = lax.broadcasted_iota(jnp.int32, (CHUNK, 2 * CHUNK), 0)
    kj = lax.broadcasted_iota(jnp.int32, (CHUNK, 2 * CHUNK), 1)
    in_window = (kj > qi) & (kj <= qi + CHUNK)
    first_chunk_mask = in_window & ((kj >= CHUNK) | (step > 0))
    low_head_c = lax.broadcasted_iota(jnp.int32, (CHUNK, LANES), 1) < HEAD_DIM
    neg_inf = jnp.float32(-jnp.inf)

    for s in range(N_SLABS):
        lanes = slice(s * LANES, (s + 1) * LANES)
        kvh = s // SLABS_PER_KV
        q = (rope(z[:, lanes] + bq_ref[:, lanes]) * (HEAD_DIM ** -0.5)).astype(_BF16)
        gate = z[:, B_WIDTH + s * LANES:B_WIDTH + (s + 1) * LANES]
        for c in range(tm // CHUNK):
            rows = slice(c * CHUNK, (c + 1) * CHUNK)
            band_rows = slice(c * CHUNK, (c + 2) * CHUNK)
            mask = first_chunk_mask if c == 0 else in_window
            qc = q[rows]
            probs = []
            inv = []
            for hh, kref in enumerate((ka_ref, kb_ref)):
                sink = sinks_ref[s * HEADS_PER_SLAB + hh]
                sc = jnp.where(mask, _dot_nt(qc, kref[kvh, band_rows, :]), neg_inf)
                m = jnp.maximum(jnp.max(sc, axis=-1, keepdims=True), sink)
                p = jnp.exp(sc - m)
                denom = jnp.sum(p, axis=-1, keepdims=True) + jnp.exp(sink - m)
                probs.append(p.astype(_BF16))
                inv.append(1.0 / denom)
            o = (_dot(probs[0], va_ref[kvh, band_rows, :])
                 + _dot(probs[1], vb_ref[kvh, band_rows, :]))
            o = o * jnp.where(low_head_c, inv[0], inv[1])
            y_ref[rows, lanes] = (o * _silu(gate[rows])).astype(_BF16)

    out = h + _dot(y_ref[...], wout_ref[...])
    o_ref[...] = out * lax.rsqrt(jnp.mean(out * out, axis=-1, keepdims=True) + EPS) * fg_ref[...]

    for band in bands:
        band[:, 0:CHUNK, :] = band[:, tm:tm + CHUNK, :]


def _layer_b(h2d, batch, seq, sinks, kv_g, b_g, w_kv, b_kv, w_in, b_q, w_out, f_g, cos, sin):
    tm = TOKENS_PER_STEP
    steps = seq // tm
    const = lambda b, i: (0, 0)
    tok = lambda b, i: (b * steps + i, 0)
    band = pltpu.VMEM((N_KV_HEADS, tm + CHUNK, LANES), _BF16)
    return pl.pallas_call(
        _layer_b_kernel,
        out_shape=jax.ShapeDtypeStruct((batch * seq, D_MODEL), _F32),
        grid=(batch, steps),
        in_specs=[
            pl.BlockSpec(memory_space=pltpu.SMEM),
            pl.BlockSpec((tm, D_MODEL), tok),
            pl.BlockSpec((1, D_MODEL), const),
            pl.BlockSpec((1, D_MODEL), const),
            pl.BlockSpec((D_MODEL, 2 * KV_WIDTH), const),
            pl.BlockSpec((1, 2 * KV_WIDTH), const),
            pl.BlockSpec((D_MODEL, 2 * B_WIDTH), const),
            pl.BlockSpec((1, B_WIDTH), const),
            pl.BlockSpec((B_WIDTH, D_MODEL), const),
            pl.BlockSpec((1, D_MODEL), const),
            pl.BlockSpec((tm, LANES), lambda b, i: (i, 0)),
            pl.BlockSpec((tm, LANES), lambda b, i: (i, 0)),
        ],
        out_specs=pl.BlockSpec((tm, D_MODEL), tok),
        scratch_shapes=[band, band, band, band, pltpu.VMEM((tm, B_WIDTH), _BF16)],
        compiler_params=pltpu.CompilerParams(
            dimension_semantics=("arbitrary", "arbitrary"), vmem_limit_bytes=VMEM_LIMIT_BYTES),
        name="swa_layer",
    )(sinks, h2d, kv_g, b_g, w_kv, b_kv, w_in, b_q, w_out, f_g, cos, sin)


def _rope_tables(seq):
    half = HEAD_DIM // 2
    inv_freq = ROPE_THETA ** (-jnp.arange(0, HEAD_DIM, 2, dtype=_F32) / HEAD_DIM)
    ang = jnp.arange(seq, dtype=_F32)[:, None] * inv_freq[None, :]
    cos = jnp.cos(ang)
    sin = jnp.sin(ang)
    cos_slab = jnp.tile(jnp.concatenate([cos, cos], axis=-1), (1, HEADS_PER_SLAB))
    sin_slab = jnp.tile(jnp.concatenate([-sin, sin], axis=-1), (1, HEADS_PER_SLAB))
    assert cos_slab.shape == (seq, LANES) and half * 2 == HEAD_DIM
    return cos_slab, sin_slab


def kernel(x, a_norm_g, a_w_in, a_ln_g, a_ln_b, a_ws, a_bs, a_w_out, kv_norm_g, w_kv, b_kv,
           b_norm_g, b_w_in, b_bq, b_sinks, b_w_out, final_norm_g):
    batch, seq, d = x.shape
    assert d == D_MODEL and seq % TOKENS_PER_STEP == 0 and TOKENS_PER_STEP % CHUNK == 0
    assert a_w_in.shape[0] == 1 and b_w_in.shape[0] == 1
    row = lambda p: p.reshape(1, -1)

    x2 = x.reshape(batch * seq, d)
    h = _layer_a(x2, row(a_norm_g[0]), a_w_in[0].astype(_BF16), row(a_ln_g[0]), row(a_ln_b[0]),
                 a_ws[0], a_bs[0].T, a_w_out[0].astype(_BF16))
    cos, sin = _rope_tables(seq)
    out = _layer_b(h, batch, seq, b_sinks[0], row(kv_norm_g), row(b_norm_g[0]),
                   w_kv.astype(_BF16), row(b_kv), b_w_in[0].astype(_BF16), row(b_bq[0]),
                   b_w_out[0].astype(_BF16), row(final_norm_g), cos, sin)
    return out.reshape(batch, seq, d)
```

```python
import jax
import jax.numpy as jnp
from jax import lax
from jax.experimental import pallas as pl
from jax.experimental.pallas import tpu as pltpu

D_MODEL = 1024
CHUNK = 128
A_WIDTH = 2048
A_GROUPS = 16
HEAD_DIM = 64
N_Q_HEADS = 16
N_KV_HEADS = 2
B_WIDTH = 1024
KV_WIDTH = 128
ROPE_THETA = 10000.0
EPS = 1e-5
LOG2_E = 1.4426950408889634

LANES = 128
HEADS_PER_SLAB = LANES // HEAD_DIM
N_SLABS = B_WIDTH // LANES
SLABS_PER_KV = N_SLABS // N_KV_HEADS

TOKENS_PER_STEP = 256
A_COL_BLOCK = 512
VMEM_LIMIT_BYTES = 56 * 1024 * 1024

_F32 = jnp.float32
_BF16 = jnp.bfloat16


def _dot(a, b):
    return jnp.dot(a, b, preferred_element_type=_F32)


def _silu(g):
    return g / (1.0 + jnp.exp(-g))


def _layer_a_kernel(x_ref, ng_ref, win_ref, lng_ref, lnb_ref, ws_ref, bst_ref, wout_ref,
                    o_ref, vn_ref, y_ref):
    tm = x_ref.shape[0]
    x = x_ref[...]
    h1 = (x * lax.rsqrt(jnp.mean(x * x, axis=-1, keepdims=True) + EPS) * ng_ref[...]).astype(_BF16)

    v = _dot(h1, win_ref[:, A_WIDTH:2 * A_WIDTH])
    mu = jnp.mean(v, axis=-1, keepdims=True)
    vc = v - mu
    var = jnp.mean(vc * vc, axis=-1, keepdims=True)
    vn_ref[...] = (vc * lax.rsqrt(var + EPS) * lng_ref[...] + lnb_ref[...]).astype(_BF16)

    row = lax.broadcasted_iota(jnp.int32, (CHUNK, CHUNK), 0)
    col = lax.broadcasted_iota(jnp.int32, (CHUNK, CHUNK), 1)
    causal = col <= row

    groups_per_block = A_COL_BLOCK // CHUNK
    for j in range(A_WIDTH // A_COL_BLOCK):
        c0 = j * A_COL_BLOCK
        u = _dot(h1, win_ref[:, c0:c0 + A_COL_BLOCK])
        gt = _dot(h1, win_ref[:, 2 * A_WIDTH + c0:2 * A_WIDTH + c0 + A_COL_BLOCK])
        for gi in range(groups_per_block):
            g = j * groups_per_block + gi
            wsm = jnp.where(causal, ws_ref[g], 0.0).astype(_BF16)
            bias = jnp.broadcast_to(bst_ref[:, g:g + 1], (CHUNK, CHUNK))
            for c in range(tm // CHUNK):
                rows = slice(c * CHUNK, (c + 1) * CHUNK)
                cols = slice(gi * CHUNK, (gi + 1) * CHUNK)
                sv = _dot(wsm, vn_ref[rows, g * CHUNK:(g + 1) * CHUNK]) + bias
                y_ref[rows, g * CHUNK:(g + 1) * CHUNK] = (
                    u[rows, cols] * sv * _silu(gt[rows, cols])).astype(_BF16)

    o_ref[...] = x + _dot(y_ref[...], wout_ref[...])


def _layer_a(x2, norm_g, w_in, ln_g, ln_b, ws, bs_t, w_out):
    t = x2.shape[0]
    tm = TOKENS_PER_STEP
    const = lambda i: (0, 0)
    return pl.pallas_call(
        _layer_a_kernel,
        out_shape=jax.ShapeDtypeStruct((t, D_MODEL), _F32),
        grid=(t // tm,),
        in_specs=[
            pl.BlockSpec((tm, D_MODEL), lambda i: (i, 0)),
            pl.BlockSpec((1, D_MODEL), const),
            pl.BlockSpec((D_MODEL, 3 * A_WIDTH), const, pipeline_mode=pl.Buffered(1)),
            pl.BlockSpec((1, A_WIDTH), const),
            pl.BlockSpec((1, A_WIDTH), const),
            pl.BlockSpec((A_GROUPS, CHUNK, CHUNK), lambda i: (0, 0, 0)),
            pl.BlockSpec((CHUNK, A_GROUPS), const),
            pl.BlockSpec((A_WIDTH, D_MODEL), const, pipeline_mode=pl.Buffered(1)),
        ],
        out_specs=pl.BlockSpec((tm, D_MODEL), lambda i: (i, 0)),
        scratch_shapes=[
            pltpu.VMEM((tm, A_WIDTH), _BF16),
            pltpu.VMEM((tm, A_WIDTH), _BF16),
        ],
        compiler_params=pltpu.CompilerParams(
            dimension_semantics=("arbitrary",), vmem_limit_bytes=VMEM_LIMIT_BYTES),
        name="gmlp_layer",
    )(x2, norm_g, w_in, ln_g, ln_b, ws, bs_t, w_out)


def _layer_b_kernel(sinks_ref, h_ref, kvg_ref, bg_ref, wkv_ref, bkv_ref, win_ref, bq_ref,
                    wout_ref, fg_ref, cos_ref, sin_ref,
                    o_ref, kt_ref, vx_ref, y_ref):
    tm = h_ref.shape[0]
    step = pl.program_id(1)

    @pl.when(step == 0)
    def _():
        kt_ref[:, :, 0:CHUNK] = jnp.zeros((2 * N_KV_HEADS, LANES, CHUNK), _BF16)
        vx_ref[:, 0:CHUNK, :] = jnp.zeros((2 * N_KV_HEADS, CHUNK, 2 * LANES), _BF16)

    @pl.when(step > 0)
    def _():
        kt_ref[:, :, 0:CHUNK] = kt_ref[:, :, tm:tm + CHUNK]
        vx_ref[:, 0:CHUNK, :] = vx_ref[:, tm:tm + CHUNK, :]

    h = h_ref[...]
    xn = h * lax.rsqrt(jnp.mean(h * h, axis=-1, keepdims=True) + EPS)
    kvn = (xn * kvg_ref[...]).astype(_BF16)
    h2 = (xn * bg_ref[...]).astype(_BF16)

    cos = cos_ref[...]
    sin = sin_ref[...]
    lane = lax.broadcasted_iota(jnp.int32, (tm, LANES), 1)
    first_half = (lane % HEAD_DIM) < (HEAD_DIM // 2)
    low_head = lane < HEAD_DIM

    def rope(t):
        swapped = jnp.where(first_half,
                            pltpu.roll(t, LANES - HEAD_DIM // 2, 1),
                            pltpu.roll(t, HEAD_DIM // 2, 1))
        return t * cos + swapped * sin

    kv = _dot(kvn, wkv_ref[...]) + bkv_ref[...]
    new = slice(CHUNK, CHUNK + tm)

    k_t = rope(kv[:, :KV_WIDTH]).T
    zero_rows = jnp.zeros((HEAD_DIM, tm), _F32)
    for kvh in range(N_KV_HEADS):
        head_t = k_t[kvh * HEAD_DIM:(kvh + 1) * HEAD_DIM]
        kt_ref[2 * kvh, :, new] = jnp.concatenate([head_t, zero_rows], axis=0).astype(_BF16)
        kt_ref[2 * kvh + 1, :, new] = jnp.concatenate([zero_rows, head_t], axis=0).astype(_BF16)

    v = kv[:, KV_WIDTH:]
    v_sw = pltpu.roll(v, HEAD_DIM, 1)
    zero = jnp.zeros_like(v)
    one = jnp.ones_like(v)
    ones_lo = jnp.where(low_head, one, zero)
    ones_hi = jnp.where(low_head, zero, one)
    v_low = (jnp.where(low_head, v, zero), jnp.where(low_head, v_sw, zero))
    v_high = (jnp.where(low_head, zero, v_sw), jnp.where(low_head, zero, v))
    for kvh in range(N_KV_HEADS):
        vx_ref[2 * kvh, new, :] = jnp.concatenate([v_low[kvh], ones_lo], axis=1).astype(_BF16)
        vx_ref[2 * kvh + 1, new, :] = jnp.concatenate([v_high[kvh], ones_hi], axis=1).astype(_BF16)

    z = _dot(h2, win_ref[...])

    qi = lax.broadcasted_iota(jnp.int32, (CHUNK, 2 * CHUNK), 0)
    kj = lax.broadcasted_iota(jnp.int32, (CHUNK, 2 * CHUNK), 1)
    in_window = (kj > qi) & (kj <= qi + CHUNK)
    neg_inf = jnp.float32(-jnp.inf)
    bias = jnp.where(in_window, 0.0, neg_inf)
    bias_first = jnp.where(in_window & ((kj >= CHUNK) | (step > 0)), 0.0, neg_inf)
    low_head_c = lax.broadcasted_iota(jnp.int32, (CHUNK, LANES), 1) < HEAD_DIM

    q_scale = (HEAD_DIM ** -0.5) * LOG2_E
    q_slabs = [(rope(z[:, s * LANES:(s + 1) * LANES] + bq_ref[:, s * LANES:(s + 1) * LANES])
                * q_scale).astype(_BF16) for s in range(N_SLABS)]

    for c in range(tm // CHUNK):
        rows = slice(c * CHUNK, (c + 1) * CHUNK)
        band = slice(c * CHUNK, (c + 2) * CHUNK)
        chunk_bias = bias_first if c == 0 else bias
        for kvh in range(N_KV_HEADS):
            slabs = range(kvh * SLABS_PER_KV, (kvh + 1) * SLABS_PER_KV)
            q_stack = jnp.concatenate([q_slabs[s][rows] for s in slabs], axis=0)
            probs = []
            sink_terms = []
            for half in range(HEADS_PER_SLAB):
                sc = _dot(q_stack, kt_ref[2 * kvh + half, :, band])
                p_parts = []
                e_parts = []
                for si, s in enumerate(slabs):
                    sink = sinks_ref[s * HEADS_PER_SLAB + half] * LOG2_E
                    sb = sc[si * CHUNK:(si + 1) * CHUNK] + chunk_bias
                    m = jnp.maximum(jnp.max(sb, axis=-1, keepdims=True), sink)
                    p_parts.append(jnp.exp2(sb - m).astype(_BF16))
                    e_parts.append(jnp.exp2(sink - m))
                probs.append(jnp.concatenate(p_parts, axis=0))
                sink_terms.append(e_parts)
            od = (_dot(probs[0], vx_ref[2 * kvh, band, :])
                  + _dot(probs[1], vx_ref[2 * kvh + 1, band, :]))
            for si, s in enumerate(slabs):
                blk = slice(si * CHUNK, (si + 1) * CHUNK)
                denom = od[blk, LANES:] + jnp.where(low_head_c, sink_terms[0][si], sink_terms[1][si])
                gate = z[rows, B_WIDTH + s * LANES:B_WIDTH + (s + 1) * LANES]
                y_ref[rows, s * LANES:(s + 1) * LANES] = (
                    od[blk, :LANES] * gate / (denom * (1.0 + jnp.exp(-gate)))).astype(_BF16)

    out = h + _dot(y_ref[...], wout_ref[...])
    o_ref[...] = out * lax.rsqrt(jnp.mean(out * out, axis=-1, keepdims=True) + EPS) * fg_ref[...]


def _layer_b(h2d, batch, seq, sinks, kv_g, b_g, w_kv, b_kv, w_in, b_q, w_out, f_g, cos, sin):
    tm = TOKENS_PER_STEP
    steps = seq // tm
    const = lambda b, i: (0, 0)
    tok = lambda b, i: (b * steps + i, 0)
    return pl.pallas_call(
        _layer_b_kernel,
        out_shape=jax.ShapeDtypeStruct((batch * seq, D_MODEL), _F32),
        grid=(batch, steps),
        in_specs=[
            pl.BlockSpec(memory_space=pltpu.SMEM),
            pl.BlockSpec((tm, D_MODEL), tok),
            pl.BlockSpec((1, D_MODEL), const),
            pl.BlockSpec((1, D_MODEL), const),
            pl.BlockSpec((D_MODEL, 2 * KV_WIDTH), const),
            pl.BlockSpec((1, 2 * KV_WIDTH), const),
            pl.BlockSpec((D_MODEL, 2 * B_WIDTH), const),
            pl.BlockSpec((1, B_WIDTH), const),
            pl.BlockSpec((B_WIDTH, D_MODEL), const),
            pl.BlockSpec((1, D_MODEL), const),
            pl.BlockSpec((tm, LANES), lambda b, i: (i, 0)),
            pl.BlockSpec((tm, LANES), lambda b, i: (i, 0)),
        ],
        out_specs=pl.BlockSpec((tm, D_MODEL), tok),
        scratch_shapes=[
            pltpu.VMEM((2 * N_KV_HEADS, LANES, CHUNK + tm), _BF16),
            pltpu.VMEM((2 * N_KV_HEADS, CHUNK + tm, 2 * LANES), _BF16),
            pltpu.VMEM((tm, B_WIDTH), _BF16),
        ],
        compiler_params=pltpu.CompilerParams(
            dimension_semantics=("arbitrary", "arbitrary"), vmem_limit_bytes=VMEM_LIMIT_BYTES),
        name="swa_layer",
    )(sinks, h2d, kv_g, b_g, w_kv, b_kv, w_in, b_q, w_out, f_g, cos, sin)


def _rope_tables(seq):
    inv_freq = ROPE_THETA ** (-jnp.arange(0, HEAD_DIM, 2, dtype=_F32) / HEAD_DIM)
    ang = jnp.arange(seq, dtype=_F32)[:, None] * inv_freq[None, :]
    cos = jnp.cos(ang)
    sin = jnp.sin(ang)
    cos_slab = jnp.tile(jnp.concatenate([cos, cos], axis=-1), (1, HEADS_PER_SLAB))
    sin_slab = jnp.tile(jnp.concatenate([-sin, sin], axis=-1), (1, HEADS_PER_SLAB))
    assert cos_slab.shape == (seq, LANES)
    return cos_slab, sin_slab


def kernel(x, a_norm_g, a_w_in, a_ln_g, a_ln_b, a_ws, a_bs, a_w_out, kv_norm_g, w_kv, b_kv,
           b_norm_g, b_w_in, b_bq, b_sinks, b_w_out, final_norm_g):
    batch, seq, d = x.shape
    assert d == D_MODEL and seq % TOKENS_PER_STEP == 0 and TOKENS_PER_STEP % CHUNK == 0
    assert a_w_in.shape[0] == 1 and b_w_in.shape[0] == 1
    row = lambda p: p.reshape(1, -1)

    x2 = x.reshape(batch * seq, d)
    h = _layer_a(x2, row(a_norm_g[0]), a_w_in[0].astype(_BF16), row(a_ln_g[0]), row(a_ln_b[0]),
                 a_ws[0], a_bs[0].T, a_w_out[0].astype(_BF16))
    cos, sin = _rope_tables(seq)
    out = _layer_b(h, batch, seq, b_sinks[0], row(kv_norm_g), row(b_norm_g[0]),
                   w_kv.astype(_BF16), row(b_kv), b_w_in[0].astype(_BF16), row(b_bq[0]),
                   b_w_out[0].astype(_BF16), row(final_norm_g), cos, sin)
    return out.reshape(batch, seq, d)
```

```python
import functools

import jax
import jax.numpy as jnp
from jax import lax
from jax.experimental import pallas as pl
from jax.experimental.pallas import tpu as pltpu

D_MODEL = 1024
CHUNK = 128
A_WIDTH = 2048
A_GROUPS = 16
HEAD_DIM = 64
N_Q_HEADS = 16
N_KV_HEADS = 2
B_WIDTH = 1024
KV_WIDTH = 128
ROPE_THETA = 10000.0
EPS = 1e-5
LOG2_E = 1.4426950408889634

LANES = 128
HEADS_PER_SLAB = LANES // HEAD_DIM
N_SLABS = B_WIDTH // LANES
SLABS_PER_KV = N_SLABS // N_KV_HEADS

TOKENS_PER_STEP = 256
P_BLOCK = 256
A_COL_BLOCK = 512
VMEM_LIMIT_BYTES = 56 * 1024 * 1024

_F32 = jnp.float32
_BF16 = jnp.bfloat16


def _dot(a, b):
    return jnp.dot(a, b, preferred_element_type=_F32)


def _silu(g):
    return g / (1.0 + jnp.exp(-g))


def _layer_a_kernel(x_ref, ng_ref, win_ref, lng_ref, lnb_ref, ws_ref, bst_ref, wout_ref,
                    o_ref, vn_ref, y_ref):
    tm = x_ref.shape[0]
    x = x_ref[...]
    h1 = (x * lax.rsqrt(jnp.mean(x * x, axis=-1, keepdims=True) + EPS) * ng_ref[...]).astype(_BF16)

    v = _dot(h1, win_ref[:, A_WIDTH:2 * A_WIDTH])
    mu = jnp.mean(v, axis=-1, keepdims=True)
    vc = v - mu
    var = jnp.mean(vc * vc, axis=-1, keepdims=True)
    vn_ref[...] = (vc * lax.rsqrt(var + EPS) * lng_ref[...] + lnb_ref[...]).astype(_BF16)

    row = lax.broadcasted_iota(jnp.int32, (CHUNK, CHUNK), 0)
    col = lax.broadcasted_iota(jnp.int32, (CHUNK, CHUNK), 1)
    causal = col <= row

    groups_per_block = A_COL_BLOCK // CHUNK
    for j in range(A_WIDTH // A_COL_BLOCK):
        c0 = j * A_COL_BLOCK
        u = _dot(h1, win_ref[:, c0:c0 + A_COL_BLOCK])
        gt = _dot(h1, win_ref[:, 2 * A_WIDTH + c0:2 * A_WIDTH + c0 + A_COL_BLOCK])
        for gi in range(groups_per_block):
            g = j * groups_per_block + gi
            wsm = jnp.where(causal, ws_ref[g], 0.0).astype(_BF16)
            bias = jnp.broadcast_to(bst_ref[:, g:g + 1], (CHUNK, CHUNK))
            for c in range(tm // CHUNK):
                rows = slice(c * CHUNK, (c + 1) * CHUNK)
                cols = slice(gi * CHUNK, (gi + 1) * CHUNK)
                sv = _dot(wsm, vn_ref[rows, g * CHUNK:(g + 1) * CHUNK]) + bias
                y_ref[rows, g * CHUNK:(g + 1) * CHUNK] = (
                    u[rows, cols] * sv * _silu(gt[rows, cols])).astype(_BF16)

    o_ref[...] = x + _dot(y_ref[...], wout_ref[...])


def _layer_a(x2, norm_g, w_in, ln_g, ln_b, ws, bs_t, w_out):
    t = x2.shape[0]
    tm = TOKENS_PER_STEP
    const = lambda i: (0, 0)
    return pl.pallas_call(
        _layer_a_kernel,
        out_shape=jax.ShapeDtypeStruct((t, D_MODEL), _F32),
        grid=(t // tm,),
        in_specs=[
            pl.BlockSpec((tm, D_MODEL), lambda i: (i, 0)),
            pl.BlockSpec((1, D_MODEL), const),
            pl.BlockSpec((D_MODEL, 3 * A_WIDTH), const, pipeline_mode=pl.Buffered(1)),
            pl.BlockSpec((1, A_WIDTH), const),
            pl.BlockSpec((1, A_WIDTH), const),
            pl.BlockSpec((A_GROUPS, CHUNK, CHUNK), lambda i: (0, 0, 0)),
            pl.BlockSpec((CHUNK, A_GROUPS), const),
            pl.BlockSpec((A_WIDTH, D_MODEL), const, pipeline_mode=pl.Buffered(1)),
        ],
        out_specs=pl.BlockSpec((tm, D_MODEL), lambda i: (i, 0)),
        scratch_shapes=[
            pltpu.VMEM((tm, A_WIDTH), _BF16),
            pltpu.VMEM((tm, A_WIDTH), _BF16),
        ],
        compiler_params=pltpu.CompilerParams(
            dimension_semantics=("arbitrary",), vmem_limit_bytes=VMEM_LIMIT_BYTES),
        name="gmlp_layer",
    )(x2, norm_g, w_in, ln_g, ln_b, ws, bs_t, w_out)


def _layer_b_kernel(sinks_ref, hc_ref, hp_ref, kvg_ref, bg_ref, wkv_ref, bkv_ref, win_ref, bq_ref,
                    wout_ref, fg_ref, cos_ref, sin_ref,
                    o_ref, kt_ref, vx_ref, q_ref, gate_ref, y_ref, *, tiles_per_seq):
    t = pl.program_id(0)
    refs = (sinks_ref, hc_ref, hp_ref, kvg_ref, bg_ref, wkv_ref, bkv_ref, win_ref, bq_ref,
            wout_ref, fg_ref, cos_ref, sin_ref, o_ref, kt_ref, vx_ref, q_ref, gate_ref, y_ref)

    @pl.when(t % 2 == 0)
    def _():
        _layer_b_step(*refs, t=t, slot=0, tiles_per_seq=tiles_per_seq)

    @pl.when(t % 2 == 1)
    def _():
        _layer_b_step(*refs, t=t, slot=1, tiles_per_seq=tiles_per_seq)


def _layer_b_step(sinks_ref, hc_ref, hp_ref, kvg_ref, bg_ref, wkv_ref, bkv_ref, win_ref, bq_ref,
                  wout_ref, fg_ref, cos_ref, sin_ref,
                  o_ref, kt_ref, vx_ref, q_ref, gate_ref, y_ref, *, t, slot, tiles_per_seq):
    tm = hc_ref.shape[0]
    prev = 1 - slot
    new = slice(CHUNK, CHUNK + tm)
    planes = 2 * N_KV_HEADS

    if slot == 0:
        @pl.when(t == 0)
        def _():
            kt_ref[prev] = jnp.zeros((planes, LANES, CHUNK + tm), _BF16)
            vx_ref[prev] = jnp.zeros((planes, CHUNK + tm, 2 * LANES), _BF16)
            q_ref[prev] = jnp.zeros((tm, B_WIDTH), _BF16)
            gate_ref[prev] = jnp.zeros((tm, B_WIDTH), _F32)

    first_in_seq = (t % tiles_per_seq) == 0

    @pl.when(first_in_seq)
    def _():
        kt_ref[slot, :, :, 0:CHUNK] = jnp.zeros((planes, LANES, CHUNK), _BF16)
        vx_ref[slot, :, 0:CHUNK, :] = jnp.zeros((planes, CHUNK, 2 * LANES), _BF16)

    @pl.when(jnp.logical_not(first_in_seq))
    def _():
        kt_ref[slot, :, :, 0:CHUNK] = kt_ref[prev, :, :, tm:tm + CHUNK]
        vx_ref[slot, :, 0:CHUNK, :] = vx_ref[prev, :, tm:tm + CHUNK, :]

    h = hc_ref[...]
    xn = h * lax.rsqrt(jnp.mean(h * h, axis=-1, keepdims=True) + EPS)
    kvn = (xn * kvg_ref[...]).astype(_BF16)
    h2 = (xn * bg_ref[...]).astype(_BF16)

    cos = cos_ref[...]
    sin = sin_ref[...]
    lane = lax.broadcasted_iota(jnp.int32, (tm, LANES), 1)
    first_half = (lane % HEAD_DIM) < (HEAD_DIM // 2)
    low_head = lane < HEAD_DIM

    def rope(x):
        swapped = jnp.where(first_half,
                            pltpu.roll(x, LANES - HEAD_DIM // 2, 1),
                            pltpu.roll(x, HEAD_DIM // 2, 1))
        return x * cos + swapped * sin

    def project_kv():
        kv = _dot(kvn, wkv_ref[...]) + bkv_ref[...]
        k_t = rope(kv[:, :KV_WIDTH]).T
        zero_rows = jnp.zeros((HEAD_DIM, tm), _F32)
        for kvh in range(N_KV_HEADS):
            head_t = k_t[kvh * HEAD_DIM:(kvh + 1) * HEAD_DIM]
            kt_ref[slot, 2 * kvh, :, new] = jnp.concatenate([head_t, zero_rows], axis=0).astype(_BF16)
            kt_ref[slot, 2 * kvh + 1, :, new] = jnp.concatenate([zero_rows, head_t], axis=0).astype(_BF16)
        v = kv[:, KV_WIDTH:]
        v_sw = pltpu.roll(v, HEAD_DIM, 1)
        zero = jnp.zeros_like(v)
        one = jnp.ones_like(v)
        ones_lo = jnp.where(low_head, one, zero)
        ones_hi = jnp.where(low_head, zero, one)
        v_low = (jnp.where(low_head, v, zero), jnp.where(low_head, v_sw, zero))
        v_high = (jnp.where(low_head, zero, v_sw), jnp.where(low_head, zero, v))
        for kvh in range(N_KV_HEADS):
            vx_ref[slot, 2 * kvh, new, :] = jnp.concatenate([v_low[kvh], ones_lo], axis=1).astype(_BF16)
            vx_ref[slot, 2 * kvh + 1, new, :] = jnp.concatenate([v_high[kvh], ones_hi], axis=1).astype(_BF16)

    q_scale = (HEAD_DIM ** -0.5) * LOG2_E

    def project_q(b):
        cols = slice(b * P_BLOCK, (b + 1) * P_BLOCK)
        zq = _dot(h2, win_ref[:, cols]) + bq_ref[:, cols]
        for i in range(P_BLOCK // LANES):
            s = b * (P_BLOCK // LANES) + i
            q_ref[slot, :, s * LANES:(s + 1) * LANES] = (
                rope(zq[:, i * LANES:(i + 1) * LANES]) * q_scale).astype(_BF16)

    def project_gate(b):
        cols = slice(b * P_BLOCK, (b + 1) * P_BLOCK)
        gate_ref[slot, :, cols] = _dot(h2, win_ref[:, B_WIDTH + b * P_BLOCK:B_WIDTH + (b + 1) * P_BLOCK])

    qi = lax.broadcasted_iota(jnp.int32, (CHUNK, 2 * CHUNK), 0)
    kj = lax.broadcasted_iota(jnp.int32, (CHUNK, 2 * CHUNK), 1)
    in_window = (kj > qi) & (kj <= qi + CHUNK)
    neg_inf = jnp.float32(-jnp.inf)
    bias = jnp.where(in_window, 0.0, neg_inf)
    prev_first_in_seq = ((t + tiles_per_seq - 1) % tiles_per_seq) == 0
    bias_first = jnp.where(in_window & ((kj >= CHUNK) | jnp.logical_not(prev_first_in_seq)), 0.0, neg_inf)
    low_head_c = lax.broadcasted_iota(jnp.int32, (CHUNK, LANES), 1) < HEAD_DIM

    units = [(c, kvh) for kvh in range(N_KV_HEADS) for c in range(tm // CHUNK)]

    def scores(unit):
        c, kvh = unit
        rows = slice(c * CHUNK, (c + 1) * CHUNK)
        band = slice(c * CHUNK, (c + 2) * CHUNK)
        q_stack = jnp.concatenate(
            [q_ref[prev, rows, s * LANES:(s + 1) * LANES]
             for s in range(kvh * SLABS_PER_KV, (kvh + 1) * SLABS_PER_KV)], axis=0)
        return [_dot(q_stack, kt_ref[prev, 2 * kvh + half, :, band]) for half in range(HEADS_PER_SLAB)]

    def softmax(unit, sc_halves):
        c, kvh = unit
        chunk_bias = bias_first if c == 0 else bias
        probs = []
        sink_terms = []
        for half, sc in enumerate(sc_halves):
            p_parts = []
            e_parts = []
            for si in range(SLABS_PER_KV):
                s = kvh * SLABS_PER_KV + si
                sink = sinks_ref[s * HEADS_PER_SLAB + half] * LOG2_E
                sb = sc[si * CHUNK:(si + 1) * CHUNK] + chunk_bias
                m = jnp.maximum(jnp.max(sb, axis=-1, keepdims=True), sink)
                p_parts.append(jnp.exp2(sb - m).astype(_BF16))
                e_parts.append(jnp.exp2(sink - m))
            probs.append(jnp.concatenate(p_parts, axis=0))
            sink_terms.append(e_parts)
        return probs, sink_terms

    def attend(unit, probs, sink_terms):
        c, kvh = unit
        rows = slice(c * CHUNK, (c + 1) * CHUNK)
        band = slice(c * CHUNK, (c + 2) * CHUNK)
        od = (_dot(probs[0], vx_ref[prev, 2 * kvh, band, :])
              + _dot(probs[1], vx_ref[prev, 2 * kvh + 1, band, :]))
        for si in range(SLABS_PER_KV):
            s = kvh * SLABS_PER_KV + si
            blk = slice(si * CHUNK, (si + 1) * CHUNK)
            denom = od[blk, LANES:] + jnp.where(low_head_c, sink_terms[0][si], sink_terms[1][si])
            gate = gate_ref[prev, rows, s * LANES:(s + 1) * LANES]
            y_ref[rows, s * LANES:(s + 1) * LANES] = (
                od[blk, :LANES] * gate / (denom * (1.0 + jnp.exp(-gate)))).astype(_BF16)

    def out_part(kvh):
        cols = slice(kvh * SLABS_PER_KV * LANES, (kvh + 1) * SLABS_PER_KV * LANES)
        return _dot(y_ref[:, cols], wout_ref[cols, :])

    s0 = scores(units[0])
    s1 = scores(units[1])
    project_kv()
    project_q(0)
    p0 = softmax(units[0], s0)
    attend(units[0], *p0)
    project_q(1)
    p1 = softmax(units[1], s1)
    attend(units[1], *p1)
    s2 = scores(units[2])
    project_q(2)
    s3 = scores(units[3])
    project_q(3)
    acc = out_part(0)
    p2 = softmax(units[2], s2)
    attend(units[2], *p2)
    project_gate(0)
    p3 = softmax(units[3], s3)
    attend(units[3], *p3)
    project_gate(1)
    project_gate(2)
    out = hp_ref[...] + acc + out_part(1)
    o_ref[...] = out * lax.rsqrt(jnp.mean(out * out, axis=-1, keepdims=True) + EPS) * fg_ref[...]
    project_gate(3)


def _layer_b(h2d, seq, sinks, kv_g, b_g, w_kv, b_kv, w_in, b_q, w_out, f_g, cos, sin):
    tm = TOKENS_PER_STEP
    n_tiles = h2d.shape[0] // tm
    tiles_per_seq = seq // tm
    const = lambda t: (0, 0)
    cur = lambda t: (jnp.minimum(t, n_tiles - 1), 0)
    prv = lambda t: (jnp.maximum(t - 1, 0), 0)
    tab = lambda t: (jnp.minimum(t, n_tiles - 1) % tiles_per_seq, 0)
    planes = 2 * N_KV_HEADS
    return pl.pallas_call(
        functools.partial(_layer_b_kernel, tiles_per_seq=tiles_per_seq),
        out_shape=jax.ShapeDtypeStruct(h2d.shape, _F32),
        grid=(n_tiles + 1,),
        in_specs=[
            pl.BlockSpec(memory_space=pltpu.SMEM),
            pl.BlockSpec((tm, D_MODEL), cur),
            pl.BlockSpec((tm, D_MODEL), prv),
            pl.BlockSpec((1, D_MODEL), const),
            pl.BlockSpec((1, D_MODEL), const),
            pl.BlockSpec((D_MODEL, 2 * KV_WIDTH), const),
            pl.BlockSpec((1, 2 * KV_WIDTH), const),
            pl.BlockSpec((D_MODEL, 2 * B_WIDTH), const),
            pl.BlockSpec((1, B_WIDTH), const),
            pl.BlockSpec((B_WIDTH, D_MODEL), const),
            pl.BlockSpec((1, D_MODEL), const),
            pl.BlockSpec((tm, LANES), tab),
            pl.BlockSpec((tm, LANES), tab),
        ],
        out_specs=pl.BlockSpec((tm, D_MODEL), prv),
        scratch_shapes=[
            pltpu.VMEM((2, planes, LANES, CHUNK + tm), _BF16),
            pltpu.VMEM((2, planes, CHUNK + tm, 2 * LANES), _BF16),
            pltpu.VMEM((2, tm, B_WIDTH), _BF16),
            pltpu.VMEM((2, tm, B_WIDTH), _F32),
            pltpu.VMEM((tm, B_WIDTH), _BF16),
        ],
        compiler_params=pltpu.CompilerParams(
            dimension_semantics=("arbitrary",), vmem_limit_bytes=VMEM_LIMIT_BYTES),
        name="swa_layer",
    )(sinks, h2d, h2d, kv_g, b_g, w_kv, b_kv, w_in, b_q, w_out, f_g, cos, sin)


def _rope_tables(seq):
    inv_freq = ROPE_THETA ** (-jnp.arange(0, HEAD_DIM, 2, dtype=_F32) / HEAD_DIM)
    ang = jnp.arange(seq, dtype=_F32)[:, None] * inv_freq[None, :]
    cos = jnp.cos(ang)
    sin = jnp.sin(ang)
    cos_slab = jnp.tile(jnp.concatenate([cos, cos], axis=-1), (1, HEADS_PER_SLAB))
    sin_slab = jnp.tile(jnp.concatenate([-sin, sin], axis=-1), (1, HEADS_PER_SLAB))
    assert cos_slab.shape == (seq, LANES)
    return cos_slab, sin_slab


def kernel(x, a_norm_g, a_w_in, a_ln_g, a_ln_b, a_ws, a_bs, a_w_out, kv_norm_g, w_kv, b_kv,
           b_norm_g, b_w_in, b_bq, b_sinks, b_w_out, final_norm_g):
    batch, seq, d = x.shape
    assert d == D_MODEL and seq % TOKENS_PER_STEP == 0 and TOKENS_PER_STEP % CHUNK == 0
    assert a_w_in.shape[0] == 1 and b_w_in.shape[0] == 1
    row = lambda p: p.reshape(1, -1)

    x2 = x.reshape(batch * seq, d)
    h = _layer_a(x2, row(a_norm_g[0]), a_w_in[0].astype(_BF16), row(a_ln_g[0]), row(a_ln_b[0]),
                 a_ws[0], a_bs[0].T, a_w_out[0].astype(_BF16))
    cos, sin = _rope_tables(seq)
    out = _layer_b(h, seq, b_sinks[0], row(kv_norm_g), row(b_norm_g[0]),
                   w_kv.astype(_BF16), row(b_kv), b_w_in[0].astype(_BF16), row(b_bq[0]),
                   b_w_out[0].astype(_BF16), row(final_norm_g), cos, sin)
    return out.reshape(batch, seq, d)
```

```python
import functools

import jax
import jax.numpy as jnp
from jax import lax
from jax.experimental import pallas as pl
from jax.experimental.pallas import tpu as pltpu

D_MODEL = 1024
CHUNK = 128
A_WIDTH = 2048
A_GROUPS = 16
HEAD_DIM = 64
N_Q_HEADS = 16
N_KV_HEADS = 2
B_WIDTH = 1024
KV_WIDTH = 128
ROPE_THETA = 10000.0
EPS = 1e-5
LOG2_E = 1.4426950408889634

LANES = 128
HEADS_PER_SLAB = LANES // HEAD_DIM
N_SLABS = B_WIDTH // LANES
SLABS_PER_KV = N_SLABS // N_KV_HEADS

TOKENS_PER_STEP = 256
P_BLOCK = 256
A_COL_BLOCK = 512
VMEM_LIMIT_BYTES = 56 * 1024 * 1024

_F32 = jnp.float32
_BF16 = jnp.bfloat16


def _dot(a, b):
    return jnp.dot(a, b, preferred_element_type=_F32)


def _silu(g):
    return g / (1.0 + jnp.exp(-g))


def _layer_a_kernel(x_ref, ng_ref, win_ref, lng_ref, lnb_ref, ws_ref, bst_ref, wout_ref,
                    o_ref, vn_ref, y_ref):
    tm = x_ref.shape[0]
    x = x_ref[...]
    h1 = (x * lax.rsqrt(jnp.mean(x * x, axis=-1, keepdims=True) + EPS) * ng_ref[...]).astype(_BF16)

    v = _dot(h1, win_ref[:, A_WIDTH:2 * A_WIDTH])
    mu = jnp.mean(v, axis=-1, keepdims=True)
    vc = v - mu
    var = jnp.mean(vc * vc, axis=-1, keepdims=True)
    vn_ref[...] = (vc * lax.rsqrt(var + EPS) * lng_ref[...] + lnb_ref[...]).astype(_BF16)

    row = lax.broadcasted_iota(jnp.int32, (CHUNK, CHUNK), 0)
    col = lax.broadcasted_iota(jnp.int32, (CHUNK, CHUNK), 1)
    causal = col <= row

    groups_per_block = A_COL_BLOCK // CHUNK
    for j in range(A_WIDTH // A_COL_BLOCK):
        c0 = j * A_COL_BLOCK
        u = _dot(h1, win_ref[:, c0:c0 + A_COL_BLOCK])
        gt = _dot(h1, win_ref[:, 2 * A_WIDTH + c0:2 * A_WIDTH + c0 + A_COL_BLOCK])
        for gi in range(groups_per_block):
            g = j * groups_per_block + gi
            wsm = jnp.where(causal, ws_ref[g], 0.0).astype(_BF16)
            bias = jnp.broadcast_to(bst_ref[:, g:g + 1], (CHUNK, CHUNK))
            for c in range(tm // CHUNK):
                rows = slice(c * CHUNK, (c + 1) * CHUNK)
                cols = slice(gi * CHUNK, (gi + 1) * CHUNK)
                sv = _dot(wsm, vn_ref[rows, g * CHUNK:(g + 1) * CHUNK]) + bias
                y_ref[rows, g * CHUNK:(g + 1) * CHUNK] = (
                    u[rows, cols] * sv * _silu(gt[rows, cols])).astype(_BF16)

    o_ref[...] = x + _dot(y_ref[...], wout_ref[...])


def _layer_a(x2, norm_g, w_in, ln_g, ln_b, ws, bs_t, w_out):
    t = x2.shape[0]
    tm = TOKENS_PER_STEP
    const = lambda i: (0, 0)
    return pl.pallas_call(
        _layer_a_kernel,
        out_shape=jax.ShapeDtypeStruct((t, D_MODEL), _F32),
        grid=(t // tm,),
        in_specs=[
            pl.BlockSpec((tm, D_MODEL), lambda i: (i, 0)),
            pl.BlockSpec((1, D_MODEL), const),
            pl.BlockSpec((D_MODEL, 3 * A_WIDTH), const, pipeline_mode=pl.Buffered(1)),
            pl.BlockSpec((1, A_WIDTH), const),
            pl.BlockSpec((1, A_WIDTH), const),
            pl.BlockSpec((A_GROUPS, CHUNK, CHUNK), lambda i: (0, 0, 0)),
            pl.BlockSpec((CHUNK, A_GROUPS), const),
            pl.BlockSpec((A_WIDTH, D_MODEL), const, pipeline_mode=pl.Buffered(1)),
        ],
        out_specs=pl.BlockSpec((tm, D_MODEL), lambda i: (i, 0)),
        scratch_shapes=[
            pltpu.VMEM((tm, A_WIDTH), _BF16),
            pltpu.VMEM((tm, A_WIDTH), _BF16),
        ],
        compiler_params=pltpu.CompilerParams(
            dimension_semantics=("arbitrary",), vmem_limit_bytes=VMEM_LIMIT_BYTES),
        name="gmlp_layer",
    )(x2, norm_g, w_in, ln_g, ln_b, ws, bs_t, w_out)


def _layer_b_kernel(sinks_ref, hc_ref, hp_ref, kvg_ref, bg_ref, wkv_ref, bkv_ref, win_ref, bq_ref,
                    wout_ref, fg_ref, cos_ref, sin_ref,
                    o_ref, kt_ref, vx_ref, q_ref, gate_ref, y_ref, *, tiles_per_seq):
    t = pl.program_id(0)
    refs = (sinks_ref, hc_ref, hp_ref, kvg_ref, bg_ref, wkv_ref, bkv_ref, win_ref, bq_ref,
            wout_ref, fg_ref, cos_ref, sin_ref, o_ref, kt_ref, vx_ref, q_ref, gate_ref, y_ref)

    @pl.when(t % 2 == 0)
    def _():
        _layer_b_step(*refs, t=t, slot=0, tiles_per_seq=tiles_per_seq)

    @pl.when(t % 2 == 1)
    def _():
        _layer_b_step(*refs, t=t, slot=1, tiles_per_seq=tiles_per_seq)


def _layer_b_step(sinks_ref, hc_ref, hp_ref, kvg_ref, bg_ref, wkv_ref, bkv_ref, win_ref, bq_ref,
                  wout_ref, fg_ref, cos_ref, sin_ref,
                  o_ref, kt_ref, vx_ref, q_ref, gate_ref, y_ref, *, t, slot, tiles_per_seq):
    tm = hc_ref.shape[0]
    prev = 1 - slot
    new = slice(CHUNK, CHUNK + tm)
    planes = 2 * N_KV_HEADS

    if slot == 0:
        @pl.when(t == 0)
        def _():
            kt_ref[prev] = jnp.zeros((planes, LANES, CHUNK + tm), _BF16)
            vx_ref[prev] = jnp.zeros((planes, CHUNK + tm, 2 * LANES), _BF16)
            q_ref[prev] = jnp.zeros((tm, B_WIDTH), _BF16)
            gate_ref[prev] = jnp.zeros((tm, B_WIDTH), _F32)

    first_in_seq = (t % tiles_per_seq) == 0

    @pl.when(first_in_seq)
    def _():
        kt_ref[slot, :, :, 0:CHUNK] = jnp.zeros((planes, LANES, CHUNK), _BF16)
        vx_ref[slot, :, 0:CHUNK, :] = jnp.zeros((planes, CHUNK, 2 * LANES), _BF16)

    @pl.when(jnp.logical_not(first_in_seq))
    def _():
        kt_ref[slot, :, :, 0:CHUNK] = kt_ref[prev, :, :, tm:tm + CHUNK]
        vx_ref[slot, :, 0:CHUNK, :] = vx_ref[prev, :, tm:tm + CHUNK, :]

    h = hc_ref[...]
    xn = h * lax.rsqrt(jnp.mean(h * h, axis=-1, keepdims=True) + EPS)
    kvn = (xn * kvg_ref[...]).astype(_BF16)
    h2 = (xn * bg_ref[...]).astype(_BF16)

    cos = cos_ref[...]
    sin = sin_ref[...]
    lane = lax.broadcasted_iota(jnp.int32, (tm, LANES), 1)
    first_half = (lane % HEAD_DIM) < (HEAD_DIM // 2)
    low_head = lane < HEAD_DIM

    def rope(x):
        swapped = jnp.where(first_half,
                            pltpu.roll(x, LANES - HEAD_DIM // 2, 1),
                            pltpu.roll(x, HEAD_DIM // 2, 1))
        return x * cos + swapped * sin

    def project_kv():
        kv = _dot(kvn, wkv_ref[...]) + bkv_ref[...]
        k_t = rope(kv[:, :KV_WIDTH]).T
        zero_rows = jnp.zeros((HEAD_DIM, tm), _F32)
        for kvh in range(N_KV_HEADS):
            head_t = k_t[kvh * HEAD_DIM:(kvh + 1) * HEAD_DIM]
            kt_ref[slot, 2 * kvh, :, new] = jnp.concatenate([head_t, zero_rows], axis=0).astype(_BF16)
            kt_ref[slot, 2 * kvh + 1, :, new] = jnp.concatenate([zero_rows, head_t], axis=0).astype(_BF16)
        v = kv[:, KV_WIDTH:]
        v_sw = pltpu.roll(v, HEAD_DIM, 1)
        zero = jnp.zeros_like(v)
        one = jnp.ones_like(v)
        ones_lo = jnp.where(low_head, one, zero)
        ones_hi = jnp.where(low_head, zero, one)
        v_low = (jnp.where(low_head, v, zero), jnp.where(low_head, v_sw, zero))
        v_high = (jnp.where(low_head, zero, v_sw), jnp.where(low_head, zero, v))
        for kvh in range(N_KV_HEADS):
            vx_ref[slot, 2 * kvh, new, :] = jnp.concatenate([v_low[kvh], ones_lo], axis=1).astype(_BF16)
            vx_ref[slot, 2 * kvh + 1, new, :] = jnp.concatenate([v_high[kvh], ones_hi], axis=1).astype(_BF16)

    q_scale = (HEAD_DIM ** -0.5) * LOG2_E

    def project_q(b):
        cols = slice(b * P_BLOCK, (b + 1) * P_BLOCK)
        zq = _dot(h2, win_ref[:, cols]) + bq_ref[:, cols]
        for i in range(P_BLOCK // LANES):
            s = b * (P_BLOCK // LANES) + i
            q_ref[slot, :, s * LANES:(s + 1) * LANES] = (
                rope(zq[:, i * LANES:(i + 1) * LANES]) * q_scale).astype(_BF16)

    def project_gate(b):
        cols = slice(b * P_BLOCK, (b + 1) * P_BLOCK)
        gate_ref[slot, :, cols] = _dot(h2, win_ref[:, B_WIDTH + b * P_BLOCK:B_WIDTH + (b + 1) * P_BLOCK])

    qi = lax.broadcasted_iota(jnp.int32, (CHUNK, 2 * CHUNK), 0)
    kj = lax.broadcasted_iota(jnp.int32, (CHUNK, 2 * CHUNK), 1)
    in_window = (kj > qi) & (kj <= qi + CHUNK)
    neg_inf = jnp.float32(-jnp.inf)
    bias = jnp.where(in_window, 0.0, neg_inf)
    prev_first_in_seq = ((t + tiles_per_seq - 1) % tiles_per_seq) == 0
    bias_first = jnp.where(in_window & ((kj >= CHUNK) | jnp.logical_not(prev_first_in_seq)), 0.0, neg_inf)
    low_head_c = lax.broadcasted_iota(jnp.int32, (CHUNK, LANES), 1) < HEAD_DIM

    units = [(c, kvh) for kvh in range(N_KV_HEADS) for c in range(tm // CHUNK)]
    assert len(units) == 4 and B_WIDTH // P_BLOCK == 4, "the program order below is written for 4 + 4"

    def scores(unit):
        c, kvh = unit
        rows = slice(c * CHUNK, (c + 1) * CHUNK)
        band = slice(c * CHUNK, (c + 2) * CHUNK)
        q_stack = jnp.concatenate(
            [q_ref[prev, rows, s * LANES:(s + 1) * LANES]
             for s in range(kvh * SLABS_PER_KV, (kvh + 1) * SLABS_PER_KV)], axis=0)
        return [_dot(q_stack, kt_ref[prev, 2 * kvh + half, :, band]) for half in range(HEADS_PER_SLAB)]

    def softmax(unit, sc_halves):
        c, kvh = unit
        chunk_bias = bias_first if c == 0 else bias
        probs = []
        sink_terms = []
        for half, sc in enumerate(sc_halves):
            p_parts = []
            e_parts = []
            for si in range(SLABS_PER_KV):
                s = kvh * SLABS_PER_KV + si
                sink = sinks_ref[s * HEADS_PER_SLAB + half] * LOG2_E
                sb = sc[si * CHUNK:(si + 1) * CHUNK] + chunk_bias
                m = jnp.maximum(jnp.max(sb, axis=-1, keepdims=True), sink)
                p_parts.append(jnp.exp2(sb - m).astype(_BF16))
                e_parts.append(jnp.exp2(sink - m))
            probs.append(jnp.concatenate(p_parts, axis=0))
            sink_terms.append(e_parts)
        return probs, sink_terms

    def attend(unit, probs, sink_terms):
        c, kvh = unit
        rows = slice(c * CHUNK, (c + 1) * CHUNK)
        band = slice(c * CHUNK, (c + 2) * CHUNK)
        od = (_dot(probs[0], vx_ref[prev, 2 * kvh, band, :])
              + _dot(probs[1], vx_ref[prev, 2 * kvh + 1, band, :]))
        for si in range(SLABS_PER_KV):
            s = kvh * SLABS_PER_KV + si
            blk = slice(si * CHUNK, (si + 1) * CHUNK)
            denom = od[blk, LANES:] + jnp.where(low_head_c, sink_terms[0][si], sink_terms[1][si])
            gate = gate_ref[prev, rows, s * LANES:(s + 1) * LANES]
            y_ref[rows, s * LANES:(s + 1) * LANES] = (
                od[blk, :LANES] * gate / (denom * (1.0 + jnp.exp(-gate)))).astype(_BF16)

    def out_part(kvh):
        cols = slice(kvh * SLABS_PER_KV * LANES, (kvh + 1) * SLABS_PER_KV * LANES)
        return _dot(y_ref[:, cols], wout_ref[cols, :])

    s0 = scores(units[0])
    project_kv()
    s1 = scores(units[1])
    project_q(0)
    p0 = softmax(units[0], s0)
    attend(units[0], *p0)
    s2 = scores(units[2])
    project_q(1)
    p1 = softmax(units[1], s1)
    attend(units[1], *p1)
    s3 = scores(units[3])
    project_q(2)
    p2 = softmax(units[2], s2)
    attend(units[2], *p2)
    project_q(3)
    project_gate(0)
    acc = out_part(0)
    p3 = softmax(units[3], s3)
    attend(units[3], *p3)
    project_gate(1)
    out = hp_ref[...] + acc + out_part(1)
    o_ref[...] = out * lax.rsqrt(jnp.mean(out * out, axis=-1, keepdims=True) + EPS) * fg_ref[...]
    project_gate(2)
    project_gate(3)


def _layer_b(h2d, seq, sinks, kv_g, b_g, w_kv, b_kv, w_in, b_q, w_out, f_g, cos, sin):
    tm = TOKENS_PER_STEP
    n_tiles = h2d.shape[0] // tm
    tiles_per_seq = seq // tm
    const = lambda t: (0, 0)
    cur = lambda t: (jnp.minimum(t, n_tiles - 1), 0)
    prv = lambda t: (jnp.maximum(t - 1, 0), 0)
    tab = lambda t: (jnp.minimum(t, n_tiles - 1) % tiles_per_seq, 0)
    planes = 2 * N_KV_HEADS
    return pl.pallas_call(
        functools.partial(_layer_b_kernel, tiles_per_seq=tiles_per_seq),
        out_shape=jax.ShapeDtypeStruct(h2d.shape, _F32),
        grid=(n_tiles + 1,),
        in_specs=[
            pl.BlockSpec(memory_space=pltpu.SMEM),
            pl.BlockSpec((tm, D_MODEL), cur),
            pl.BlockSpec((tm, D_MODEL), prv),
            pl.BlockSpec((1, D_MODEL), const),
            pl.BlockSpec((1, D_MODEL), const),
            pl.BlockSpec((D_MODEL, 2 * KV_WIDTH), const),
            pl.BlockSpec((1, 2 * KV_WIDTH), const),
            pl.BlockSpec((D_MODEL, 2 * B_WIDTH), const),
            pl.BlockSpec((1, B_WIDTH), const),
            pl.BlockSpec((B_WIDTH, D_MODEL), const),
            pl.BlockSpec((1, D_MODEL), const),
            pl.BlockSpec((tm, LANES), tab),
            pl.BlockSpec((tm, LANES), tab),
        ],
        out_specs=pl.BlockSpec((tm, D_MODEL), prv),
        scratch_shapes=[
            pltpu.VMEM((2, planes, LANES, CHUNK + tm), _BF16),
            pltpu.VMEM((2, planes, CHUNK + tm, 2 * LANES), _BF16),
            pltpu.VMEM((2, tm, B_WIDTH), _BF16),
            pltpu.VMEM((2, tm, B_WIDTH), _F32),
            pltpu.VMEM((tm, B_WIDTH), _BF16),
        ],
        compiler_params=pltpu.CompilerParams(
            dimension_semantics=("arbitrary",), vmem_limit_bytes=VMEM_LIMIT_BYTES),
        name="swa_layer",
    )(sinks, h2d, h2d, kv_g, b_g, w_kv, b_kv, w_in, b_q, w_out, f_g, cos, sin)


def _rope_tables(seq):
    inv_freq = ROPE_THETA ** (-jnp.arange(0, HEAD_DIM, 2, dtype=_F32) / HEAD_DIM)
    ang = jnp.arange(seq, dtype=_F32)[:, None] * inv_freq[None, :]
    cos = jnp.cos(ang)
    sin = jnp.sin(ang)
    cos_slab = jnp.tile(jnp.concatenate([cos, cos], axis=-1), (1, HEADS_PER_SLAB))
    sin_slab = jnp.tile(jnp.concatenate([-sin, sin], axis=-1), (1, HEADS_PER_SLAB))
    assert cos_slab.shape == (seq, LANES)
    return cos_slab, sin_slab


def kernel(x, a_norm_g, a_w_in, a_ln_g, a_ln_b, a_ws, a_bs, a_w_out, kv_norm_g, w_kv, b_kv,
           b_norm_g, b_w_in, b_bq, b_sinks, b_w_out, final_norm_g):
    batch, seq, d = x.shape
    assert d == D_MODEL and seq % TOKENS_PER_STEP == 0 and TOKENS_PER_STEP % CHUNK == 0
    assert a_w_in.shape[0] == 1 and b_w_in.shape[0] == 1
    row = lambda p: p.reshape(1, -1)

    x2 = x.reshape(batch * seq, d)
    h = _layer_a(x2, row(a_norm_g[0]), a_w_in[0].astype(_BF16), row(a_ln_g[0]), row(a_ln_b[0]),
                 a_ws[0], a_bs[0].T, a_w_out[0].astype(_BF16))
    cos, sin = _rope_tables(seq)
    out = _layer_b(h, seq, b_sinks[0], row(kv_norm_g), row(b_norm_g[0]),
                   w_kv.astype(_BF16), row(b_kv), b_w_in[0].astype(_BF16), row(b_bq[0]),
                   b_w_out[0].astype(_BF16), row(final_norm_g), cos, sin)
    return out.reshape(batch, seq, d)
```

```python
import functools

import jax
import jax.numpy as jnp
from jax import lax
from jax.experimental import pallas as pl
from jax.experimental.pallas import tpu as pltpu

D_MODEL = 1024
CHUNK = 128
A_WIDTH = 2048
A_GROUPS = 16
HEAD_DIM = 64
N_Q_HEADS = 16
N_KV_HEADS = 2
B_WIDTH = 1024
KV_WIDTH = 128
ROPE_THETA = 10000.0
EPS = 1e-5
LOG2_E = 1.4426950408889634

LANES = 128
HEADS_PER_SLAB = LANES // HEAD_DIM
N_SLABS = B_WIDTH // LANES
SLABS_PER_KV = N_SLABS // N_KV_HEADS

A_TOKENS_PER_STEP = 512
B_TOKENS_PER_STEP = 256
P_BLOCK = 256
A_COL_BLOCK = 512
VMEM_LIMIT_BYTES = 56 * 1024 * 1024

_F32 = jnp.float32
_BF16 = jnp.bfloat16


def _dot(a, b):
    return jnp.dot(a, b, preferred_element_type=_F32)


def _silu(g):
    return g / (1.0 + jnp.exp(-g))


def _layer_a_kernel(x_ref, ng_ref, win_ref, lng_ref, lnb_ref, ws_ref, bst_ref, wout_ref,
                    o_ref, vn_ref, y_ref):
    tm = x_ref.shape[0]
    x = x_ref[...]
    h1 = (x * lax.rsqrt(jnp.mean(x * x, axis=-1, keepdims=True) + EPS) * ng_ref[...]).astype(_BF16)

    v = _dot(h1, win_ref[:, A_WIDTH:2 * A_WIDTH])
    mu = jnp.mean(v, axis=-1, keepdims=True)
    vc = v - mu
    var = jnp.mean(vc * vc, axis=-1, keepdims=True)
    vn_ref[...] = (vc * lax.rsqrt(var + EPS) * lng_ref[...] + lnb_ref[...]).astype(_BF16)

    row = lax.broadcasted_iota(jnp.int32, (CHUNK, CHUNK), 0)
    col = lax.broadcasted_iota(jnp.int32, (CHUNK, CHUNK), 1)
    causal = col <= row

    groups_per_block = A_COL_BLOCK // CHUNK
    for j in range(A_WIDTH // A_COL_BLOCK):
        c0 = j * A_COL_BLOCK
        u = _dot(h1, win_ref[:, c0:c0 + A_COL_BLOCK])
        gt = _dot(h1, win_ref[:, 2 * A_WIDTH + c0:2 * A_WIDTH + c0 + A_COL_BLOCK])
        for gi in range(groups_per_block):
            g = j * groups_per_block + gi
            wsm = jnp.where(causal, ws_ref[g], 0.0).astype(_BF16)
            bias = jnp.broadcast_to(bst_ref[:, g:g + 1], (CHUNK, CHUNK))
            for c in range(tm // CHUNK):
                rows = slice(c * CHUNK, (c + 1) * CHUNK)
                cols = slice(gi * CHUNK, (gi + 1) * CHUNK)
                sv = _dot(wsm, vn_ref[rows, g * CHUNK:(g + 1) * CHUNK]) + bias
                y_ref[rows, g * CHUNK:(g + 1) * CHUNK] = (
                    u[rows, cols] * sv * _silu(gt[rows, cols])).astype(_BF16)

    o_ref[...] = x + _dot(y_ref[...], wout_ref[...])


def _layer_a(x2, norm_g, w_in, ln_g, ln_b, ws, bs_t, w_out):
    t = x2.shape[0]
    tm = A_TOKENS_PER_STEP
    const = lambda i: (0, 0)
    return pl.pallas_call(
        _layer_a_kernel,
        out_shape=jax.ShapeDtypeStruct((t, D_MODEL), _F32),
        grid=(t // tm,),
        in_specs=[
            pl.BlockSpec((tm, D_MODEL), lambda i: (i, 0)),
            pl.BlockSpec((1, D_MODEL), const),
            pl.BlockSpec((D_MODEL, 3 * A_WIDTH), const, pipeline_mode=pl.Buffered(1)),
            pl.BlockSpec((1, A_WIDTH), const),
            pl.BlockSpec((1, A_WIDTH), const),
            pl.BlockSpec((A_GROUPS, CHUNK, CHUNK), lambda i: (0, 0, 0)),
            pl.BlockSpec((CHUNK, A_GROUPS), const),
            pl.BlockSpec((A_WIDTH, D_MODEL), const, pipeline_mode=pl.Buffered(1)),
        ],
        out_specs=pl.BlockSpec((tm, D_MODEL), lambda i: (i, 0)),
        scratch_shapes=[
            pltpu.VMEM((tm, A_WIDTH), _BF16),
            pltpu.VMEM((tm, A_WIDTH), _BF16),
        ],
        compiler_params=pltpu.CompilerParams(
            dimension_semantics=("arbitrary",), vmem_limit_bytes=VMEM_LIMIT_BYTES),
        name="gmlp_layer",
    )(x2, norm_g, w_in, ln_g, ln_b, ws, bs_t, w_out)


def _layer_b_kernel(sinks_ref, hc_ref, hp_ref, kvg_ref, bg_ref, wkv_ref, bkv_ref, win_ref, bq_ref,
                    wout_ref, fg_ref, cos_ref, sin_ref,
                    o_ref, kt_ref, vx_ref, q_ref, gate_ref, y_ref, *, tiles_per_seq):
    t = pl.program_id(0)
    refs = (sinks_ref, hc_ref, hp_ref, kvg_ref, bg_ref, wkv_ref, bkv_ref, win_ref, bq_ref,
            wout_ref, fg_ref, cos_ref, sin_ref, o_ref, kt_ref, vx_ref, q_ref, gate_ref, y_ref)

    @pl.when(t % 2 == 0)
    def _():
        _layer_b_step(*refs, t=t, slot=0, tiles_per_seq=tiles_per_seq)

    @pl.when(t % 2 == 1)
    def _():
        _layer_b_step(*refs, t=t, slot=1, tiles_per_seq=tiles_per_seq)


def _layer_b_step(sinks_ref, hc_ref, hp_ref, kvg_ref, bg_ref, wkv_ref, bkv_ref, win_ref, bq_ref,
                  wout_ref, fg_ref, cos_ref, sin_ref,
                  o_ref, kt_ref, vx_ref, q_ref, gate_ref, y_ref, *, t, slot, tiles_per_seq):
    tm = hc_ref.shape[0]
    prev = 1 - slot
    new = slice(CHUNK, CHUNK + tm)
    planes = 2 * N_KV_HEADS

    if slot == 0:
        @pl.when(t == 0)
        def _():
            kt_ref[prev] = jnp.zeros((planes, LANES, CHUNK + tm), _BF16)
            vx_ref[prev] = jnp.zeros((planes, CHUNK + tm, 2 * LANES), _BF16)
            q_ref[prev] = jnp.zeros((tm, B_WIDTH), _BF16)
            gate_ref[prev] = jnp.zeros((tm, B_WIDTH), _F32)

    first_in_seq = (t % tiles_per_seq) == 0

    @pl.when(first_in_seq)
    def _():
        kt_ref[slot, :, :, 0:CHUNK] = jnp.zeros((planes, LANES, CHUNK), _BF16)
        vx_ref[slot, :, 0:CHUNK, :] = jnp.zeros((planes, CHUNK, 2 * LANES), _BF16)

    @pl.when(jnp.logical_not(first_in_seq))
    def _():
        kt_ref[slot, :, :, 0:CHUNK] = kt_ref[prev, :, :, tm:tm + CHUNK]
        vx_ref[slot, :, 0:CHUNK, :] = vx_ref[prev, :, tm:tm + CHUNK, :]

    h = hc_ref[...]
    xn = h * lax.rsqrt(jnp.mean(h * h, axis=-1, keepdims=True) + EPS)
    kvn = (xn * kvg_ref[...]).astype(_BF16)
    h2 = (xn * bg_ref[...]).astype(_BF16)

    cos = cos_ref[...]
    sin = sin_ref[...]
    lane = lax.broadcasted_iota(jnp.int32, (tm, LANES), 1)
    first_half = (lane % HEAD_DIM) < (HEAD_DIM // 2)
    low_head = lane < HEAD_DIM

    def rope(x):
        swapped = jnp.where(first_half,
                            pltpu.roll(x, LANES - HEAD_DIM // 2, 1),
                            pltpu.roll(x, HEAD_DIM // 2, 1))
        return x * cos + swapped * sin

    def project_kv():
        kv = _dot(kvn, wkv_ref[...]) + bkv_ref[...]
        k_t = rope(kv[:, :KV_WIDTH]).T
        zero_rows = jnp.zeros((HEAD_DIM, tm), _F32)
        for kvh in range(N_KV_HEADS):
            head_t = k_t[kvh * HEAD_DIM:(kvh + 1) * HEAD_DIM]
            kt_ref[slot, 2 * kvh, :, new] = jnp.concatenate([head_t, zero_rows], axis=0).astype(_BF16)
            kt_ref[slot, 2 * kvh + 1, :, new] = jnp.concatenate([zero_rows, head_t], axis=0).astype(_BF16)
        v = kv[:, KV_WIDTH:]
        v_sw = pltpu.roll(v, HEAD_DIM, 1)
        zero = jnp.zeros_like(v)
        one = jnp.ones_like(v)
        ones_lo = jnp.where(low_head, one, zero)
        ones_hi = jnp.where(low_head, zero, one)
        v_low = (jnp.where(low_head, v, zero), jnp.where(low_head, v_sw, zero))
        v_high = (jnp.where(low_head, zero, v_sw), jnp.where(low_head, zero, v))
        for kvh in range(N_KV_HEADS):
            vx_ref[slot, 2 * kvh, new, :] = jnp.concatenate([v_low[kvh], ones_lo], axis=1).astype(_BF16)
            vx_ref[slot, 2 * kvh + 1, new, :] = jnp.concatenate([v_high[kvh], ones_hi], axis=1).astype(_BF16)

    q_scale = (HEAD_DIM ** -0.5) * LOG2_E

    def project_q(b):
        cols = slice(b * P_BLOCK, (b + 1) * P_BLOCK)
        zq = _dot(h2, win_ref[:, cols]) + bq_ref[:, cols]
        for i in range(P_BLOCK // LANES):
            s = b * (P_BLOCK // LANES) + i
            q_ref[slot, :, s * LANES:(s + 1) * LANES] = (
                rope(zq[:, i * LANES:(i + 1) * LANES]) * q_scale).astype(_BF16)

    def project_gate(b):
        cols = slice(b * P_BLOCK, (b + 1) * P_BLOCK)
        gate_ref[slot, :, cols] = _dot(h2, win_ref[:, B_WIDTH + b * P_BLOCK:B_WIDTH + (b + 1) * P_BLOCK])

    qi = lax.broadcasted_iota(jnp.int32, (CHUNK, 2 * CHUNK), 0)
    kj = lax.broadcasted_iota(jnp.int32, (CHUNK, 2 * CHUNK), 1)
    in_window = (kj > qi) & (kj <= qi + CHUNK)
    neg_inf = jnp.float32(-jnp.inf)
    bias = jnp.where(in_window, 0.0, neg_inf)
    prev_first_in_seq = ((t + tiles_per_seq - 1) % tiles_per_seq) == 0
    bias_first = jnp.where(in_window & ((kj >= CHUNK) | jnp.logical_not(prev_first_in_seq)), 0.0, neg_inf)
    low_head_c = lax.broadcasted_iota(jnp.int32, (CHUNK, LANES), 1) < HEAD_DIM

    units = [(c, kvh) for kvh in range(N_KV_HEADS) for c in range(tm // CHUNK)]
    assert len(units) == 4 and B_WIDTH // P_BLOCK == 4, "the program order below is written for 4 + 4"

    def scores(unit):
        c, kvh = unit
        rows = slice(c * CHUNK, (c + 1) * CHUNK)
        band = slice(c * CHUNK, (c + 2) * CHUNK)
        q_stack = jnp.concatenate(
            [q_ref[prev, rows, s * LANES:(s + 1) * LANES]
             for s in range(kvh * SLABS_PER_KV, (kvh + 1) * SLABS_PER_KV)], axis=0)
        return [_dot(q_stack, kt_ref[prev, 2 * kvh + half, :, band]) for half in range(HEADS_PER_SLAB)]

    def softmax(unit, sc_halves):
        c, kvh = unit
        chunk_bias = bias_first if c == 0 else bias
        probs = []
        sink_terms = []
        for half, sc in enumerate(sc_halves):
            p_parts = []
            e_parts = []
            for si in range(SLABS_PER_KV):
                s = kvh * SLABS_PER_KV + si
                sink = sinks_ref[s * HEADS_PER_SLAB + half] * LOG2_E
                sb = sc[si * CHUNK:(si + 1) * CHUNK] + chunk_bias
                m = jnp.maximum(jnp.max(sb, axis=-1, keepdims=True), sink)
                p_parts.append(jnp.exp2(sb - m).astype(_BF16))
                e_parts.append(jnp.exp2(sink - m))
            probs.append(jnp.concatenate(p_parts, axis=0))
            sink_terms.append(e_parts)
        return probs, sink_terms

    def attend(unit, probs, sink_terms):
        c, kvh = unit
        rows = slice(c * CHUNK, (c + 1) * CHUNK)
        band = slice(c * CHUNK, (c + 2) * CHUNK)
        od = (_dot(probs[0], vx_ref[prev, 2 * kvh, band, :])
              + _dot(probs[1], vx_ref[prev, 2 * kvh + 1, band, :]))
        for si in range(SLABS_PER_KV):
            s = kvh * SLABS_PER_KV + si
            blk = slice(si * CHUNK, (si + 1) * CHUNK)
            denom = od[blk, LANES:] + jnp.where(low_head_c, sink_terms[0][si], sink_terms[1][si])
            gate = gate_ref[prev, rows, s * LANES:(s + 1) * LANES]
            y_ref[rows, s * LANES:(s + 1) * LANES] = (
                od[blk, :LANES] * gate / (denom * (1.0 + jnp.exp(-gate)))).astype(_BF16)

    def out_part(kvh):
        cols = slice(kvh * SLABS_PER_KV * LANES, (kvh + 1) * SLABS_PER_KV * LANES)
        return _dot(y_ref[:, cols], wout_ref[cols, :])

    s0 = scores(units[0])
    project_kv()
    s1 = scores(units[1])
    project_q(0)
    p0 = softmax(units[0], s0)
    attend(units[0], *p0)
    s2 = scores(units[2])
    project_q(1)
    p1 = softmax(units[1], s1)
    attend(units[1], *p1)
    s3 = scores(units[3])
    project_q(2)
    p2 = softmax(units[2], s2)
    attend(units[2], *p2)
    project_q(3)
    project_gate(0)
    acc = out_part(0)
    p3 = softmax(units[3], s3)
    attend(units[3], *p3)
    project_gate(1)
    out = hp_ref[...] + acc + out_part(1)
    o_ref[...] = out * lax.rsqrt(jnp.mean(out * out, axis=-1, keepdims=True) + EPS) * fg_ref[...]
    project_gate(2)
    project_gate(3)


def _layer_b(h2d, seq, sinks, kv_g, b_g, w_kv, b_kv, w_in, b_q, w_out, f_g, cos, sin):
    tm = B_TOKENS_PER_STEP
    n_tiles = h2d.shape[0] // tm
    tiles_per_seq = seq // tm
    const = lambda t: (0, 0)
    cur = lambda t: (jnp.minimum(t, n_tiles - 1), 0)
    prv = lambda t: (jnp.maximum(t - 1, 0), 0)
    tab = lambda t: (jnp.minimum(t, n_tiles - 1) % tiles_per_seq, 0)
    planes = 2 * N_KV_HEADS
    return pl.pallas_call(
        functools.partial(_layer_b_kernel, tiles_per_seq=tiles_per_seq),
        out_shape=jax.ShapeDtypeStruct(h2d.shape, _F32),
        grid=(n_tiles + 1,),
        in_specs=[
            pl.BlockSpec(memory_space=pltpu.SMEM),
            pl.BlockSpec((tm, D_MODEL), cur),
            pl.BlockSpec((tm, D_MODEL), prv),
            pl.BlockSpec((1, D_MODEL), const),
            pl.BlockSpec((1, D_MODEL), const),
            pl.BlockSpec((D_MODEL, 2 * KV_WIDTH), const),
            pl.BlockSpec((1, 2 * KV_WIDTH), const),
            pl.BlockSpec((D_MODEL, 2 * B_WIDTH), const),
            pl.BlockSpec((1, B_WIDTH), const),
            pl.BlockSpec((B_WIDTH, D_MODEL), const),
            pl.BlockSpec((1, D_MODEL), const),
            pl.BlockSpec((tm, LANES), tab),
            pl.BlockSpec((tm, LANES), tab),
        ],
        out_specs=pl.BlockSpec((tm, D_MODEL), prv),
        scratch_shapes=[
            pltpu.VMEM((2, planes, LANES, CHUNK + tm), _BF16),
            pltpu.VMEM((2, planes, CHUNK + tm, 2 * LANES), _BF16),
            pltpu.VMEM((2, tm, B_WIDTH), _BF16),
            pltpu.VMEM((2, tm, B_WIDTH), _F32),
            pltpu.VMEM((tm, B_WIDTH), _BF16),
        ],
        compiler_params=pltpu.CompilerParams(
            dimension_semantics=("arbitrary",), vmem_limit_bytes=VMEM_LIMIT_BYTES),
        name="swa_layer",
    )(sinks, h2d, h2d, kv_g, b_g, w_kv, b_kv, w_in, b_q, w_out, f_g, cos, sin)


def _rope_tables(seq):
    inv_freq = ROPE_THETA ** (-jnp.arange(0, HEAD_DIM, 2, dtype=_F32) / HEAD_DIM)
    ang = jnp.arange(seq, dtype=_F32)[:, None] * inv_freq[None, :]
    cos = jnp.cos(ang)
    sin = jnp.sin(ang)
    cos_slab = jnp.tile(jnp.concatenate([cos, cos], axis=-1), (1, HEADS_PER_SLAB))
    sin_slab = jnp.tile(jnp.concatenate([-sin, sin], axis=-1), (1, HEADS_PER_SLAB))
    assert cos_slab.shape == (seq, LANES)
    return cos_slab, sin_slab


def kernel(x, a_norm_g, a_w_in, a_ln_g, a_ln_b, a_ws, a_bs, a_w_out, kv_norm_g, w_kv, b_kv,
           b_norm_g, b_w_in, b_bq, b_sinks, b_w_out, final_norm_g):
    batch, seq, d = x.shape
    assert d == D_MODEL
    for tile in (A_TOKENS_PER_STEP, B_TOKENS_PER_STEP):
        assert seq % tile == 0 and tile % CHUNK == 0
    assert a_w_in.shape[0] == 1 and b_w_in.shape[0] == 1
    row = lambda p: p.reshape(1, -1)

    x2 = x.reshape(batch * seq, d)
    h = _layer_a(x2, row(a_norm_g[0]), a_w_in[0].astype(_BF16), row(a_ln_g[0]), row(a_ln_b[0]),
                 a_ws[0], a_bs[0].T, a_w_out[0].astype(_BF16))
    cos, sin = _rope_tables(seq)
    out = _layer_b(h, seq, b_sinks[0], row(kv_norm_g), row(b_norm_g[0]),
                   w_kv.astype(_BF16), row(b_kv), b_w_in[0].astype(_BF16), row(b_bq[0]),
                   b_w_out[0].astype(_BF16), row(final_norm_g), cos, sin)
    return out.reshape(batch, seq, d)
```

```python
import functools

import jax
import jax.numpy as jnp
from jax import lax
from jax.experimental import pallas as pl
from jax.experimental.pallas import tpu as pltpu

D_MODEL = 1024
CHUNK = 128
A_WIDTH = 2048
A_GROUPS = 16
HEAD_DIM = 64
N_Q_HEADS = 16
N_KV_HEADS = 2
B_WIDTH = 1024
KV_WIDTH = 128
ROPE_THETA = 10000.0
EPS = 1e-5
LOG2_E = 1.4426950408889634

LANES = 128
HEADS_PER_SLAB = LANES // HEAD_DIM
N_SLABS = B_WIDTH // LANES
SLABS_PER_KV = N_SLABS // N_KV_HEADS

A_TOKENS_PER_STEP = 512
B_TOKENS_PER_STEP = 256
P_BLOCK = 256
A_COL_BLOCK = 512
A_ROW_SPLIT = 2
VMEM_LIMIT_BYTES = 56 * 1024 * 1024

_F32 = jnp.float32
_BF16 = jnp.bfloat16


def _dot(a, b):
    return jnp.dot(a, b, preferred_element_type=_F32)


def _silu(g):
    return g / (1.0 + jnp.exp(-g))


def _layer_a_kernel(x_ref, ng_ref, win_ref, lng_ref, lnb_ref, ws_ref, bst_ref, wout_ref,
                    o_ref, vn_ref, y_ref):
    tm = x_ref.shape[0]
    half = tm // A_ROW_SPLIT
    n_blocks = A_WIDTH // A_COL_BLOCK
    groups_per_block = A_COL_BLOCK // CHUNK

    row = lax.broadcasted_iota(jnp.int32, (CHUNK, CHUNK), 0)
    col = lax.broadcasted_iota(jnp.int32, (CHUNK, CHUNK), 1)
    causal = col <= row

    def normed(r0):
        x = x_ref[r0:r0 + half, :]
        return (x * lax.rsqrt(jnp.mean(x * x, axis=-1, keepdims=True) + EPS) * ng_ref[...]).astype(_BF16)

    def layer_norm_v(r0, v):
        mu = jnp.mean(v, axis=-1, keepdims=True)
        vc = v - mu
        var = jnp.mean(vc * vc, axis=-1, keepdims=True)
        vn_ref[r0:r0 + half, :] = (vc * lax.rsqrt(var + EPS) * lng_ref[...] + lnb_ref[...]).astype(_BF16)

    def gated_block(r0, h1, j):
        c0 = j * A_COL_BLOCK
        u = _dot(h1, win_ref[:, c0:c0 + A_COL_BLOCK])
        gt = _dot(h1, win_ref[:, 2 * A_WIDTH + c0:2 * A_WIDTH + c0 + A_COL_BLOCK])
        for gi in range(groups_per_block):
            g = j * groups_per_block + gi
            wsm = jnp.where(causal, ws_ref[g], 0.0).astype(_BF16)
            bias = jnp.broadcast_to(bst_ref[:, g:g + 1], (CHUNK, CHUNK))
            for c in range(half // CHUNK):
                rows = slice(c * CHUNK, (c + 1) * CHUNK)
                tile_rows = slice(r0 + c * CHUNK, r0 + (c + 1) * CHUNK)
                cols = slice(gi * CHUNK, (gi + 1) * CHUNK)
                sv = _dot(wsm, vn_ref[tile_rows, g * CHUNK:(g + 1) * CHUNK]) + bias
                y_ref[tile_rows, g * CHUNK:(g + 1) * CHUNK] = (
                    u[rows, cols] * sv * _silu(gt[rows, cols])).astype(_BF16)

    starts = [i * half for i in range(A_ROW_SPLIT)]
    h1 = [normed(r0) for r0 in starts]
    v = [_dot(h, win_ref[:, A_WIDTH:2 * A_WIDTH]) for h in h1]
    for r0, vi in zip(starts, v):
        layer_norm_v(r0, vi)
    for j in range(n_blocks):
        for r0, h in zip(starts, h1):
            gated_block(r0, h, j)
    for r0 in starts:
        o_ref[r0:r0 + half, :] = x_ref[r0:r0 + half, :] + _dot(y_ref[r0:r0 + half, :], wout_ref[...])


def _layer_a(x2, norm_g, w_in, ln_g, ln_b, ws, bs_t, w_out):
    t = x2.shape[0]
    tm = A_TOKENS_PER_STEP
    const = lambda i: (0, 0)
    return pl.pallas_call(
        _layer_a_kernel,
        out_shape=jax.ShapeDtypeStruct((t, D_MODEL), _F32),
        grid=(t // tm,),
        in_specs=[
            pl.BlockSpec((tm, D_MODEL), lambda i: (i, 0)),
            pl.BlockSpec((1, D_MODEL), const),
            pl.BlockSpec((D_MODEL, 3 * A_WIDTH), const, pipeline_mode=pl.Buffered(1)),
            pl.BlockSpec((1, A_WIDTH), const),
            pl.BlockSpec((1, A_WIDTH), const),
            pl.BlockSpec((A_GROUPS, CHUNK, CHUNK), lambda i: (0, 0, 0)),
            pl.BlockSpec((CHUNK, A_GROUPS), const),
            pl.BlockSpec((A_WIDTH, D_MODEL), const, pipeline_mode=pl.Buffered(1)),
        ],
        out_specs=pl.BlockSpec((tm, D_MODEL), lambda i: (i, 0)),
        scratch_shapes=[
            pltpu.VMEM((tm, A_WIDTH), _BF16),
            pltpu.VMEM((tm, A_WIDTH), _BF16),
        ],
        compiler_params=pltpu.CompilerParams(
            dimension_semantics=("arbitrary",), vmem_limit_bytes=VMEM_LIMIT_BYTES),
        name="gmlp_layer",
    )(x2, norm_g, w_in, ln_g, ln_b, ws, bs_t, w_out)


def _layer_b_kernel(sinks_ref, hc_ref, hp_ref, kvg_ref, bg_ref, wkv_ref, bkv_ref, win_ref, bq_ref,
                    wout_ref, fg_ref, cos_ref, sin_ref,
                    o_ref, kt_ref, vx_ref, q_ref, gate_ref, y_ref, *, tiles_per_seq):
    t = pl.program_id(0)
    refs = (sinks_ref, hc_ref, hp_ref, kvg_ref, bg_ref, wkv_ref, bkv_ref, win_ref, bq_ref,
            wout_ref, fg_ref, cos_ref, sin_ref, o_ref, kt_ref, vx_ref, q_ref, gate_ref, y_ref)

    @pl.when(t % 2 == 0)
    def _():
        _layer_b_step(*refs, t=t, slot=0, tiles_per_seq=tiles_per_seq)

    @pl.when(t % 2 == 1)
    def _():
        _layer_b_step(*refs, t=t, slot=1, tiles_per_seq=tiles_per_seq)


def _layer_b_step(sinks_ref, hc_ref, hp_ref, kvg_ref, bg_ref, wkv_ref, bkv_ref, win_ref, bq_ref,
                  wout_ref, fg_ref, cos_ref, sin_ref,
                  o_ref, kt_ref, vx_ref, q_ref, gate_ref, y_ref, *, t, slot, tiles_per_seq):
    tm = hc_ref.shape[0]
    prev = 1 - slot
    new = slice(CHUNK, CHUNK + tm)
    planes = 2 * N_KV_HEADS

    if slot == 0:
        @pl.when(t == 0)
        def _():
            kt_ref[prev] = jnp.zeros((planes, LANES, CHUNK + tm), _BF16)
            vx_ref[prev] = jnp.zeros((planes, CHUNK + tm, 2 * LANES), _BF16)
            q_ref[prev] = jnp.zeros((tm, B_WIDTH), _BF16)
            gate_ref[prev] = jnp.zeros((tm, B_WIDTH), _F32)

    first_in_seq = (t % tiles_per_seq) == 0

    @pl.when(first_in_seq)
    def _():
        kt_ref[slot, :, :, 0:CHUNK] = jnp.zeros((planes, LANES, CHUNK), _BF16)
        vx_ref[slot, :, 0:CHUNK, :] = jnp.zeros((planes, CHUNK, 2 * LANES), _BF16)

    @pl.when(jnp.logical_not(first_in_seq))
    def _():
        kt_ref[slot, :, :, 0:CHUNK] = kt_ref[prev, :, :, tm:tm + CHUNK]
        vx_ref[slot, :, 0:CHUNK, :] = vx_ref[prev, :, tm:tm + CHUNK, :]

    h = hc_ref[...]
    xn = h * lax.rsqrt(jnp.mean(h * h, axis=-1, keepdims=True) + EPS)
    kvn = (xn * kvg_ref[...]).astype(_BF16)
    h2 = (xn * bg_ref[...]).astype(_BF16)

    cos = cos_ref[...]
    sin = sin_ref[...]
    lane = lax.broadcasted_iota(jnp.int32, (tm, LANES), 1)
    first_half = (lane % HEAD_DIM) < (HEAD_DIM // 2)
    low_head = lane < HEAD_DIM

    def rope(x):
        swapped = jnp.where(first_half,
                            pltpu.roll(x, LANES - HEAD_DIM // 2, 1),
                            pltpu.roll(x, HEAD_DIM // 2, 1))
        return x * cos + swapped * sin

    def project_kv():
        kv = _dot(kvn, wkv_ref[...]) + bkv_ref[...]
        k_t = rope(kv[:, :KV_WIDTH]).T
        zero_rows = jnp.zeros((HEAD_DIM, tm), _F32)
        for kvh in range(N_KV_HEADS):
            head_t = k_t[kvh * HEAD_DIM:(kvh + 1) * HEAD_DIM]
            kt_ref[slot, 2 * kvh, :, new] = jnp.concatenate([head_t, zero_rows], axis=0).astype(_BF16)
            kt_ref[slot, 2 * kvh + 1, :, new] = jnp.concatenate([zero_rows, head_t], axis=0).astype(_BF16)
        v = kv[:, KV_WIDTH:]
        v_sw = pltpu.roll(v, HEAD_DIM, 1)
        zero = jnp.zeros_like(v)
        one = jnp.ones_like(v)
        ones_lo = jnp.where(low_head, one, zero)
        ones_hi = jnp.where(low_head, zero, one)
        v_low = (jnp.where(low_head, v, zero), jnp.where(low_head, v_sw, zero))
        v_high = (jnp.where(low_head, zero, v_sw), jnp.where(low_head, zero, v))
        for kvh in range(N_KV_HEADS):
            vx_ref[slot, 2 * kvh, new, :] = jnp.concatenate([v_low[kvh], ones_lo], axis=1).astype(_BF16)
            vx_ref[slot, 2 * kvh + 1, new, :] = jnp.concatenate([v_high[kvh], ones_hi], axis=1).astype(_BF16)

    q_scale = (HEAD_DIM ** -0.5) * LOG2_E

    def project_q(b):
        cols = slice(b * P_BLOCK, (b + 1) * P_BLOCK)
        zq = _dot(h2, win_ref[:, cols]) + bq_ref[:, cols]
        for i in range(P_BLOCK // LANES):
            s = b * (P_BLOCK // LANES) + i
            q_ref[slot, :, s * LANES:(s + 1) * LANES] = (
                rope(zq[:, i * LANES:(i + 1) * LANES]) * q_scale).astype(_BF16)

    def project_gate(b):
        cols = slice(b * P_BLOCK, (b + 1) * P_BLOCK)
        gate_ref[slot, :, cols] = _dot(h2, win_ref[:, B_WIDTH + b * P_BLOCK:B_WIDTH + (b + 1) * P_BLOCK])

    qi = lax.broadcasted_iota(jnp.int32, (CHUNK, 2 * CHUNK), 0)
    kj = lax.broadcasted_iota(jnp.int32, (CHUNK, 2 * CHUNK), 1)
    in_window = (kj > qi) & (kj <= qi + CHUNK)
    neg_inf = jnp.float32(-jnp.inf)
    bias = jnp.where(in_window, 0.0, neg_inf)
    prev_first_in_seq = ((t + tiles_per_seq - 1) % tiles_per_seq) == 0
    bias_first = jnp.where(in_window & ((kj >= CHUNK) | jnp.logical_not(prev_first_in_seq)), 0.0, neg_inf)
    low_head_c = lax.broadcasted_iota(jnp.int32, (CHUNK, LANES), 1) < HEAD_DIM

    units = [(c, kvh) for kvh in range(N_KV_HEADS) for c in range(tm // CHUNK)]
    assert len(units) == 4 and B_WIDTH // P_BLOCK == 4, "the program order below is written for 4 + 4"

    def scores(unit):
        c, kvh = unit
        rows = slice(c * CHUNK, (c + 1) * CHUNK)
        band = slice(c * CHUNK, (c + 2) * CHUNK)
        q_stack = jnp.concatenate(
            [q_ref[prev, rows, s * LANES:(s + 1) * LANES]
             for s in range(kvh * SLABS_PER_KV, (kvh + 1) * SLABS_PER_KV)], axis=0)
        return [_dot(q_stack, kt_ref[prev, 2 * kvh + half, :, band]) for half in range(HEADS_PER_SLAB)]

    def softmax(unit, sc_halves):
        c, kvh = unit
        chunk_bias = bias_first if c == 0 else bias
        probs = []
        sink_terms = []
        for half, sc in enumerate(sc_halves):
            p_parts = []
            e_parts = []
            for si in range(SLABS_PER_KV):
                s = kvh * SLABS_PER_KV + si
                sink = sinks_ref[s * HEADS_PER_SLAB + half] * LOG2_E
                sb = sc[si * CHUNK:(si + 1) * CHUNK] + chunk_bias
                m = jnp.maximum(jnp.max(sb, axis=-1, keepdims=True), sink)
                p_parts.append(jnp.exp2(sb - m).astype(_BF16))
                e_parts.append(jnp.exp2(sink - m))
            probs.append(jnp.concatenate(p_parts, axis=0))
            sink_terms.append(e_parts)
        return probs, sink_terms

    def attend(unit, probs, sink_terms):
        c, kvh = unit
        rows = slice(c * CHUNK, (c + 1) * CHUNK)
        band = slice(c * CHUNK, (c + 2) * CHUNK)
        od = (_dot(probs[0], vx_ref[prev, 2 * kvh, band, :])
              + _dot(probs[1], vx_ref[prev, 2 * kvh + 1, band, :]))
        for si in range(SLABS_PER_KV):
            s = kvh * SLABS_PER_KV + si
            blk = slice(si * CHUNK, (si + 1) * CHUNK)
            denom = od[blk, LANES:] + jnp.where(low_head_c, sink_terms[0][si], sink_terms[1][si])
            gate = gate_ref[prev, rows, s * LANES:(s + 1) * LANES]
            y_ref[rows, s * LANES:(s + 1) * LANES] = (
                od[blk, :LANES] * gate / (denom * (1.0 + jnp.exp(-gate)))).astype(_BF16)

    def out_part(kvh):
        cols = slice(kvh * SLABS_PER_KV * LANES, (kvh + 1) * SLABS_PER_KV * LANES)
        return _dot(y_ref[:, cols], wout_ref[cols, :])

    s0 = scores(units[0])
    project_kv()
    s1 = scores(units[1])
    project_q(0)
    p0 = softmax(units[0], s0)
    attend(units[0], *p0)
    s2 = scores(units[2])
    project_q(1)
    p1 = softmax(units[1], s1)
    attend(units[1], *p1)
    s3 = scores(units[3])
    project_q(2)
    p2 = softmax(units[2], s2)
    attend(units[2], *p2)
    project_q(3)
    project_gate(0)
    acc = out_part(0)
    p3 = softmax(units[3], s3)
    attend(units[3], *p3)
    project_gate(1)
    out = hp_ref[...] + acc + out_part(1)
    o_ref[...] = out * lax.rsqrt(jnp.mean(out * out, axis=-1, keepdims=True) + EPS) * fg_ref[...]
    project_gate(2)
    project_gate(3)


def _layer_b(h2d, seq, sinks, kv_g, b_g, w_kv, b_kv, w_in, b_q, w_out, f_g, cos, sin):
    tm = B_TOKENS_PER_STEP
    n_tiles = h2d.shape[0] // tm
    tiles_per_seq = seq // tm
    const = lambda t: (0, 0)
    cur = lambda t: (jnp.minimum(t, n_tiles - 1), 0)
    prv = lambda t: (jnp.maximum(t - 1, 0), 0)
    tab = lambda t: (jnp.minimum(t, n_tiles - 1) % tiles_per_seq, 0)
    planes = 2 * N_KV_HEADS
    return pl.pallas_call(
        functools.partial(_layer_b_kernel, tiles_per_seq=tiles_per_seq),
        out_shape=jax.ShapeDtypeStruct(h2d.shape, _F32),
        grid=(n_tiles + 1,),
        in_specs=[
            pl.BlockSpec(memory_space=pltpu.SMEM),
            pl.BlockSpec((tm, D_MODEL), cur),
            pl.BlockSpec((tm, D_MODEL), prv),
            pl.BlockSpec((1, D_MODEL), const),
            pl.BlockSpec((1, D_MODEL), const),
            pl.BlockSpec((D_MODEL, 2 * KV_WIDTH), const),
            pl.BlockSpec((1, 2 * KV_WIDTH), const),
            pl.BlockSpec((D_MODEL, 2 * B_WIDTH), const),
            pl.BlockSpec((1, B_WIDTH), const),
            pl.BlockSpec((B_WIDTH, D_MODEL), const),
            pl.BlockSpec((1, D_MODEL), const),
            pl.BlockSpec((tm, LANES), tab),
            pl.BlockSpec((tm, LANES), tab),
        ],
        out_specs=pl.BlockSpec((tm, D_MODEL), prv),
        scratch_shapes=[
            pltpu.VMEM((2, planes, LANES, CHUNK + tm), _BF16),
            pltpu.VMEM((2, planes, CHUNK + tm, 2 * LANES), _BF16),
            pltpu.VMEM((2, tm, B_WIDTH), _BF16),
            pltpu.VMEM((2, tm, B_WIDTH), _F32),
            pltpu.VMEM((tm, B_WIDTH), _BF16),
        ],
        compiler_params=pltpu.CompilerParams(
            dimension_semantics=("arbitrary",), vmem_limit_bytes=VMEM_LIMIT_BYTES),
        name="swa_layer",
    )(sinks, h2d, h2d, kv_g, b_g, w_kv, b_kv, w_in, b_q, w_out, f_g, cos, sin)


def _rope_tables(seq):
    inv_freq = ROPE_THETA ** (-jnp.arange(0, HEAD_DIM, 2, dtype=_F32) / HEAD_DIM)
    ang = jnp.arange(seq, dtype=_F32)[:, None] * inv_freq[None, :]
    cos = jnp.cos(ang)
    sin = jnp.sin(ang)
    cos_slab = jnp.tile(jnp.concatenate([cos, cos], axis=-1), (1, HEADS_PER_SLAB))
    sin_slab = jnp.tile(jnp.concatenate([-sin, sin], axis=-1), (1, HEADS_PER_SLAB))
    assert cos_slab.shape == (seq, LANES)
    return cos_slab, sin_slab


def kernel(x, a_norm_g, a_w_in, a_ln_g, a_ln_b, a_ws, a_bs, a_w_out, kv_norm_g, w_kv, b_kv,
           b_norm_g, b_w_in, b_bq, b_sinks, b_w_out, final_norm_g):
    batch, seq, d = x.shape
    assert d == D_MODEL
    for tile in (A_TOKENS_PER_STEP, B_TOKENS_PER_STEP):
        assert seq % tile == 0 and tile % CHUNK == 0
    assert a_w_in.shape[0] == 1 and b_w_in.shape[0] == 1
    row = lambda p: p.reshape(1, -1)

    x2 = x.reshape(batch * seq, d)
    h = _layer_a(x2, row(a_norm_g[0]), a_w_in[0].astype(_BF16), row(a_ln_g[0]), row(a_ln_b[0]),
                 a_ws[0], a_bs[0].T, a_w_out[0].astype(_BF16))
    cos, sin = _rope_tables(seq)
    out = _layer_b(h, seq, b_sinks[0], row(kv_norm_g), row(b_norm_g[0]),
                   w_kv.astype(_BF16), row(b_kv), b_w_in[0].astype(_BF16), row(b_bq[0]),
                   b_w_out[0].astype(_BF16), row(final_norm_g), cos, sin)
    return out.reshape(batch, seq, d)
```

```python
import functools

import jax
import jax.numpy as jnp
from jax import lax
from jax.experimental import pallas as pl
from jax.experimental.pallas import tpu as pltpu

D_MODEL = 1024
CHUNK = 128
A_WIDTH = 2048
A_GROUPS = 16
HEAD_DIM = 64
N_Q_HEADS = 16
N_KV_HEADS = 2
B_WIDTH = 1024
KV_WIDTH = 128
ROPE_THETA = 10000.0
EPS = 1e-5
LOG2_E = 1.4426950408889634

LANES = 128
HEADS_PER_SLAB = LANES // HEAD_DIM
N_SLABS = B_WIDTH // LANES
SLABS_PER_KV = N_SLABS // N_KV_HEADS

A_TOKENS_PER_STEP = 512
B_TOKENS_PER_STEP = 256
P_BLOCK = 256
A_COL_BLOCK = 512
A_ROW_SPLIT = 2
VMEM_LIMIT_BYTES = 56 * 1024 * 1024
WEIGHT_STAGE_BYTES = 2 * 1024 * 1024

_F32 = jnp.float32
_BF16 = jnp.bfloat16


def _dot(a, b):
    return jnp.dot(a, b, preferred_element_type=_F32)


def _silu(g):
    return g / (1.0 + jnp.exp(-g))


def _load_weight_as_bf16(w_hbm, w_ref):
    k, n = w_hbm.shape
    rows = min(k, pl.next_power_of_2(WEIGHT_STAGE_BYTES // (4 * n) + 1) // 2)
    assert k % rows == 0 and rows % 16 == 0
    n_chunks = k // rows

    def body(stage, sem):
        def copy(i):
            return pltpu.make_async_copy(w_hbm.at[pl.ds(i * rows, rows), :], stage.at[i % 2], sem.at[i % 2])

        copy(0).start()
        for i in range(n_chunks):
            if i + 1 < n_chunks:
                copy(i + 1).start()
            copy(i).wait()
            w_ref[i * rows:(i + 1) * rows, :] = stage[i % 2].astype(_BF16)

    pl.run_scoped(body, pltpu.VMEM((2, rows, n), _F32), pltpu.SemaphoreType.DMA((2,)))


def _layer_a_kernel(x_ref, ng_ref, win_hbm, lng_ref, lnb_ref, ws_ref, bst_ref, wout_hbm,
                    o_ref, vn_ref, y_ref, win_ref, wout_ref):
    @pl.when(pl.program_id(0) == 0)
    def _():
        _load_weight_as_bf16(win_hbm, win_ref)
        _load_weight_as_bf16(wout_hbm, wout_ref)

    tm = x_ref.shape[0]
    half = tm // A_ROW_SPLIT
    n_blocks = A_WIDTH // A_COL_BLOCK
    groups_per_block = A_COL_BLOCK // CHUNK

    row = lax.broadcasted_iota(jnp.int32, (CHUNK, CHUNK), 0)
    col = lax.broadcasted_iota(jnp.int32, (CHUNK, CHUNK), 1)
    causal = col <= row

    def normed(r0):
        x = x_ref[r0:r0 + half, :]
        return (x * lax.rsqrt(jnp.mean(x * x, axis=-1, keepdims=True) + EPS) * ng_ref[...]).astype(_BF16)

    def layer_norm_v(r0, v):
        mu = jnp.mean(v, axis=-1, keepdims=True)
        vc = v - mu
        var = jnp.mean(vc * vc, axis=-1, keepdims=True)
        vn_ref[r0:r0 + half, :] = (vc * lax.rsqrt(var + EPS) * lng_ref[...] + lnb_ref[...]).astype(_BF16)

    def gated_block(r0, h1, j):
        c0 = j * A_COL_BLOCK
        u = _dot(h1, win_ref[:, c0:c0 + A_COL_BLOCK])
        gt = _dot(h1, win_ref[:, 2 * A_WIDTH + c0:2 * A_WIDTH + c0 + A_COL_BLOCK])
        for gi in range(groups_per_block):
            g = j * groups_per_block + gi
            wsm = jnp.where(causal, ws_ref[g], 0.0).astype(_BF16)
            bias = jnp.broadcast_to(bst_ref[:, g:g + 1], (CHUNK, CHUNK))
            for c in range(half // CHUNK):
                rows = slice(c * CHUNK, (c + 1) * CHUNK)
                tile_rows = slice(r0 + c * CHUNK, r0 + (c + 1) * CHUNK)
                cols = slice(gi * CHUNK, (gi + 1) * CHUNK)
                sv = _dot(wsm, vn_ref[tile_rows, g * CHUNK:(g + 1) * CHUNK]) + bias
                y_ref[tile_rows, g * CHUNK:(g + 1) * CHUNK] = (
                    u[rows, cols] * sv * _silu(gt[rows, cols])).astype(_BF16)

    starts = [i * half for i in range(A_ROW_SPLIT)]
    h1 = [normed(r0) for r0 in starts]
    v = [_dot(h, win_ref[:, A_WIDTH:2 * A_WIDTH]) for h in h1]
    for r0, vi in zip(starts, v):
        layer_norm_v(r0, vi)
    for j in range(n_blocks):
        for r0, h in zip(starts, h1):
            gated_block(r0, h, j)
    for r0 in starts:
        o_ref[r0:r0 + half, :] = x_ref[r0:r0 + half, :] + _dot(y_ref[r0:r0 + half, :], wout_ref[...])


def _layer_a(x2, norm_g, w_in, ln_g, ln_b, ws, bs_t, w_out):
    t = x2.shape[0]
    tm = A_TOKENS_PER_STEP
    const = lambda i: (0, 0)
    return pl.pallas_call(
        _layer_a_kernel,
        out_shape=jax.ShapeDtypeStruct((t, D_MODEL), _F32),
        grid=(t // tm,),
        in_specs=[
            pl.BlockSpec((tm, D_MODEL), lambda i: (i, 0)),
            pl.BlockSpec((1, D_MODEL), const),
            pl.BlockSpec(memory_space=pl.ANY),
            pl.BlockSpec((1, A_WIDTH), const),
            pl.BlockSpec((1, A_WIDTH), const),
            pl.BlockSpec((A_GROUPS, CHUNK, CHUNK), lambda i: (0, 0, 0)),
            pl.BlockSpec((CHUNK, A_GROUPS), const),
            pl.BlockSpec(memory_space=pl.ANY),
        ],
        out_specs=pl.BlockSpec((tm, D_MODEL), lambda i: (i, 0)),
        scratch_shapes=[
            pltpu.VMEM((tm, A_WIDTH), _BF16),
            pltpu.VMEM((tm, A_WIDTH), _BF16),
            pltpu.VMEM((D_MODEL, 3 * A_WIDTH), _BF16),
            pltpu.VMEM((A_WIDTH, D_MODEL), _BF16),
        ],
        compiler_params=pltpu.CompilerParams(
            dimension_semantics=("arbitrary",), vmem_limit_bytes=VMEM_LIMIT_BYTES),
        name="gmlp_layer",
    )(x2, norm_g, w_in, ln_g, ln_b, ws, bs_t, w_out)


def _layer_b_kernel(sinks_ref, hc_ref, hp_ref, kvg_ref, bg_ref, wkv_hbm, bkv_ref, win_hbm, bq_ref,
                    wout_hbm, fg_ref, cos_ref, sin_ref,
                    o_ref, kt_ref, vx_ref, q_ref, gate_ref, y_ref, wkv_ref, win_ref, wout_ref,
                    *, tiles_per_seq):
    t = pl.program_id(0)

    @pl.when(t == 0)
    def _():
        _load_weight_as_bf16(wkv_hbm, wkv_ref)
        _load_weight_as_bf16(win_hbm, win_ref)
        _load_weight_as_bf16(wout_hbm, wout_ref)

    refs = (sinks_ref, hc_ref, hp_ref, kvg_ref, bg_ref, wkv_ref, bkv_ref, win_ref, bq_ref,
            wout_ref, fg_ref, cos_ref, sin_ref, o_ref, kt_ref, vx_ref, q_ref, gate_ref, y_ref)

    @pl.when(t % 2 == 0)
    def _():
        _layer_b_step(*refs, t=t, slot=0, tiles_per_seq=tiles_per_seq)

    @pl.when(t % 2 == 1)
    def _():
        _layer_b_step(*refs, t=t, slot=1, tiles_per_seq=tiles_per_seq)


def _layer_b_step(sinks_ref, hc_ref, hp_ref, kvg_ref, bg_ref, wkv_ref, bkv_ref, win_ref, bq_ref,
                  wout_ref, fg_ref, cos_ref, sin_ref,
                  o_ref, kt_ref, vx_ref, q_ref, gate_ref, y_ref, *, t, slot, tiles_per_seq):
    tm = hc_ref.shape[0]
    prev = 1 - slot
    new = slice(CHUNK, CHUNK + tm)
    planes = 2 * N_KV_HEADS

    if slot == 0:
        @pl.when(t == 0)
        def _():
            kt_ref[prev] = jnp.zeros((planes, LANES, CHUNK + tm), _BF16)
            vx_ref[prev] = jnp.zeros((planes, CHUNK + tm, 2 * LANES), _BF16)
            q_ref[prev] = jnp.zeros((tm, B_WIDTH), _BF16)
            gate_ref[prev] = jnp.zeros((tm, B_WIDTH), _F32)

    first_in_seq = (t % tiles_per_seq) == 0

    @pl.when(first_in_seq)
    def _():
        kt_ref[slot, :, :, 0:CHUNK] = jnp.zeros((planes, LANES, CHUNK), _BF16)
        vx_ref[slot, :, 0:CHUNK, :] = jnp.zeros((planes, CHUNK, 2 * LANES), _BF16)

    @pl.when(jnp.logical_not(first_in_seq))
    def _():
        kt_ref[slot, :, :, 0:CHUNK] = kt_ref[prev, :, :, tm:tm + CHUNK]
        vx_ref[slot, :, 0:CHUNK, :] = vx_ref[prev, :, tm:tm + CHUNK, :]

    h = hc_ref[...]
    xn = h * lax.rsqrt(jnp.mean(h * h, axis=-1, keepdims=True) + EPS)
    kvn = (xn * kvg_ref[...]).astype(_BF16)
    h2 = (xn * bg_ref[...]).astype(_BF16)

    cos = cos_ref[...]
    sin = sin_ref[...]
    lane = lax.broadcasted_iota(jnp.int32, (tm, LANES), 1)
    first_half = (lane % HEAD_DIM) < (HEAD_DIM // 2)
    low_head = lane < HEAD_DIM

    def rope(x):
        swapped = jnp.where(first_half,
                            pltpu.roll(x, LANES - HEAD_DIM // 2, 1),
                            pltpu.roll(x, HEAD_DIM // 2, 1))
        return x * cos + swapped * sin

    def project_kv():
        kv = _dot(kvn, wkv_ref[...]) + bkv_ref[...]
        k_t = rope(kv[:, :KV_WIDTH]).T
        zero_rows = jnp.zeros((HEAD_DIM, tm), _F32)
        for kvh in range(N_KV_HEADS):
            head_t = k_t[kvh * HEAD_DIM:(kvh + 1) * HEAD_DIM]
            kt_ref[slot, 2 * kvh, :, new] = jnp.concatenate([head_t, zero_rows], axis=0).astype(_BF16)
            kt_ref[slot, 2 * kvh + 1, :, new] = jnp.concatenate([zero_rows, head_t], axis=0).astype(_BF16)
        v = kv[:, KV_WIDTH:]
        v_sw = pltpu.roll(v, HEAD_DIM, 1)
        zero = jnp.zeros_like(v)
        one = jnp.ones_like(v)
        ones_lo = jnp.where(low_head, one, zero)
        ones_hi = jnp.where(low_head, zero, one)
        v_low = (jnp.where(low_head, v, zero), jnp.where(low_head, v_sw, zero))
        v_high = (jnp.where(low_head, zero, v_sw), jnp.where(low_head, zero, v))
        for kvh in range(N_KV_HEADS):
            vx_ref[slot, 2 * kvh, new, :] = jnp.concatenate([v_low[kvh], ones_lo], axis=1).astype(_BF16)
            vx_ref[slot, 2 * kvh + 1, new, :] = jnp.concatenate([v_high[kvh], ones_hi], axis=1).astype(_BF16)

    q_scale = (HEAD_DIM ** -0.5) * LOG2_E

    def project_q(b):
        cols = slice(b * P_BLOCK, (b + 1) * P_BLOCK)
        zq = _dot(h2, win_ref[:, cols]) + bq_ref[:, cols]
        for i in range(P_BLOCK // LANES):
            s = b * (P_BLOCK // LANES) + i
            q_ref[slot, :, s * LANES:(s + 1) * LANES] = (
                rope(zq[:, i * LANES:(i + 1) * LANES]) * q_scale).astype(_BF16)

    def project_gate(b):
        cols = slice(b * P_BLOCK, (b + 1) * P_BLOCK)
        gate_ref[slot, :, cols] = _dot(h2, win_ref[:, B_WIDTH + b * P_BLOCK:B_WIDTH + (b + 1) * P_BLOCK])

    qi = lax.broadcasted_iota(jnp.int32, (CHUNK, 2 * CHUNK), 0)
    kj = lax.broadcasted_iota(jnp.int32, (CHUNK, 2 * CHUNK), 1)
    in_window = (kj > qi) & (kj <= qi + CHUNK)
    neg_inf = jnp.float32(-jnp.inf)
    bias = jnp.where(in_window, 0.0, neg_inf)
    prev_first_in_seq = ((t + tiles_per_seq - 1) % tiles_per_seq) == 0
    bias_first = jnp.where(in_window & ((kj >= CHUNK) | jnp.logical_not(prev_first_in_seq)), 0.0, neg_inf)
    low_head_c = lax.broadcasted_iota(jnp.int32, (CHUNK, LANES), 1) < HEAD_DIM

    units = [(c, kvh) for kvh in range(N_KV_HEADS) for c in range(tm // CHUNK)]
    assert len(units) == 4 and B_WIDTH // P_BLOCK == 4, "the program order below is written for 4 + 4"

    def scores(unit):
        c, kvh = unit
        rows = slice(c * CHUNK, (c + 1) * CHUNK)
        band = slice(c * CHUNK, (c + 2) * CHUNK)
        q_stack = jnp.concatenate(
            [q_ref[prev, rows, s * LANES:(s + 1) * LANES]
             for s in range(kvh * SLABS_PER_KV, (kvh + 1) * SLABS_PER_KV)], axis=0)
        return [_dot(q_stack, kt_ref[prev, 2 * kvh + half, :, band]) for half in range(HEADS_PER_SLAB)]

    def softmax(unit, sc_halves):
        c, kvh = unit
        chunk_bias = bias_first if c == 0 else bias
        probs = []
        sink_terms = []
        for half, sc in enumerate(sc_halves):
            p_parts = []
            e_parts = []
            for si in range(SLABS_PER_KV):
                s = kvh * SLABS_PER_KV + si
                sink = sinks_ref[s * HEADS_PER_SLAB + half] * LOG2_E
                sb = sc[si * CHUNK:(si + 1) * CHUNK] + chunk_bias
                m = jnp.maximum(jnp.max(sb, axis=-1, keepdims=True), sink)
                p_parts.append(jnp.exp2(sb - m).astype(_BF16))
                e_parts.append(jnp.exp2(sink - m))
            probs.append(jnp.concatenate(p_parts, axis=0))
            sink_terms.append(e_parts)
        return probs, sink_terms

    def attend(unit, probs, sink_terms):
        c, kvh = unit
        rows = slice(c * CHUNK, (c + 1) * CHUNK)
        band = slice(c * CHUNK, (c + 2) * CHUNK)
        od = (_dot(probs[0], vx_ref[prev, 2 * kvh, band, :])
              + _dot(probs[1], vx_ref[prev, 2 * kvh + 1, band, :]))
        for si in range(SLABS_PER_KV):
            s = kvh * SLABS_PER_KV + si
            blk = slice(si * CHUNK, (si + 1) * CHUNK)
            denom = od[blk, LANES:] + jnp.where(low_head_c, sink_terms[0][si], sink_terms[1][si])
            gate = gate_ref[prev, rows, s * LANES:(s + 1) * LANES]
            y_ref[rows, s * LANES:(s + 1) * LANES] = (
                od[blk, :LANES] * gate / (denom * (1.0 + jnp.exp(-gate)))).astype(_BF16)

    def out_part(kvh):
        cols = slice(kvh * SLABS_PER_KV * LANES, (kvh + 1) * SLABS_PER_KV * LANES)
        return _dot(y_ref[:, cols], wout_ref[cols, :])

    s0 = scores(units[0])
    project_kv()
    s1 = scores(units[1])
    project_q(0)
    p0 = softmax(units[0], s0)
    attend(units[0], *p0)
    s2 = scores(units[2])
    project_q(1)
    p1 = softmax(units[1], s1)
    attend(units[1], *p1)
    s3 = scores(units[3])
    project_q(2)
    p2 = softmax(units[2], s2)
    attend(units[2], *p2)
    project_q(3)
    project_gate(0)
    acc = out_part(0)
    p3 = softmax(units[3], s3)
    attend(units[3], *p3)
    project_gate(1)
    out = hp_ref[...] + acc + out_part(1)
    o_ref[...] = out * lax.rsqrt(jnp.mean(out * out, axis=-1, keepdims=True) + EPS) * fg_ref[...]
    project_gate(2)
    project_gate(3)


def _layer_b(h2d, seq, sinks, kv_g, b_g, w_kv, b_kv, w_in, b_q, w_out, f_g, cos, sin):
    tm = B_TOKENS_PER_STEP
    n_tiles = h2d.shape[0] // tm
    tiles_per_seq = seq // tm
    const = lambda t: (0, 0)
    cur = lambda t: (jnp.minimum(t, n_tiles - 1), 0)
    prv = lambda t: (jnp.maximum(t - 1, 0), 0)
    tab = lambda t: (jnp.minimum(t, n_tiles - 1) % tiles_per_seq, 0)
    planes = 2 * N_KV_HEADS
    return pl.pallas_call(
        functools.partial(_layer_b_kernel, tiles_per_seq=tiles_per_seq),
        out_shape=jax.ShapeDtypeStruct(h2d.shape, _F32),
        grid=(n_tiles + 1,),
        in_specs=[
            pl.BlockSpec(memory_space=pltpu.SMEM),
            pl.BlockSpec((tm, D_MODEL), cur),
            pl.BlockSpec((tm, D_MODEL), prv),
            pl.BlockSpec((1, D_MODEL), const),
            pl.BlockSpec((1, D_MODEL), const),
            pl.BlockSpec(memory_space=pl.ANY),
            pl.BlockSpec((1, 2 * KV_WIDTH), const),
            pl.BlockSpec(memory_space=pl.ANY),
            pl.BlockSpec((1, B_WIDTH), const),
            pl.BlockSpec(memory_space=pl.ANY),
            pl.BlockSpec((1, D_MODEL), const),
            pl.BlockSpec((tm, LANES), tab),
            pl.BlockSpec((tm, LANES), tab),
        ],
        out_specs=pl.BlockSpec((tm, D_MODEL), prv),
        scratch_shapes=[
            pltpu.VMEM((2, planes, LANES, CHUNK + tm), _BF16),
            pltpu.VMEM((2, planes, CHUNK + tm, 2 * LANES), _BF16),
            pltpu.VMEM((2, tm, B_WIDTH), _BF16),
            pltpu.VMEM((2, tm, B_WIDTH), _F32),
            pltpu.VMEM((tm, B_WIDTH), _BF16),
            pltpu.VMEM((D_MODEL, 2 * KV_WIDTH), _BF16),
            pltpu.VMEM((D_MODEL, 2 * B_WIDTH), _BF16),
            pltpu.VMEM((B_WIDTH, D_MODEL), _BF16),
        ],
        compiler_params=pltpu.CompilerParams(
            dimension_semantics=("arbitrary",), vmem_limit_bytes=VMEM_LIMIT_BYTES),
        name="swa_layer",
    )(sinks, h2d, h2d, kv_g, b_g, w_kv, b_kv, w_in, b_q, w_out, f_g, cos, sin)


def _rope_tables(seq):
    inv_freq = ROPE_THETA ** (-jnp.arange(0, HEAD_DIM, 2, dtype=_F32) / HEAD_DIM)
    ang = jnp.arange(seq, dtype=_F32)[:, None] * inv_freq[None, :]
    cos = jnp.cos(ang)
    sin = jnp.sin(ang)
    cos_slab = jnp.tile(jnp.concatenate([cos, cos], axis=-1), (1, HEADS_PER_SLAB))
    sin_slab = jnp.tile(jnp.concatenate([-sin, sin], axis=-1), (1, HEADS_PER_SLAB))
    assert cos_slab.shape == (seq, LANES)
    return cos_slab, sin_slab


def kernel(x, a_norm_g, a_w_in, a_ln_g, a_ln_b, a_ws, a_bs, a_w_out, kv_norm_g, w_kv, b_kv,
           b_norm_g, b_w_in, b_bq, b_sinks, b_w_out, final_norm_g):
    batch, seq, d = x.shape
    assert d == D_MODEL
    for tile in (A_TOKENS_PER_STEP, B_TOKENS_PER_STEP):
        assert seq % tile == 0 and tile % CHUNK == 0
    assert a_w_in.shape[0] == 1 and b_w_in.shape[0] == 1
    row = lambda p: p.reshape(1, -1)

    x2 = x.reshape(batch * seq, d)
    h = _layer_a(x2, row(a_norm_g[0]), a_w_in[0], row(a_ln_g[0]), row(a_ln_b[0]),
                 a_ws[0], a_bs[0].T, a_w_out[0])
    cos, sin = _rope_tables(seq)
    out = _layer_b(h, seq, b_sinks[0], row(kv_norm_g), row(b_norm_g[0]),
                   w_kv, row(b_kv), b_w_in[0], row(b_bq[0]),
                   b_w_out[0], row(final_norm_g), cos, sin)
    return out.reshape(batch, seq, d)
```

```python
import functools

import jax
import jax.numpy as jnp
import numpy as np
from jax import lax
from jax.experimental import pallas as pl
from jax.experimental.pallas import tpu as pltpu

D_MODEL = 1024
CHUNK = 128
A_WIDTH = 2048
A_GROUPS = 16
HEAD_DIM = 64
N_Q_HEADS = 16
N_KV_HEADS = 2
B_WIDTH = 1024
KV_WIDTH = 128
ROPE_THETA = 10000.0
EPS = 1e-5
LOG2_E = 1.4426950408889634

LANES = 128
HEADS_PER_SLAB = LANES // HEAD_DIM
N_SLABS = B_WIDTH // LANES
SLABS_PER_KV = N_SLABS // N_KV_HEADS

A_TOKENS_PER_STEP = 512
B_TOKENS_PER_STEP = 256
P_BLOCK = 256
A_COL_BLOCK = 512
A_ROW_SPLIT = 2
VMEM_LIMIT_BYTES = 56 * 1024 * 1024
WEIGHT_STAGE_BYTES = 2 * 1024 * 1024

_F32 = jnp.float32
_BF16 = jnp.bfloat16


def _dot(a, b):
    return jnp.dot(a, b, preferred_element_type=_F32)


def _silu(g):
    return g / (1.0 + jnp.exp(-g))


def _load_weight_as_bf16(w_hbm, w_ref):
    k, n = w_hbm.shape
    rows = min(k, pl.next_power_of_2(WEIGHT_STAGE_BYTES // (4 * n) + 1) // 2)
    assert k % rows == 0 and rows % 16 == 0
    n_chunks = k // rows

    def body(stage, sem):
        def copy(i):
            return pltpu.make_async_copy(w_hbm.at[pl.ds(i * rows, rows), :], stage.at[i % 2], sem.at[i % 2])

        copy(0).start()
        for i in range(n_chunks):
            if i + 1 < n_chunks:
                copy(i + 1).start()
            copy(i).wait()
            w_ref[i * rows:(i + 1) * rows, :] = stage[i % 2].astype(_BF16)

    pl.run_scoped(body, pltpu.VMEM((2, rows, n), _F32), pltpu.SemaphoreType.DMA((2,)))


def _layer_a_kernel(x_ref, ng_ref, win_hbm, lng_ref, lnb_ref, ws_ref, bst_ref, wout_hbm,
                    o_ref, vn_ref, y_ref, win_ref, wout_ref):
    @pl.when(pl.program_id(0) == 0)
    def _():
        _load_weight_as_bf16(win_hbm, win_ref)
        _load_weight_as_bf16(wout_hbm, wout_ref)

    tm = x_ref.shape[0]
    half = tm // A_ROW_SPLIT
    n_blocks = A_WIDTH // A_COL_BLOCK
    groups_per_block = A_COL_BLOCK // CHUNK

    row = lax.broadcasted_iota(jnp.int32, (CHUNK, CHUNK), 0)
    col = lax.broadcasted_iota(jnp.int32, (CHUNK, CHUNK), 1)
    causal = col <= row

    def normed(r0):
        x = x_ref[r0:r0 + half, :]
        return (x * lax.rsqrt(jnp.mean(x * x, axis=-1, keepdims=True) + EPS) * ng_ref[...]).astype(_BF16)

    def layer_norm_v(r0, v):
        mu = jnp.mean(v, axis=-1, keepdims=True)
        vc = v - mu
        var = jnp.mean(vc * vc, axis=-1, keepdims=True)
        vn_ref[r0:r0 + half, :] = (vc * lax.rsqrt(var + EPS) * lng_ref[...] + lnb_ref[...]).astype(_BF16)

    def gated_block(r0, h1, j):
        c0 = j * A_COL_BLOCK
        u = _dot(h1, win_ref[:, c0:c0 + A_COL_BLOCK])
        gt = _dot(h1, win_ref[:, 2 * A_WIDTH + c0:2 * A_WIDTH + c0 + A_COL_BLOCK])
        for gi in range(groups_per_block):
            g = j * groups_per_block + gi
            wsm = jnp.where(causal, ws_ref[g], 0.0).astype(_BF16)
            bias = jnp.broadcast_to(bst_ref[:, g:g + 1], (CHUNK, CHUNK))
            for c in range(half // CHUNK):
                rows = slice(c * CHUNK, (c + 1) * CHUNK)
                tile_rows = slice(r0 + c * CHUNK, r0 + (c + 1) * CHUNK)
                cols = slice(gi * CHUNK, (gi + 1) * CHUNK)
                sv = _dot(wsm, vn_ref[tile_rows, g * CHUNK:(g + 1) * CHUNK]) + bias
                y_ref[tile_rows, g * CHUNK:(g + 1) * CHUNK] = (
                    u[rows, cols] * sv * _silu(gt[rows, cols])).astype(_BF16)

    starts = [i * half for i in range(A_ROW_SPLIT)]
    h1 = [normed(r0) for r0 in starts]
    v = [_dot(h, win_ref[:, A_WIDTH:2 * A_WIDTH]) for h in h1]
    for r0, vi in zip(starts, v):
        layer_norm_v(r0, vi)
    for j in range(n_blocks):
        for r0, h in zip(starts, h1):
            gated_block(r0, h, j)
    for r0 in starts:
        o_ref[r0:r0 + half, :] = x_ref[r0:r0 + half, :] + _dot(y_ref[r0:r0 + half, :], wout_ref[...])


def _layer_a(x2, norm_g, w_in, ln_g, ln_b, ws, bs_t, w_out):
    t = x2.shape[0]
    tm = A_TOKENS_PER_STEP
    const = lambda i: (0, 0)
    return pl.pallas_call(
        _layer_a_kernel,
        out_shape=jax.ShapeDtypeStruct((t, D_MODEL), _F32),
        grid=(t // tm,),
        in_specs=[
            pl.BlockSpec((tm, D_MODEL), lambda i: (i, 0)),
            pl.BlockSpec((1, D_MODEL), const),
            pl.BlockSpec(memory_space=pl.ANY),
            pl.BlockSpec((1, A_WIDTH), const),
            pl.BlockSpec((1, A_WIDTH), const),
            pl.BlockSpec((A_GROUPS, CHUNK, CHUNK), lambda i: (0, 0, 0)),
            pl.BlockSpec((CHUNK, A_GROUPS), const),
            pl.BlockSpec(memory_space=pl.ANY),
        ],
        out_specs=pl.BlockSpec((tm, D_MODEL), lambda i: (i, 0)),
        scratch_shapes=[
            pltpu.VMEM((tm, A_WIDTH), _BF16),
            pltpu.VMEM((tm, A_WIDTH), _BF16),
            pltpu.VMEM((D_MODEL, 3 * A_WIDTH), _BF16),
            pltpu.VMEM((A_WIDTH, D_MODEL), _BF16),
        ],
        compiler_params=pltpu.CompilerParams(
            dimension_semantics=("arbitrary",), vmem_limit_bytes=VMEM_LIMIT_BYTES),
        name="gmlp_layer",
    )(x2, norm_g, w_in, ln_g, ln_b, ws, bs_t, w_out)


def _layer_b_kernel(sinks_ref, hc_ref, hp_ref, kvg_ref, bg_ref, wkv_hbm, bkv_ref, win_hbm, bq_ref,
                    wout_hbm, fg_ref, cos_ref, sin_ref,
                    o_ref, kt_ref, vx_ref, q_ref, gate_ref, y_ref, wkv_ref, win_ref, wout_ref,
                    *, tiles_per_seq):
    t = pl.program_id(0)

    @pl.when(t == 0)
    def _():
        _load_weight_as_bf16(wkv_hbm, wkv_ref)
        _load_weight_as_bf16(win_hbm, win_ref)
        _load_weight_as_bf16(wout_hbm, wout_ref)

    refs = (sinks_ref, hc_ref, hp_ref, kvg_ref, bg_ref, wkv_ref, bkv_ref, win_ref, bq_ref,
            wout_ref, fg_ref, cos_ref, sin_ref, o_ref, kt_ref, vx_ref, q_ref, gate_ref, y_ref)

    @pl.when(t % 2 == 0)
    def _():
        _layer_b_step(*refs, t=t, slot=0, tiles_per_seq=tiles_per_seq)

    @pl.when(t % 2 == 1)
    def _():
        _layer_b_step(*refs, t=t, slot=1, tiles_per_seq=tiles_per_seq)


def _layer_b_step(sinks_ref, hc_ref, hp_ref, kvg_ref, bg_ref, wkv_ref, bkv_ref, win_ref, bq_ref,
                  wout_ref, fg_ref, cos_ref, sin_ref,
                  o_ref, kt_ref, vx_ref, q_ref, gate_ref, y_ref, *, t, slot, tiles_per_seq):
    tm = hc_ref.shape[0]
    prev = 1 - slot
    new = slice(CHUNK, CHUNK + tm)
    planes = 2 * N_KV_HEADS

    if slot == 0:
        @pl.when(t == 0)
        def _():
            kt_ref[prev] = jnp.zeros((planes, LANES, CHUNK + tm), _BF16)
            vx_ref[prev] = jnp.zeros((planes, CHUNK + tm, 2 * LANES), _BF16)
            q_ref[prev] = jnp.zeros((tm, B_WIDTH), _BF16)
            gate_ref[prev] = jnp.zeros((tm, B_WIDTH), _F32)

    first_in_seq = (t % tiles_per_seq) == 0

    @pl.when(first_in_seq)
    def _():
        kt_ref[slot, :, :, 0:CHUNK] = jnp.zeros((planes, LANES, CHUNK), _BF16)
        vx_ref[slot, :, 0:CHUNK, :] = jnp.zeros((planes, CHUNK, 2 * LANES), _BF16)

    @pl.when(jnp.logical_not(first_in_seq))
    def _():
        kt_ref[slot, :, :, 0:CHUNK] = kt_ref[prev, :, :, tm:tm + CHUNK]
        vx_ref[slot, :, 0:CHUNK, :] = vx_ref[prev, :, tm:tm + CHUNK, :]

    h = hc_ref[...]
    xn = h * lax.rsqrt(jnp.mean(h * h, axis=-1, keepdims=True) + EPS)
    kvn = (xn * kvg_ref[...]).astype(_BF16)
    h2 = (xn * bg_ref[...]).astype(_BF16)

    cos = cos_ref[...]
    sin = sin_ref[...]
    lane = lax.broadcasted_iota(jnp.int32, (tm, LANES), 1)
    first_half = (lane % HEAD_DIM) < (HEAD_DIM // 2)
    low_head = lane < HEAD_DIM

    def rope(x):
        swapped = jnp.where(first_half,
                            pltpu.roll(x, LANES - HEAD_DIM // 2, 1),
                            pltpu.roll(x, HEAD_DIM // 2, 1))
        return x * cos + swapped * sin

    def project_kv():
        kv = _dot(kvn, wkv_ref[...]) + bkv_ref[...]
        k_t = rope(kv[:, :KV_WIDTH]).T
        zero_rows = jnp.zeros((HEAD_DIM, tm), _F32)
        for kvh in range(N_KV_HEADS):
            head_t = k_t[kvh * HEAD_DIM:(kvh + 1) * HEAD_DIM]
            kt_ref[slot, 2 * kvh, :, new] = jnp.concatenate([head_t, zero_rows], axis=0).astype(_BF16)
            kt_ref[slot, 2 * kvh + 1, :, new] = jnp.concatenate([zero_rows, head_t], axis=0).astype(_BF16)
        v = kv[:, KV_WIDTH:]
        v_sw = pltpu.roll(v, HEAD_DIM, 1)
        zero = jnp.zeros_like(v)
        one = jnp.ones_like(v)
        ones_lo = jnp.where(low_head, one, zero)
        ones_hi = jnp.where(low_head, zero, one)
        v_low = (jnp.where(low_head, v, zero), jnp.where(low_head, v_sw, zero))
        v_high = (jnp.where(low_head, zero, v_sw), jnp.where(low_head, zero, v))
        for kvh in range(N_KV_HEADS):
            vx_ref[slot, 2 * kvh, new, :] = jnp.concatenate([v_low[kvh], ones_lo], axis=1).astype(_BF16)
            vx_ref[slot, 2 * kvh + 1, new, :] = jnp.concatenate([v_high[kvh], ones_hi], axis=1).astype(_BF16)

    q_scale = (HEAD_DIM ** -0.5) * LOG2_E

    def project_q(b):
        cols = slice(b * P_BLOCK, (b + 1) * P_BLOCK)
        zq = _dot(h2, win_ref[:, cols]) + bq_ref[:, cols]
        for i in range(P_BLOCK // LANES):
            s = b * (P_BLOCK // LANES) + i
            q_ref[slot, :, s * LANES:(s + 1) * LANES] = (
                rope(zq[:, i * LANES:(i + 1) * LANES]) * q_scale).astype(_BF16)

    def project_gate(b):
        cols = slice(b * P_BLOCK, (b + 1) * P_BLOCK)
        gate_ref[slot, :, cols] = _dot(h2, win_ref[:, B_WIDTH + b * P_BLOCK:B_WIDTH + (b + 1) * P_BLOCK])

    qi = lax.broadcasted_iota(jnp.int32, (CHUNK, 2 * CHUNK), 0)
    kj = lax.broadcasted_iota(jnp.int32, (CHUNK, 2 * CHUNK), 1)
    in_window = (kj > qi) & (kj <= qi + CHUNK)
    neg_inf = jnp.float32(-jnp.inf)
    bias = jnp.where(in_window, 0.0, neg_inf)
    prev_first_in_seq = ((t + tiles_per_seq - 1) % tiles_per_seq) == 0
    bias_first = jnp.where(in_window & ((kj >= CHUNK) | jnp.logical_not(prev_first_in_seq)), 0.0, neg_inf)
    low_head_c = lax.broadcasted_iota(jnp.int32, (CHUNK, LANES), 1) < HEAD_DIM

    units = [(c, kvh) for kvh in range(N_KV_HEADS) for c in range(tm // CHUNK)]
    assert len(units) == 4 and B_WIDTH // P_BLOCK == 4, "the program order below is written for 4 + 4"

    def scores(unit):
        c, kvh = unit
        rows = slice(c * CHUNK, (c + 1) * CHUNK)
        band = slice(c * CHUNK, (c + 2) * CHUNK)
        q_stack = jnp.concatenate(
            [q_ref[prev, rows, s * LANES:(s + 1) * LANES]
             for s in range(kvh * SLABS_PER_KV, (kvh + 1) * SLABS_PER_KV)], axis=0)
        return [_dot(q_stack, kt_ref[prev, 2 * kvh + half, :, band]) for half in range(HEADS_PER_SLAB)]

    def softmax(unit, sc_halves):
        c, kvh = unit
        chunk_bias = bias_first if c == 0 else bias
        probs = []
        sink_terms = []
        for half, sc in enumerate(sc_halves):
            p_parts = []
            e_parts = []
            for si in range(SLABS_PER_KV):
                s = kvh * SLABS_PER_KV + si
                sink = sinks_ref[s * HEADS_PER_SLAB + half] * LOG2_E
                sb = sc[si * CHUNK:(si + 1) * CHUNK] + chunk_bias
                m = jnp.maximum(jnp.max(sb, axis=-1, keepdims=True), sink)
                p_parts.append(jnp.exp2(sb - m).astype(_BF16))
                e_parts.append(jnp.exp2(sink - m))
            probs.append(jnp.concatenate(p_parts, axis=0))
            sink_terms.append(e_parts)
        return probs, sink_terms

    def attend(unit, probs, sink_terms):
        c, kvh = unit
        rows = slice(c * CHUNK, (c + 1) * CHUNK)
        band = slice(c * CHUNK, (c + 2) * CHUNK)
        od = (_dot(probs[0], vx_ref[prev, 2 * kvh, band, :])
              + _dot(probs[1], vx_ref[prev, 2 * kvh + 1, band, :]))
        for si in range(SLABS_PER_KV):
            s = kvh * SLABS_PER_KV + si
            blk = slice(si * CHUNK, (si + 1) * CHUNK)
            denom = od[blk, LANES:] + jnp.where(low_head_c, sink_terms[0][si], sink_terms[1][si])
            gate = gate_ref[prev, rows, s * LANES:(s + 1) * LANES]
            y_ref[rows, s * LANES:(s + 1) * LANES] = (
                od[blk, :LANES] * gate / (denom * (1.0 + jnp.exp(-gate)))).astype(_BF16)

    def out_part(kvh):
        cols = slice(kvh * SLABS_PER_KV * LANES, (kvh + 1) * SLABS_PER_KV * LANES)
        return _dot(y_ref[:, cols], wout_ref[cols, :])

    s0 = scores(units[0])
    project_kv()
    s1 = scores(units[1])
    project_q(0)
    p0 = softmax(units[0], s0)
    attend(units[0], *p0)
    s2 = scores(units[2])
    project_q(1)
    p1 = softmax(units[1], s1)
    attend(units[1], *p1)
    s3 = scores(units[3])
    project_q(2)
    p2 = softmax(units[2], s2)
    attend(units[2], *p2)
    project_q(3)
    project_gate(0)
    acc = out_part(0)
    p3 = softmax(units[3], s3)
    attend(units[3], *p3)
    project_gate(1)
    out = hp_ref[...] + acc + out_part(1)
    o_ref[...] = out * lax.rsqrt(jnp.mean(out * out, axis=-1, keepdims=True) + EPS) * fg_ref[...]
    project_gate(2)
    project_gate(3)


def _layer_b(h2d, seq, sinks, kv_g, b_g, w_kv, b_kv, w_in, b_q, w_out, f_g, cos, sin):
    tm = B_TOKENS_PER_STEP
    n_tiles = h2d.shape[0] // tm
    tiles_per_seq = seq // tm
    const = lambda t: (0, 0)
    cur = lambda t: (jnp.minimum(t, n_tiles - 1), 0)
    prv = lambda t: (jnp.maximum(t - 1, 0), 0)
    tab = lambda t: (jnp.minimum(t, n_tiles - 1) % tiles_per_seq, 0)
    planes = 2 * N_KV_HEADS
    return pl.pallas_call(
        functools.partial(_layer_b_kernel, tiles_per_seq=tiles_per_seq),
        out_shape=jax.ShapeDtypeStruct(h2d.shape, _F32),
        grid=(n_tiles + 1,),
        in_specs=[
            pl.BlockSpec(memory_space=pltpu.SMEM),
            pl.BlockSpec((tm, D_MODEL), cur),
            pl.BlockSpec((tm, D_MODEL), prv),
            pl.BlockSpec((1, D_MODEL), const),
            pl.BlockSpec((1, D_MODEL), const),
            pl.BlockSpec(memory_space=pl.ANY),
            pl.BlockSpec((1, 2 * KV_WIDTH), const),
            pl.BlockSpec(memory_space=pl.ANY),
            pl.BlockSpec((1, B_WIDTH), const),
            pl.BlockSpec(memory_space=pl.ANY),
            pl.BlockSpec((1, D_MODEL), const),
            pl.BlockSpec((tm, LANES), tab),
            pl.BlockSpec((tm, LANES), tab),
        ],
        out_specs=pl.BlockSpec((tm, D_MODEL), prv),
        scratch_shapes=[
            pltpu.VMEM((2, planes, LANES, CHUNK + tm), _BF16),
            pltpu.VMEM((2, planes, CHUNK + tm, 2 * LANES), _BF16),
            pltpu.VMEM((2, tm, B_WIDTH), _BF16),
            pltpu.VMEM((2, tm, B_WIDTH), _F32),
            pltpu.VMEM((tm, B_WIDTH), _BF16),
            pltpu.VMEM((D_MODEL, 2 * KV_WIDTH), _BF16),
            pltpu.VMEM((D_MODEL, 2 * B_WIDTH), _BF16),
            pltpu.VMEM((B_WIDTH, D_MODEL), _BF16),
        ],
        compiler_params=pltpu.CompilerParams(
            dimension_semantics=("arbitrary",), vmem_limit_bytes=VMEM_LIMIT_BYTES),
        name="swa_layer",
    )(sinks, h2d, h2d, kv_g, b_g, w_kv, b_kv, w_in, b_q, w_out, f_g, cos, sin)


def _rope_tables(seq):
    inv_freq = ROPE_THETA ** (-np.arange(0, HEAD_DIM, 2, dtype=np.float64) / HEAD_DIM)
    ang = np.arange(seq, dtype=np.float64)[:, None] * inv_freq[None, :]
    cos = np.cos(ang)
    sin = np.sin(ang)
    cos_slab = np.tile(np.concatenate([cos, cos], axis=-1), (1, HEADS_PER_SLAB))
    sin_slab = np.tile(np.concatenate([-sin, sin], axis=-1), (1, HEADS_PER_SLAB))
    assert cos_slab.shape == (seq, LANES)
    return jnp.asarray(cos_slab, dtype=_F32), jnp.asarray(sin_slab, dtype=_F32)


def kernel(x, a_norm_g, a_w_in, a_ln_g, a_ln_b, a_ws, a_bs, a_w_out, kv_norm_g, w_kv, b_kv,
           b_norm_g, b_w_in, b_bq, b_sinks, b_w_out, final_norm_g):
    batch, seq, d = x.shape
    assert d == D_MODEL
    for tile in (A_TOKENS_PER_STEP, B_TOKENS_PER_STEP):
        assert seq % tile == 0 and tile % CHUNK == 0
    assert a_w_in.shape[0] == 1 and b_w_in.shape[0] == 1
    row = lambda p: p.reshape(1, -1)

    x2 = x.reshape(batch * seq, d)
    h = _layer_a(x2, row(a_norm_g[0]), a_w_in[0], row(a_ln_g[0]), row(a_ln_b[0]),
                 a_ws[0], a_bs[0].T, a_w_out[0])
    cos, sin = _rope_tables(seq)
    out = _layer_b(h, seq, b_sinks[0], row(kv_norm_g), row(b_norm_g[0]),
                   w_kv, row(b_kv), b_w_in[0], row(b_bq[0]),
                   b_w_out[0], row(final_norm_g), cos, sin)
    return out.reshape(batch, seq, d)
```

```python
import functools

import jax
import jax.numpy as jnp
import numpy as np
from jax import lax
from jax.experimental import pallas as pl
from jax.experimental.pallas import tpu as pltpu

D_MODEL = 1024
CHUNK = 128
A_WIDTH = 2048
A_GROUPS = 16
HEAD_DIM = 64
N_Q_HEADS = 16
N_KV_HEADS = 2
B_WIDTH = 1024
KV_WIDTH = 128
ROPE_THETA = 10000.0
EPS = 1e-5
LOG2_E = 1.4426950408889634

LANES = 128
HEADS_PER_SLAB = LANES // HEAD_DIM
N_SLABS = B_WIDTH // LANES
SLABS_PER_KV = N_SLABS // N_KV_HEADS

A_TOKENS_PER_STEP = 512
B_TOKENS_PER_STEP = 512
B_UNITS = N_KV_HEADS * B_TOKENS_PER_STEP // CHUNK


def _b_program_order():
    assert B_UNITS == 8
    text = ("scores0 kv scores1 q0 softmax0 attend0 scores2 softmax1 attend1 scores3 q1 "
            "softmax2 attend2 scores4 softmax3 attend3 scores5 q2 out0 softmax4 attend4 scores6 "
            "softmax5 attend5 scores7 q3 softmax6 attend6 gate0 softmax7 attend7 gate1 out1 final "
            "gate2 gate3")
    order = []
    for tok in text.split():
        op = tok.rstrip("0123456789")
        order.append((op, int(tok[len(op):]) if len(tok) > len(op) else None))
    return tuple(order)


B_PROGRAM_ORDER = _b_program_order()
P_BLOCK = 256
A_COL_BLOCK = 512
A_ROW_SPLIT = 2
VMEM_LIMIT_BYTES = 56 * 1024 * 1024
WEIGHT_STAGE_BYTES = 2 * 1024 * 1024

_F32 = jnp.float32
_BF16 = jnp.bfloat16


def _dot(a, b):
    return jnp.dot(a, b, preferred_element_type=_F32)


def _silu(g):
    return g / (1.0 + jnp.exp(-g))


def _load_weight_as_bf16(w_hbm, w_ref):
    k, n = w_hbm.shape
    rows = min(k, pl.next_power_of_2(WEIGHT_STAGE_BYTES // (4 * n) + 1) // 2)
    assert k % rows == 0 and rows % 16 == 0
    n_chunks = k // rows

    def body(stage, sem):
        def copy(i):
            return pltpu.make_async_copy(w_hbm.at[pl.ds(i * rows, rows), :], stage.at[i % 2], sem.at[i % 2])

        copy(0).start()
        for i in range(n_chunks):
            if i + 1 < n_chunks:
                copy(i + 1).start()
            copy(i).wait()
            w_ref[i * rows:(i + 1) * rows, :] = stage[i % 2].astype(_BF16)

    pl.run_scoped(body, pltpu.VMEM((2, rows, n), _F32), pltpu.SemaphoreType.DMA((2,)))


def _layer_a_kernel(x_ref, ng_ref, win_hbm, lng_ref, lnb_ref, ws_ref, bst_ref, wout_hbm,
                    o_ref, vn_ref, y_ref, win_ref, wout_ref):
    @pl.when(pl.program_id(0) == 0)
    def _():
        _load_weight_as_bf16(win_hbm, win_ref)
        _load_weight_as_bf16(wout_hbm, wout_ref)

    tm = x_ref.shape[0]
    half = tm // A_ROW_SPLIT
    n_blocks = A_WIDTH // A_COL_BLOCK
    groups_per_block = A_COL_BLOCK // CHUNK

    row = lax.broadcasted_iota(jnp.int32, (CHUNK, CHUNK), 0)
    col = lax.broadcasted_iota(jnp.int32, (CHUNK, CHUNK), 1)
    causal = col <= row

    def normed(r0):
        x = x_ref[r0:r0 + half, :]
        return (x * lax.rsqrt(jnp.mean(x * x, axis=-1, keepdims=True) + EPS) * ng_ref[...]).astype(_BF16)

    def layer_norm_v(r0, v):
        mu = jnp.mean(v, axis=-1, keepdims=True)
        vc = v - mu
        var = jnp.mean(vc * vc, axis=-1, keepdims=True)
        vn_ref[r0:r0 + half, :] = (vc * lax.rsqrt(var + EPS) * lng_ref[...] + lnb_ref[...]).astype(_BF16)

    def gated_block(r0, h1, j):
        c0 = j * A_COL_BLOCK
        u = _dot(h1, win_ref[:, c0:c0 + A_COL_BLOCK])
        gt = _dot(h1, win_ref[:, 2 * A_WIDTH + c0:2 * A_WIDTH + c0 + A_COL_BLOCK])
        for gi in range(groups_per_block):
            g = j * groups_per_block + gi
            wsm = jnp.where(causal, ws_ref[g], 0.0).astype(_BF16)
            bias = jnp.broadcast_to(bst_ref[:, g:g + 1], (CHUNK, CHUNK))
            for c in range(half // CHUNK):
                rows = slice(c * CHUNK, (c + 1) * CHUNK)
                tile_rows = slice(r0 + c * CHUNK, r0 + (c + 1) * CHUNK)
                cols = slice(gi * CHUNK, (gi + 1) * CHUNK)
                sv = _dot(wsm, vn_ref[tile_rows, g * CHUNK:(g + 1) * CHUNK]) + bias
                y_ref[tile_rows, g * CHUNK:(g + 1) * CHUNK] = (
                    u[rows, cols] * sv * _silu(gt[rows, cols])).astype(_BF16)

    starts = [i * half for i in range(A_ROW_SPLIT)]
    h1 = [normed(r0) for r0 in starts]
    v = [_dot(h, win_ref[:, A_WIDTH:2 * A_WIDTH]) for h in h1]
    for r0, vi in zip(starts, v):
        layer_norm_v(r0, vi)
    for j in range(n_blocks):
        for r0, h in zip(starts, h1):
            gated_block(r0, h, j)
    for r0 in starts:
        o_ref[r0:r0 + half, :] = x_ref[r0:r0 + half, :] + _dot(y_ref[r0:r0 + half, :], wout_ref[...])


def _layer_a(x2, norm_g, w_in, ln_g, ln_b, ws, bs_t, w_out):
    t = x2.shape[0]
    tm = A_TOKENS_PER_STEP
    const = lambda i: (0, 0)
    return pl.pallas_call(
        _layer_a_kernel,
        out_shape=jax.ShapeDtypeStruct((t, D_MODEL), _F32),
        grid=(t // tm,),
        in_specs=[
            pl.BlockSpec((tm, D_MODEL), lambda i: (i, 0)),
            pl.BlockSpec((1, D_MODEL), const),
            pl.BlockSpec(memory_space=pl.ANY),
            pl.BlockSpec((1, A_WIDTH), const),
            pl.BlockSpec((1, A_WIDTH), const),
            pl.BlockSpec((A_GROUPS, CHUNK, CHUNK), lambda i: (0, 0, 0)),
            pl.BlockSpec((CHUNK, A_GROUPS), const),
            pl.BlockSpec(memory_space=pl.ANY),
        ],
        out_specs=pl.BlockSpec((tm, D_MODEL), lambda i: (i, 0)),
        scratch_shapes=[
            pltpu.VMEM((tm, A_WIDTH), _BF16),
            pltpu.VMEM((tm, A_WIDTH), _BF16),
            pltpu.VMEM((D_MODEL, 3 * A_WIDTH), _BF16),
            pltpu.VMEM((A_WIDTH, D_MODEL), _BF16),
        ],
        compiler_params=pltpu.CompilerParams(
            dimension_semantics=("arbitrary",), vmem_limit_bytes=VMEM_LIMIT_BYTES),
        name="gmlp_layer",
    )(x2, norm_g, w_in, ln_g, ln_b, ws, bs_t, w_out)


def _layer_b_kernel(sinks_ref, hc_ref, hp_ref, kvg_ref, bg_ref, wkv_hbm, bkv_ref, win_hbm, bq_ref,
                    wout_hbm, fg_ref, cos_ref, sin_ref,
                    o_ref, kt_ref, vx_ref, q_ref, gate_ref, y_ref, wkv_ref, win_ref, wout_ref,
                    *, tiles_per_seq):
    t = pl.program_id(0)

    @pl.when(t == 0)
    def _():
        _load_weight_as_bf16(wkv_hbm, wkv_ref)
        _load_weight_as_bf16(win_hbm, win_ref)
        _load_weight_as_bf16(wout_hbm, wout_ref)

    refs = (sinks_ref, hc_ref, hp_ref, kvg_ref, bg_ref, wkv_ref, bkv_ref, win_ref, bq_ref,
            wout_ref, fg_ref, cos_ref, sin_ref, o_ref, kt_ref, vx_ref, q_ref, gate_ref, y_ref)

    @pl.when(t % 2 == 0)
    def _():
        _layer_b_step(*refs, t=t, slot=0, tiles_per_seq=tiles_per_seq)

    @pl.when(t % 2 == 1)
    def _():
        _layer_b_step(*refs, t=t, slot=1, tiles_per_seq=tiles_per_seq)


def _layer_b_step(sinks_ref, hc_ref, hp_ref, kvg_ref, bg_ref, wkv_ref, bkv_ref, win_ref, bq_ref,
                  wout_ref, fg_ref, cos_ref, sin_ref,
                  o_ref, kt_ref, vx_ref, q_ref, gate_ref, y_ref, *, t, slot, tiles_per_seq):
    tm = hc_ref.shape[0]
    prev = 1 - slot
    new = slice(CHUNK, CHUNK + tm)
    planes = 2 * N_KV_HEADS

    if slot == 0:
        @pl.when(t == 0)
        def _():
            kt_ref[prev] = jnp.zeros((planes, LANES, CHUNK + tm), _BF16)
            vx_ref[prev] = jnp.zeros((planes, CHUNK + tm, 2 * LANES), _BF16)
            q_ref[prev] = jnp.zeros((tm, B_WIDTH), _BF16)
            gate_ref[prev] = jnp.zeros((tm, B_WIDTH), _F32)

    first_in_seq = (t % tiles_per_seq) == 0

    @pl.when(first_in_seq)
    def _():
        kt_ref[slot, :, :, 0:CHUNK] = jnp.zeros((planes, LANES, CHUNK), _BF16)
        vx_ref[slot, :, 0:CHUNK, :] = jnp.zeros((planes, CHUNK, 2 * LANES), _BF16)

    @pl.when(jnp.logical_not(first_in_seq))
    def _():
        kt_ref[slot, :, :, 0:CHUNK] = kt_ref[prev, :, :, tm:tm + CHUNK]
        vx_ref[slot, :, 0:CHUNK, :] = vx_ref[prev, :, tm:tm + CHUNK, :]

    h = hc_ref[...]
    xn = h * lax.rsqrt(jnp.mean(h * h, axis=-1, keepdims=True) + EPS)
    kvn = (xn * kvg_ref[...]).astype(_BF16)
    h2 = (xn * bg_ref[...]).astype(_BF16)

    cos = cos_ref[...]
    sin = sin_ref[...]
    lane = lax.broadcasted_iota(jnp.int32, (tm, LANES), 1)
    first_half = (lane % HEAD_DIM) < (HEAD_DIM // 2)
    low_head = lane < HEAD_DIM

    def rope(x):
        swapped = jnp.where(first_half,
                            pltpu.roll(x, LANES - HEAD_DIM // 2, 1),
                            pltpu.roll(x, HEAD_DIM // 2, 1))
        return x * cos + swapped * sin

    def project_kv():
        kv = _dot(kvn, wkv_ref[...]) + bkv_ref[...]
        k_t = rope(kv[:, :KV_WIDTH]).T
        zero_rows = jnp.zeros((HEAD_DIM, tm), _F32)
        for kvh in range(N_KV_HEADS):
            head_t = k_t[kvh * HEAD_DIM:(kvh + 1) * HEAD_DIM]
            kt_ref[slot, 2 * kvh, :, new] = jnp.concatenate([head_t, zero_rows], axis=0).astype(_BF16)
            kt_ref[slot, 2 * kvh + 1, :, new] = jnp.concatenate([zero_rows, head_t], axis=0).astype(_BF16)
        v = kv[:, KV_WIDTH:]
        v_sw = pltpu.roll(v, HEAD_DIM, 1)
        zero = jnp.zeros_like(v)
        one = jnp.ones_like(v)
        ones_lo = jnp.where(low_head, one, zero)
        ones_hi = jnp.where(low_head, zero, one)
        v_low = (jnp.where(low_head, v, zero), jnp.where(low_head, v_sw, zero))
        v_high = (jnp.where(low_head, zero, v_sw), jnp.where(low_head, zero, v))
        for kvh in range(N_KV_HEADS):
            vx_ref[slot, 2 * kvh, new, :] = jnp.concatenate([v_low[kvh], ones_lo], axis=1).astype(_BF16)
            vx_ref[slot, 2 * kvh + 1, new, :] = jnp.concatenate([v_high[kvh], ones_hi], axis=1).astype(_BF16)

    q_scale = (HEAD_DIM ** -0.5) * LOG2_E

    def project_q(b):
        cols = slice(b * P_BLOCK, (b + 1) * P_BLOCK)
        zq = _dot(h2, win_ref[:, cols]) + bq_ref[:, cols]
        for i in range(P_BLOCK // LANES):
            s = b * (P_BLOCK // LANES) + i
            q_ref[slot, :, s * LANES:(s + 1) * LANES] = (
                rope(zq[:, i * LANES:(i + 1) * LANES]) * q_scale).astype(_BF16)

    def project_gate(b):
        cols = slice(b * P_BLOCK, (b + 1) * P_BLOCK)
        gate_ref[slot, :, cols] = _dot(h2, win_ref[:, B_WIDTH + b * P_BLOCK:B_WIDTH + (b + 1) * P_BLOCK])

    qi = lax.broadcasted_iota(jnp.int32, (CHUNK, 2 * CHUNK), 0)
    kj = lax.broadcasted_iota(jnp.int32, (CHUNK, 2 * CHUNK), 1)
    in_window = (kj > qi) & (kj <= qi + CHUNK)
    neg_inf = jnp.float32(-jnp.inf)
    bias = jnp.where(in_window, 0.0, neg_inf)
    prev_first_in_seq = ((t + tiles_per_seq - 1) % tiles_per_seq) == 0
    bias_first = jnp.where(in_window & ((kj >= CHUNK) | jnp.logical_not(prev_first_in_seq)), 0.0, neg_inf)
    low_head_c = lax.broadcasted_iota(jnp.int32, (CHUNK, LANES), 1) < HEAD_DIM

    units = [(c, kvh) for kvh in range(N_KV_HEADS) for c in range(tm // CHUNK)]
    assert len(units) == B_UNITS and B_WIDTH // P_BLOCK == 4, "B_PROGRAM_ORDER is written for these"

    def scores(unit):
        c, kvh = unit
        rows = slice(c * CHUNK, (c + 1) * CHUNK)
        band = slice(c * CHUNK, (c + 2) * CHUNK)
        q_stack = jnp.concatenate(
            [q_ref[prev, rows, s * LANES:(s + 1) * LANES]
             for s in range(kvh * SLABS_PER_KV, (kvh + 1) * SLABS_PER_KV)], axis=0)
        return [_dot(q_stack, kt_ref[prev, 2 * kvh + half, :, band]) for half in range(HEADS_PER_SLAB)]

    def softmax(unit, sc_halves):
        c, kvh = unit
        chunk_bias = bias_first if c == 0 else bias
        probs = []
        sink_terms = []
        for half, sc in enumerate(sc_halves):
            p_parts = []
            e_parts = []
            for si in range(SLABS_PER_KV):
                s = kvh * SLABS_PER_KV + si
                sink = sinks_ref[s * HEADS_PER_SLAB + half] * LOG2_E
                sb = sc[si * CHUNK:(si + 1) * CHUNK] + chunk_bias
                m = jnp.maximum(jnp.max(sb, axis=-1, keepdims=True), sink)
                p_parts.append(jnp.exp2(sb - m).astype(_BF16))
                e_parts.append(jnp.exp2(sink - m))
            probs.append(jnp.concatenate(p_parts, axis=0))
            sink_terms.append(e_parts)
        return probs, sink_terms

    def attend(unit, probs, sink_terms):
        c, kvh = unit
        rows = slice(c * CHUNK, (c + 1) * CHUNK)
        band = slice(c * CHUNK, (c + 2) * CHUNK)
        od = (_dot(probs[0], vx_ref[prev, 2 * kvh, band, :])
              + _dot(probs[1], vx_ref[prev, 2 * kvh + 1, band, :]))
        for si in range(SLABS_PER_KV):
            s = kvh * SLABS_PER_KV + si
            blk = slice(si * CHUNK, (si + 1) * CHUNK)
            denom = od[blk, LANES:] + jnp.where(low_head_c, sink_terms[0][si], sink_terms[1][si])
            gate = gate_ref[prev, rows, s * LANES:(s + 1) * LANES]
            y_ref[rows, s * LANES:(s + 1) * LANES] = (
                od[blk, :LANES] * gate / (denom * (1.0 + jnp.exp(-gate)))).astype(_BF16)

    def out_part(kvh):
        cols = slice(kvh * SLABS_PER_KV * LANES, (kvh + 1) * SLABS_PER_KV * LANES)
        return _dot(y_ref[:, cols], wout_ref[cols, :])

    sc, probs, partial = {}, {}, {}
    for op, k in B_PROGRAM_ORDER:
        if op == "scores":
            sc[k] = scores(units[k])
        elif op == "softmax":
            probs[k] = softmax(units[k], sc.pop(k))
        elif op == "attend":
            attend(units[k], *probs.pop(k))
        elif op == "kv":
            project_kv()
        elif op == "q":
            project_q(k)
        elif op == "gate":
            project_gate(k)
        elif op == "out":
            partial[k] = out_part(k)
        else:
            assert op == "final"
            out = hp_ref[...] + partial.pop(0) + partial.pop(1)
            o_ref[...] = out * lax.rsqrt(jnp.mean(out * out, axis=-1, keepdims=True) + EPS) * fg_ref[...]


def _layer_b(h2d, seq, sinks, kv_g, b_g, w_kv, b_kv, w_in, b_q, w_out, f_g, cos, sin):
    tm = B_TOKENS_PER_STEP
    n_tiles = h2d.shape[0] // tm
    tiles_per_seq = seq // tm
    const = lambda t: (0, 0)
    cur = lambda t: (jnp.minimum(t, n_tiles - 1), 0)
    prv = lambda t: (jnp.maximum(t - 1, 0), 0)
    tab = lambda t: (jnp.minimum(t, n_tiles - 1) % tiles_per_seq, 0)
    planes = 2 * N_KV_HEADS
    return pl.pallas_call(
        functools.partial(_layer_b_kernel, tiles_per_seq=tiles_per_seq),
        out_shape=jax.ShapeDtypeStruct(h2d.shape, _F32),
        grid=(n_tiles + 1,),
        in_specs=[
            pl.BlockSpec(memory_space=pltpu.SMEM),
            pl.BlockSpec((tm, D_MODEL), cur),
            pl.BlockSpec((tm, D_MODEL), prv),
            pl.BlockSpec((1, D_MODEL), const),
            pl.BlockSpec((1, D_MODEL), const),
            pl.BlockSpec(memory_space=pl.ANY),
            pl.BlockSpec((1, 2 * KV_WIDTH), const),
            pl.BlockSpec(memory_space=pl.ANY),
            pl.BlockSpec((1, B_WIDTH), const),
            pl.BlockSpec(memory_space=pl.ANY),
            pl.BlockSpec((1, D_MODEL), const),
            pl.BlockSpec((tm, LANES), tab),
            pl.BlockSpec((tm, LANES), tab),
        ],
        out_specs=pl.BlockSpec((tm, D_MODEL), prv),
        scratch_shapes=[
            pltpu.VMEM((2, planes, LANES, CHUNK + tm), _BF16),
            pltpu.VMEM((2, planes, CHUNK + tm, 2 * LANES), _BF16),
            pltpu.VMEM((2, tm, B_WIDTH), _BF16),
            pltpu.VMEM((2, tm, B_WIDTH), _F32),
            pltpu.VMEM((tm, B_WIDTH), _BF16),
            pltpu.VMEM((D_MODEL, 2 * KV_WIDTH), _BF16),
            pltpu.VMEM((D_MODEL, 2 * B_WIDTH), _BF16),
            pltpu.VMEM((B_WIDTH, D_MODEL), _BF16),
        ],
        compiler_params=pltpu.CompilerParams(
            dimension_semantics=("arbitrary",), vmem_limit_bytes=VMEM_LIMIT_BYTES),
        name="swa_layer",
    )(sinks, h2d, h2d, kv_g, b_g, w_kv, b_kv, w_in, b_q, w_out, f_g, cos, sin)


def _rope_tables(seq):
    inv_freq = ROPE_THETA ** (-np.arange(0, HEAD_DIM, 2, dtype=np.float64) / HEAD_DIM)
    ang = np.arange(seq, dtype=np.float64)[:, None] * inv_freq[None, :]
    cos = np.cos(ang)
    sin = np.sin(ang)
    cos_slab = np.tile(np.concatenate([cos, cos], axis=-1), (1, HEADS_PER_SLAB))
    sin_slab = np.tile(np.concatenate([-sin, sin], axis=-1), (1, HEADS_PER_SLAB))
    assert cos_slab.shape == (seq, LANES)
    return jnp.asarray(cos_slab, dtype=_F32), jnp.asarray(sin_slab, dtype=_F32)


def kernel(x, a_norm_g, a_w_in, a_ln_g, a_ln_b, a_ws, a_bs, a_w_out, kv_norm_g, w_kv, b_kv,
           b_norm_g, b_w_in, b_bq, b_sinks, b_w_out, final_norm_g):
    batch, seq, d = x.shape
    assert d == D_MODEL
    for tile in (A_TOKENS_PER_STEP, B_TOKENS_PER_STEP):
        assert seq % tile == 0 and tile % CHUNK == 0
    assert a_w_in.shape[0] == 1 and b_w_in.shape[0] == 1
    row = lambda p: p.reshape(1, -1)

    x2 = x.reshape(batch * seq, d)
    h = _layer_a(x2, row(a_norm_g[0]), a_w_in[0], row(a_ln_g[0]), row(a_ln_b[0]),
                 a_ws[0], a_bs[0].T, a_w_out[0])
    cos, sin = _rope_tables(seq)
    out = _layer_b(h, seq, b_sinks[0], row(kv_norm_g), row(b_norm_g[0]),
                   w_kv, row(b_kv), b_w_in[0], row(b_bq[0]),
                   b_w_out[0], row(final_norm_g), cos, sin)
    return out.reshape(batch, seq, d)
```

```python
import functools

import jax
import jax.numpy as jnp
import numpy as np
from jax import lax
from jax.experimental import pallas as pl
from jax.experimental.pallas import tpu as pltpu

D_MODEL = 1024
CHUNK = 128
A_WIDTH = 2048
A_GROUPS = 16
HEAD_DIM = 64
N_Q_HEADS = 16
N_KV_HEADS = 2
B_WIDTH = 1024
KV_WIDTH = 128
ROPE_THETA = 10000.0
EPS = 1e-5
LOG2_E = 1.4426950408889634

LANES = 128
HEADS_PER_SLAB = LANES // HEAD_DIM
N_SLABS = B_WIDTH // LANES
SLABS_PER_KV = N_SLABS // N_KV_HEADS

A_TOKENS_PER_STEP = 512
B_TOKENS_PER_STEP = 256
P_BLOCK = 256
A_COL_BLOCK = 512
A_ROW_SPLIT = 2
VMEM_LIMIT_BYTES = 56 * 1024 * 1024
WEIGHT_STAGE_BYTES = 2 * 1024 * 1024

_F32 = jnp.float32
_BF16 = jnp.bfloat16


def _dot(a, b):
    return jnp.dot(a, b, preferred_element_type=_F32)


def _silu(g):
    half_g = 0.5 * g
    return half_g * (1.0 + jnp.tanh(half_g))


def _load_weight_as_bf16(w_hbm, w_ref):
    k, n = w_hbm.shape
    rows = min(k, pl.next_power_of_2(WEIGHT_STAGE_BYTES // (4 * n) + 1) // 2)
    assert k % rows == 0 and rows % 16 == 0
    n_chunks = k // rows

    def body(stage, sem):
        def copy(i):
            return pltpu.make_async_copy(w_hbm.at[pl.ds(i * rows, rows), :], stage.at[i % 2], sem.at[i % 2])

        copy(0).start()
        for i in range(n_chunks):
            if i + 1 < n_chunks:
                copy(i + 1).start()
            copy(i).wait()
            w_ref[i * rows:(i + 1) * rows, :] = stage[i % 2].astype(_BF16)

    pl.run_scoped(body, pltpu.VMEM((2, rows, n), _F32), pltpu.SemaphoreType.DMA((2,)))


def _layer_a_kernel(x_ref, ng_ref, win_hbm, lng_ref, lnb_ref, ws_ref, bst_ref, wout_hbm,
                    o_ref, vn_ref, y_ref, win_ref, wout_ref):
    @pl.when(pl.program_id(0) == 0)
    def _():
        _load_weight_as_bf16(win_hbm, win_ref)
        _load_weight_as_bf16(wout_hbm, wout_ref)

    tm = x_ref.shape[0]
    half = tm // A_ROW_SPLIT
    n_blocks = A_WIDTH // A_COL_BLOCK
    groups_per_block = A_COL_BLOCK // CHUNK

    row = lax.broadcasted_iota(jnp.int32, (CHUNK, CHUNK), 0)
    col = lax.broadcasted_iota(jnp.int32, (CHUNK, CHUNK), 1)
    causal = col <= row

    def normed(r0):
        x = x_ref[r0:r0 + half, :]
        return (x * lax.rsqrt(jnp.mean(x * x, axis=-1, keepdims=True) + EPS) * ng_ref[...]).astype(_BF16)

    def layer_norm_v(r0, v):
        mu = jnp.mean(v, axis=-1, keepdims=True)
        vc = v - mu
        var = jnp.mean(vc * vc, axis=-1, keepdims=True)
        vn_ref[r0:r0 + half, :] = (vc * lax.rsqrt(var + EPS) * lng_ref[...] + lnb_ref[...]).astype(_BF16)

    def gated_block(r0, h1, j):
        c0 = j * A_COL_BLOCK
        u = _dot(h1, win_ref[:, c0:c0 + A_COL_BLOCK])
        gt = _dot(h1, win_ref[:, 2 * A_WIDTH + c0:2 * A_WIDTH + c0 + A_COL_BLOCK])
        for gi in range(groups_per_block):
            g = j * groups_per_block + gi
            wsm = jnp.where(causal, ws_ref[g], 0.0).astype(_BF16)
            bias = jnp.broadcast_to(bst_ref[:, g:g + 1], (CHUNK, CHUNK))
            for c in range(half // CHUNK):
                rows = slice(c * CHUNK, (c + 1) * CHUNK)
                tile_rows = slice(r0 + c * CHUNK, r0 + (c + 1) * CHUNK)
                cols = slice(gi * CHUNK, (gi + 1) * CHUNK)
                sv = _dot(wsm, vn_ref[tile_rows, g * CHUNK:(g + 1) * CHUNK]) + bias
                y_ref[tile_rows, g * CHUNK:(g + 1) * CHUNK] = (
                    u[rows, cols] * sv * _silu(gt[rows, cols])).astype(_BF16)

    starts = [i * half for i in range(A_ROW_SPLIT)]
    h1 = [normed(r0) for r0 in starts]
    v = [_dot(h, win_ref[:, A_WIDTH:2 * A_WIDTH]) for h in h1]
    for r0, vi in zip(starts, v):
        layer_norm_v(r0, vi)
    for j in range(n_blocks):
        for r0, h in zip(starts, h1):
            gated_block(r0, h, j)
    for r0 in starts:
        o_ref[r0:r0 + half, :] = x_ref[r0:r0 + half, :] + _dot(y_ref[r0:r0 + half, :], wout_ref[...])


def _layer_a(x2, norm_g, w_in, ln_g, ln_b, ws, bs_t, w_out):
    t = x2.shape[0]
    tm = A_TOKENS_PER_STEP
    const = lambda i: (0, 0)
    return pl.pallas_call(
        _layer_a_kernel,
        out_shape=jax.ShapeDtypeStruct((t, D_MODEL), _F32),
        grid=(t // tm,),
        in_specs=[
            pl.BlockSpec((tm, D_MODEL), lambda i: (i, 0)),
            pl.BlockSpec((1, D_MODEL), const),
            pl.BlockSpec(memory_space=pl.ANY),
            pl.BlockSpec((1, A_WIDTH), const),
            pl.BlockSpec((1, A_WIDTH), const),
            pl.BlockSpec((A_GROUPS, CHUNK, CHUNK), lambda i: (0, 0, 0)),
            pl.BlockSpec((CHUNK, A_GROUPS), const),
            pl.BlockSpec(memory_space=pl.ANY),
        ],
        out_specs=pl.BlockSpec((tm, D_MODEL), lambda i: (i, 0)),
        scratch_shapes=[
            pltpu.VMEM((tm, A_WIDTH), _BF16),
            pltpu.VMEM((tm, A_WIDTH), _BF16),
            pltpu.VMEM((D_MODEL, 3 * A_WIDTH), _BF16),
            pltpu.VMEM((A_WIDTH, D_MODEL), _BF16),
        ],
        compiler_params=pltpu.CompilerParams(
            dimension_semantics=("arbitrary",), vmem_limit_bytes=VMEM_LIMIT_BYTES),
        name="gmlp_layer",
    )(x2, norm_g, w_in, ln_g, ln_b, ws, bs_t, w_out)


def _layer_b_kernel(sinks_ref, hc_ref, hp_ref, kvg_ref, bg_ref, wkv_hbm, bkv_ref, win_hbm, bq_ref,
                    wout_hbm, fg_ref, cos_ref, sin_ref,
                    o_ref, kt_ref, vx_ref, q_ref, gate_ref, y_ref, wkv_ref, win_ref, wout_ref,
                    *, tiles_per_seq):
    t = pl.program_id(0)

    @pl.when(t == 0)
    def _():
        _load_weight_as_bf16(wkv_hbm, wkv_ref)
        _load_weight_as_bf16(win_hbm, win_ref)
        _load_weight_as_bf16(wout_hbm, wout_ref)

    refs = (sinks_ref, hc_ref, hp_ref, kvg_ref, bg_ref, wkv_ref, bkv_ref, win_ref, bq_ref,
            wout_ref, fg_ref, cos_ref, sin_ref, o_ref, kt_ref, vx_ref, q_ref, gate_ref, y_ref)

    @pl.when(t % 2 == 0)
    def _():
        _layer_b_step(*refs, t=t, slot=0, tiles_per_seq=tiles_per_seq)

    @pl.when(t % 2 == 1)
    def _():
        _layer_b_step(*refs, t=t, slot=1, tiles_per_seq=tiles_per_seq)


def _layer_b_step(sinks_ref, hc_ref, hp_ref, kvg_ref, bg_ref, wkv_ref, bkv_ref, win_ref, bq_ref,
                  wout_ref, fg_ref, cos_ref, sin_ref,
                  o_ref, kt_ref, vx_ref, q_ref, gate_ref, y_ref, *, t, slot, tiles_per_seq):
    tm = hc_ref.shape[0]
    prev = 1 - slot
    new = slice(CHUNK, CHUNK + tm)
    planes = 2 * N_KV_HEADS

    if slot == 0:
        @pl.when(t == 0)
        def _():
            kt_ref[prev] = jnp.zeros((planes, LANES, CHUNK + tm), _BF16)
            vx_ref[prev] = jnp.zeros((planes, CHUNK + tm, 2 * LANES), _BF16)
            q_ref[prev] = jnp.zeros((tm, B_WIDTH), _BF16)
            gate_ref[prev] = jnp.zeros((tm, B_WIDTH), _F32)

    first_in_seq = (t % tiles_per_seq) == 0

    @pl.when(first_in_seq)
    def _():
        kt_ref[slot, :, :, 0:CHUNK] = jnp.zeros((planes, LANES, CHUNK), _BF16)
        vx_ref[slot, :, 0:CHUNK, :] = jnp.zeros((planes, CHUNK, 2 * LANES), _BF16)

    @pl.when(jnp.logical_not(first_in_seq))
    def _():
        kt_ref[slot, :, :, 0:CHUNK] = kt_ref[prev, :, :, tm:tm + CHUNK]
        vx_ref[slot, :, 0:CHUNK, :] = vx_ref[prev, :, tm:tm + CHUNK, :]

    h = hc_ref[...]
    xn = h * lax.rsqrt(jnp.mean(h * h, axis=-1, keepdims=True) + EPS)
    kvn = (xn * kvg_ref[...]).astype(_BF16)
    h2 = (xn * bg_ref[...]).astype(_BF16)

    cos = cos_ref[...]
    sin = sin_ref[...]
    lane = lax.broadcasted_iota(jnp.int32, (tm, LANES), 1)
    first_half = (lane % HEAD_DIM) < (HEAD_DIM // 2)
    low_head = lane < HEAD_DIM

    def rope(x):
        swapped = jnp.where(first_half,
                            pltpu.roll(x, LANES - HEAD_DIM // 2, 1),
                            pltpu.roll(x, HEAD_DIM // 2, 1))
        return x * cos + swapped * sin

    def project_kv():
        kv = _dot(kvn, wkv_ref[...]) + bkv_ref[...]
        k_t = rope(kv[:, :KV_WIDTH]).T
        zero_rows = jnp.zeros((HEAD_DIM, tm), _F32)
        for kvh in range(N_KV_HEADS):
            head_t = k_t[kvh * HEAD_DIM:(kvh + 1) * HEAD_DIM]
            kt_ref[slot, 2 * kvh, :, new] = jnp.concatenate([head_t, zero_rows], axis=0).astype(_BF16)
            kt_ref[slot, 2 * kvh + 1, :, new] = jnp.concatenate([zero_rows, head_t], axis=0).astype(_BF16)
        v = kv[:, KV_WIDTH:]
        v_sw = pltpu.roll(v, HEAD_DIM, 1)
        zero = jnp.zeros_like(v)
        one = jnp.ones_like(v)
        ones_lo = jnp.where(low_head, one, zero)
        ones_hi = jnp.where(low_head, zero, one)
        v_low = (jnp.where(low_head, v, zero), jnp.where(low_head, v_sw, zero))
        v_high = (jnp.where(low_head, zero, v_sw), jnp.where(low_head, zero, v))
        for kvh in range(N_KV_HEADS):
            vx_ref[slot, 2 * kvh, new, :] = jnp.concatenate([v_low[kvh], ones_lo], axis=1).astype(_BF16)
            vx_ref[slot, 2 * kvh + 1, new, :] = jnp.concatenate([v_high[kvh], ones_hi], axis=1).astype(_BF16)

    q_scale = (HEAD_DIM ** -0.5) * LOG2_E

    def project_q(b):
        cols = slice(b * P_BLOCK, (b + 1) * P_BLOCK)
        zq = _dot(h2, win_ref[:, cols]) + bq_ref[:, cols]
        for i in range(P_BLOCK // LANES):
            s = b * (P_BLOCK // LANES) + i
            q_ref[slot, :, s * LANES:(s + 1) * LANES] = (
                rope(zq[:, i * LANES:(i + 1) * LANES]) * q_scale).astype(_BF16)

    def project_gate(b):
        cols = slice(b * P_BLOCK, (b + 1) * P_BLOCK)
        gate_ref[slot, :, cols] = _dot(h2, win_ref[:, B_WIDTH + b * P_BLOCK:B_WIDTH + (b + 1) * P_BLOCK])

    qi = lax.broadcasted_iota(jnp.int32, (CHUNK, 2 * CHUNK), 0)
    kj = lax.broadcasted_iota(jnp.int32, (CHUNK, 2 * CHUNK), 1)
    in_window = (kj > qi) & (kj <= qi + CHUNK)
    neg_inf = jnp.float32(-jnp.inf)
    bias = jnp.where(in_window, 0.0, neg_inf)
    prev_first_in_seq = ((t + tiles_per_seq - 1) % tiles_per_seq) == 0
    bias_first = jnp.where(in_window & ((kj >= CHUNK) | jnp.logical_not(prev_first_in_seq)), 0.0, neg_inf)
    low_head_c = lax.broadcasted_iota(jnp.int32, (CHUNK, LANES), 1) < HEAD_DIM

    units = [(c, kvh) for kvh in range(N_KV_HEADS) for c in range(tm // CHUNK)]
    assert len(units) == 4 and B_WIDTH // P_BLOCK == 4, "the program order below is written for 4 + 4"

    def scores(unit):
        c, kvh = unit
        rows = slice(c * CHUNK, (c + 1) * CHUNK)
        band = slice(c * CHUNK, (c + 2) * CHUNK)
        q_stack = jnp.concatenate(
            [q_ref[prev, rows, s * LANES:(s + 1) * LANES]
             for s in range(kvh * SLABS_PER_KV, (kvh + 1) * SLABS_PER_KV)], axis=0)
        return [_dot(q_stack, kt_ref[prev, 2 * kvh + half, :, band]) for half in range(HEADS_PER_SLAB)]

    def softmax(unit, sc_halves):
        c, kvh = unit
        chunk_bias = bias_first if c == 0 else bias
        probs = []
        sink_terms = []
        for half, sc in enumerate(sc_halves):
            p_parts = []
            e_parts = []
            for si in range(SLABS_PER_KV):
                s = kvh * SLABS_PER_KV + si
                sink = sinks_ref[s * HEADS_PER_SLAB + half] * LOG2_E
                sb = sc[si * CHUNK:(si + 1) * CHUNK] + chunk_bias
                m = jnp.maximum(jnp.max(sb, axis=-1, keepdims=True), sink)
                p_parts.append(jnp.exp2(sb - m).astype(_BF16))
                e_parts.append(jnp.exp2(sink - m))
            probs.append(jnp.concatenate(p_parts, axis=0))
            sink_terms.append(e_parts)
        return probs, sink_terms

    def attend(unit, probs, sink_terms):
        c, kvh = unit
        rows = slice(c * CHUNK, (c + 1) * CHUNK)
        band = slice(c * CHUNK, (c + 2) * CHUNK)
        od = (_dot(probs[0], vx_ref[prev, 2 * kvh, band, :])
              + _dot(probs[1], vx_ref[prev, 2 * kvh + 1, band, :]))
        for si in range(SLABS_PER_KV):
            s = kvh * SLABS_PER_KV + si
            blk = slice(si * CHUNK, (si + 1) * CHUNK)
            denom = od[blk, LANES:] + jnp.where(low_head_c, sink_terms[0][si], sink_terms[1][si])
            gate = gate_ref[prev, rows, s * LANES:(s + 1) * LANES]
            y_ref[rows, s * LANES:(s + 1) * LANES] = (
                od[blk, :LANES] * gate / (denom * (1.0 + jnp.exp(-gate)))).astype(_BF16)

    def out_part(kvh):
        cols = slice(kvh * SLABS_PER_KV * LANES, (kvh + 1) * SLABS_PER_KV * LANES)
        return _dot(y_ref[:, cols], wout_ref[cols, :])

    s0 = scores(units[0])
    project_kv()
    s1 = scores(units[1])
    project_q(0)
    p0 = softmax(units[0], s0)
    attend(units[0], *p0)
    s2 = scores(units[2])
    project_q(1)
    p1 = softmax(units[1], s1)
    attend(units[1], *p1)
    s3 = scores(units[3])
    project_q(2)
    p2 = softmax(units[2], s2)
    attend(units[2], *p2)
    project_q(3)
    project_gate(0)
    acc = out_part(0)
    p3 = softmax(units[3], s3)
    attend(units[3], *p3)
    project_gate(1)
    out = hp_ref[...] + acc + out_part(1)
    o_ref[...] = out * lax.rsqrt(jnp.mean(out * out, axis=-1, keepdims=True) + EPS) * fg_ref[...]
    project_gate(2)
    project_gate(3)


def _layer_b(h2d, seq, sinks, kv_g, b_g, w_kv, b_kv, w_in, b_q, w_out, f_g, cos, sin):
    tm = B_TOKENS_PER_STEP
    n_tiles = h2d.shape[0] // tm
    tiles_per_seq = seq // tm
    const = lambda t: (0, 0)
    cur = lambda t: (jnp.minimum(t, n_tiles - 1), 0)
    prv = lambda t: (jnp.maximum(t - 1, 0), 0)
    tab = lambda t: (jnp.minimum(t, n_tiles - 1) % tiles_per_seq, 0)
    planes = 2 * N_KV_HEADS
    return pl.pallas_call(
        functools.partial(_layer_b_kernel, tiles_per_seq=tiles_per_seq),
        out_shape=jax.ShapeDtypeStruct(h2d.shape, _F32),
        grid=(n_tiles + 1,),
        in_specs=[
            pl.BlockSpec(memory_space=pltpu.SMEM),
            pl.BlockSpec((tm, D_MODEL), cur),
            pl.BlockSpec((tm, D_MODEL), prv),
            pl.BlockSpec((1, D_MODEL), const),
            pl.BlockSpec((1, D_MODEL), const),
            pl.BlockSpec(memory_space=pl.ANY),
            pl.BlockSpec((1, 2 * KV_WIDTH), const),
            pl.BlockSpec(memory_space=pl.ANY),
            pl.BlockSpec((1, B_WIDTH), const),
            pl.BlockSpec(memory_space=pl.ANY),
            pl.BlockSpec((1, D_MODEL), const),
            pl.BlockSpec((tm, LANES), tab),
            pl.BlockSpec((tm, LANES), tab),
        ],
        out_specs=pl.BlockSpec((tm, D_MODEL), prv),
        scratch_shapes=[
            pltpu.VMEM((2, planes, LANES, CHUNK + tm), _BF16),
            pltpu.VMEM((2, planes, CHUNK + tm, 2 * LANES), _BF16),
            pltpu.VMEM((2, tm, B_WIDTH), _BF16),
            pltpu.VMEM((2, tm, B_WIDTH), _F32),
            pltpu.VMEM((tm, B_WIDTH), _BF16),
            pltpu.VMEM((D_MODEL, 2 * KV_WIDTH), _BF16),
            pltpu.VMEM((D_MODEL, 2 * B_WIDTH), _BF16),
            pltpu.VMEM((B_WIDTH, D_MODEL), _BF16),
        ],
        compiler_params=pltpu.CompilerParams(
            dimension_semantics=("arbitrary",), vmem_limit_bytes=VMEM_LIMIT_BYTES),
        name="swa_layer",
    )(sinks, h2d, h2d, kv_g, b_g, w_kv, b_kv, w_in, b_q, w_out, f_g, cos, sin)


def _rope_tables(seq):
    inv_freq = ROPE_THETA ** (-np.arange(0, HEAD_DIM, 2, dtype=np.float64) / HEAD_DIM)
    ang = np.arange(seq, dtype=np.float64)[:, None] * inv_freq[None, :]
    cos = np.cos(ang)
    sin = np.sin(ang)
    cos_slab = np.tile(np.concatenate([cos, cos], axis=-1), (1, HEADS_PER_SLAB))
    sin_slab = np.tile(np.concatenate([-sin, sin], axis=-1), (1, HEADS_PER_SLAB))
    assert cos_slab.shape == (seq, LANES)
    return jnp.asarray(cos_slab, dtype=_F32), jnp.asarray(sin_slab, dtype=_F32)


def kernel(x, a_norm_g, a_w_in, a_ln_g, a_ln_b, a_ws, a_bs, a_w_out, kv_norm_g, w_kv, b_kv,
           b_norm_g, b_w_in, b_bq, b_sinks, b_w_out, final_norm_g):
    batch, seq, d = x.shape
    assert d == D_MODEL
    for tile in (A_TOKENS_PER_STEP, B_TOKENS_PER_STEP):
        assert seq % tile == 0 and tile % CHUNK == 0
    assert a_w_in.shape[0] == 1 and b_w_in.shape[0] == 1
    row = lambda p: p.reshape(1, -1)

    x2 = x.reshape(batch * seq, d)
    h = _layer_a(x2, row(a_norm_g[0]), a_w_in[0], row(a_ln_g[0]), row(a_ln_b[0]),
                 a_ws[0], a_bs[0].T, a_w_out[0])
    cos, sin = _rope_tables(seq)
    out = _layer_b(h, seq, b_sinks[0], row(kv_norm_g), row(b_norm_g[0]),
                   w_kv, row(b_kv), b_w_in[0], row(b_bq[0]),
                   b_w_out[0], row(final_norm_g), cos, sin)
    return out.reshape(batch, seq, d)
```

```python
import functools

import jax
import jax.numpy as jnp
import numpy as np
from jax import lax
from jax.experimental import pallas as pl
from jax.experimental.pallas import tpu as pltpu

D_MODEL = 1024
CHUNK = 128
A_WIDTH = 2048
A_GROUPS = 16
HEAD_DIM = 64
N_Q_HEADS = 16
N_KV_HEADS = 2
B_WIDTH = 1024
KV_WIDTH = 128
ROPE_THETA = 10000.0
EPS = 1e-5
LOG2_E = 1.4426950408889634

LANES = 128
HEADS_PER_SLAB = LANES // HEAD_DIM
N_SLABS = B_WIDTH // LANES
SLABS_PER_KV = N_SLABS // N_KV_HEADS

A_TOKENS_PER_STEP = 512
B_TOKENS_PER_STEP = 256
P_BLOCK = 256
A_COL_BLOCK = 512
A_ROW_SPLIT = 2
VMEM_LIMIT_BYTES = 56 * 1024 * 1024
WEIGHT_STAGE_BYTES = 2 * 1024 * 1024

_F32 = jnp.float32
_BF16 = jnp.bfloat16


def _dot(a, b):
    return jnp.dot(a, b, preferred_element_type=_F32)


def _silu(g):
    half_g = 0.5 * g
    return half_g * (1.0 + jnp.tanh(half_g))


def _load_weight_as_bf16(w_hbm, w_ref, row_scale_ref=None):
    k, n = w_hbm.shape
    rows = min(k, pl.next_power_of_2(WEIGHT_STAGE_BYTES // (4 * n) + 1) // 2)
    assert k % rows == 0 and rows % 16 == 0
    n_chunks = k // rows
    if row_scale_ref is not None:
        scale_col = jnp.broadcast_to(row_scale_ref[...], (LANES, k)).T

    def body(stage, sem):
        def copy(i):
            return pltpu.make_async_copy(w_hbm.at[pl.ds(i * rows, rows), :], stage.at[i % 2], sem.at[i % 2])

        copy(0).start()
        for i in range(n_chunks):
            if i + 1 < n_chunks:
                copy(i + 1).start()
            copy(i).wait()
            chunk = stage[i % 2]
            if row_scale_ref is not None:
                chunk = chunk * scale_col[i * rows:(i + 1) * rows, 0:1]
            w_ref[i * rows:(i + 1) * rows, :] = chunk.astype(_BF16)

    pl.run_scoped(body, pltpu.VMEM((2, rows, n), _F32), pltpu.SemaphoreType.DMA((2,)))


def _layer_a_kernel(x_ref, ng_ref, win_hbm, lng_ref, lnb_ref, ws_ref, bst_ref, wout_hbm,
                    o_ref, vn_ref, y_ref, win_ref, wout_ref, wsm_ref, bias_ref):
    @pl.when(pl.program_id(0) == 0)
    def _():
        _load_weight_as_bf16(win_hbm, win_ref)
        _load_weight_as_bf16(wout_hbm, wout_ref)
        row = lax.broadcasted_iota(jnp.int32, (CHUNK, CHUNK), 0)
        col = lax.broadcasted_iota(jnp.int32, (CHUNK, CHUNK), 1)
        for g in range(A_GROUPS):
            wsm_ref[g] = jnp.where(col <= row, ws_ref[g], 0.0).astype(_BF16)
            bias_ref[g] = jnp.broadcast_to(bst_ref[:, g:g + 1], (CHUNK, CHUNK))

    tm = x_ref.shape[0]
    half = tm // A_ROW_SPLIT
    n_blocks = A_WIDTH // A_COL_BLOCK
    groups_per_block = A_COL_BLOCK // CHUNK

    def normed(r0):
        x = x_ref[r0:r0 + half, :]
        return (x * lax.rsqrt(jnp.mean(x * x, axis=-1, keepdims=True) + EPS) * ng_ref[...]).astype(_BF16)

    def layer_norm_v(r0, v):
        mu = jnp.mean(v, axis=-1, keepdims=True)
        vc = v - mu
        var = jnp.mean(vc * vc, axis=-1, keepdims=True)
        vn_ref[r0:r0 + half, :] = (vc * lax.rsqrt(var + EPS) * lng_ref[...] + lnb_ref[...]).astype(_BF16)

    def gated_block(r0, h1, j):
        c0 = j * A_COL_BLOCK
        u = _dot(h1, win_ref[:, c0:c0 + A_COL_BLOCK])
        gt = _dot(h1, win_ref[:, 2 * A_WIDTH + c0:2 * A_WIDTH + c0 + A_COL_BLOCK])
        for gi in range(groups_per_block):
            g = j * groups_per_block + gi
            for c in range(half // CHUNK):
                rows = slice(c * CHUNK, (c + 1) * CHUNK)
                tile_rows = slice(r0 + c * CHUNK, r0 + (c + 1) * CHUNK)
                cols = slice(gi * CHUNK, (gi + 1) * CHUNK)
                sv = _dot(wsm_ref[g], vn_ref[tile_rows, g * CHUNK:(g + 1) * CHUNK]) + bias_ref[g]
                y_ref[tile_rows, g * CHUNK:(g + 1) * CHUNK] = (
                    u[rows, cols] * sv * _silu(gt[rows, cols])).astype(_BF16)

    starts = [i * half for i in range(A_ROW_SPLIT)]
    h1 = [normed(r0) for r0 in starts]
    v = [_dot(h, win_ref[:, A_WIDTH:2 * A_WIDTH]) for h in h1]
    for r0, vi in zip(starts, v):
        layer_norm_v(r0, vi)
    for j in range(n_blocks):
        for r0, h in zip(starts, h1):
            gated_block(r0, h, j)
    for r0 in starts:
        o_ref[r0:r0 + half, :] = x_ref[r0:r0 + half, :] + _dot(y_ref[r0:r0 + half, :], wout_ref[...])


def _layer_a(x2, norm_g, w_in, ln_g, ln_b, ws, bs_t, w_out):
    t = x2.shape[0]
    tm = A_TOKENS_PER_STEP
    const = lambda i: (0, 0)
    return pl.pallas_call(
        _layer_a_kernel,
        out_shape=jax.ShapeDtypeStruct((t, D_MODEL), _F32),
        grid=(t // tm,),
        in_specs=[
            pl.BlockSpec((tm, D_MODEL), lambda i: (i, 0)),
            pl.BlockSpec((1, D_MODEL), const),
            pl.BlockSpec(memory_space=pl.ANY),
            pl.BlockSpec((1, A_WIDTH), const),
            pl.BlockSpec((1, A_WIDTH), const),
            pl.BlockSpec((A_GROUPS, CHUNK, CHUNK), lambda i: (0, 0, 0)),
            pl.BlockSpec((CHUNK, A_GROUPS), const),
            pl.BlockSpec(memory_space=pl.ANY),
        ],
        out_specs=pl.BlockSpec((tm, D_MODEL), lambda i: (i, 0)),
        scratch_shapes=[
            pltpu.VMEM((tm, A_WIDTH), _BF16),
            pltpu.VMEM((tm, A_WIDTH), _BF16),
            pltpu.VMEM((D_MODEL, 3 * A_WIDTH), _BF16),
            pltpu.VMEM((A_WIDTH, D_MODEL), _BF16),
            pltpu.VMEM((A_GROUPS, CHUNK, CHUNK), _BF16),
            pltpu.VMEM((A_GROUPS, CHUNK, CHUNK), _F32),
        ],
        compiler_params=pltpu.CompilerParams(
            dimension_semantics=("arbitrary",), vmem_limit_bytes=VMEM_LIMIT_BYTES),
        name="gmlp_layer",
    )(x2, norm_g, w_in, ln_g, ln_b, ws, bs_t, w_out)


def _layer_b_kernel(sinks_ref, hc_ref, hp_ref, kvg_ref, bg_ref, wkv_hbm, bkv_ref, win_hbm, bq_ref,
                    wout_hbm, fg_ref, cos_ref, sin_ref,
                    o_ref, kt_ref, vx_ref, q_ref, gate_ref, y_ref, wkv_ref, win_ref, wout_ref,
                    *, tiles_per_seq):
    t = pl.program_id(0)

    @pl.when(t == 0)
    def _():
        _load_weight_as_bf16(wkv_hbm, wkv_ref, kvg_ref)
        _load_weight_as_bf16(win_hbm, win_ref, bg_ref)
        _load_weight_as_bf16(wout_hbm, wout_ref)

    refs = (sinks_ref, hc_ref, hp_ref, kvg_ref, bg_ref, wkv_ref, bkv_ref, win_ref, bq_ref,
            wout_ref, fg_ref, cos_ref, sin_ref, o_ref, kt_ref, vx_ref, q_ref, gate_ref, y_ref)

    @pl.when(t % 2 == 0)
    def _():
        _layer_b_step(*refs, t=t, slot=0, tiles_per_seq=tiles_per_seq)

    @pl.when(t % 2 == 1)
    def _():
        _layer_b_step(*refs, t=t, slot=1, tiles_per_seq=tiles_per_seq)


def _layer_b_step(sinks_ref, hc_ref, hp_ref, kvg_ref, bg_ref, wkv_ref, bkv_ref, win_ref, bq_ref,
                  wout_ref, fg_ref, cos_ref, sin_ref,
                  o_ref, kt_ref, vx_ref, q_ref, gate_ref, y_ref, *, t, slot, tiles_per_seq):
    tm = hc_ref.shape[0]
    prev = 1 - slot
    new = slice(CHUNK, CHUNK + tm)
    planes = 2 * N_KV_HEADS

    if slot == 0:
        @pl.when(t == 0)
        def _():
            kt_ref[prev] = jnp.zeros((planes, LANES, CHUNK + tm), _BF16)
            vx_ref[prev] = jnp.zeros((planes, CHUNK + tm, 2 * LANES), _BF16)
            q_ref[prev] = jnp.zeros((tm, B_WIDTH), _BF16)
            gate_ref[prev] = jnp.zeros((tm, B_WIDTH), _F32)

    first_in_seq = (t % tiles_per_seq) == 0

    @pl.when(first_in_seq)
    def _():
        kt_ref[slot, :, :, 0:CHUNK] = jnp.zeros((planes, LANES, CHUNK), _BF16)
        vx_ref[slot, :, 0:CHUNK, :] = jnp.zeros((planes, CHUNK, 2 * LANES), _BF16)

    @pl.when(jnp.logical_not(first_in_seq))
    def _():
        kt_ref[slot, :, :, 0:CHUNK] = kt_ref[prev, :, :, tm:tm + CHUNK]
        vx_ref[slot, :, 0:CHUNK, :] = vx_ref[prev, :, tm:tm + CHUNK, :]

    h = hc_ref[...]
    h2 = (h * lax.rsqrt(jnp.mean(h * h, axis=-1, keepdims=True) + EPS)).astype(_BF16)
    kvn = h2

    cos = cos_ref[...]
    sin = sin_ref[...]
    lane = lax.broadcasted_iota(jnp.int32, (tm, LANES), 1)
    first_half = (lane % HEAD_DIM) < (HEAD_DIM // 2)
    low_head = lane < HEAD_DIM

    def rope(x, cos_t, sin_t):
        swapped = jnp.where(first_half,
                            pltpu.roll(x, LANES - HEAD_DIM // 2, 1),
                            pltpu.roll(x, HEAD_DIM // 2, 1))
        return x * cos_t + swapped * sin_t

    def project_kv():
        kv = _dot(kvn, wkv_ref[...]) + bkv_ref[...]
        k_t = rope(kv[:, :KV_WIDTH], cos, sin).T
        zero_rows = jnp.zeros((HEAD_DIM, tm), _F32)
        for kvh in range(N_KV_HEADS):
            head_t = k_t[kvh * HEAD_DIM:(kvh + 1) * HEAD_DIM]
            kt_ref[slot, 2 * kvh, :, new] = jnp.concatenate([head_t, zero_rows], axis=0).astype(_BF16)
            kt_ref[slot, 2 * kvh + 1, :, new] = jnp.concatenate([zero_rows, head_t], axis=0).astype(_BF16)
        v = kv[:, KV_WIDTH:]
        v_sw = pltpu.roll(v, HEAD_DIM, 1)
        zero = jnp.zeros_like(v)
        one = jnp.ones_like(v)
        ones_lo = jnp.where(low_head, one, zero)
        ones_hi = jnp.where(low_head, zero, one)
        v_low = (jnp.where(low_head, v, zero), jnp.where(low_head, v_sw, zero))
        v_high = (jnp.where(low_head, zero, v_sw), jnp.where(low_head, zero, v))
        for kvh in range(N_KV_HEADS):
            vx_ref[slot, 2 * kvh, new, :] = jnp.concatenate([v_low[kvh], ones_lo], axis=1).astype(_BF16)
            vx_ref[slot, 2 * kvh + 1, new, :] = jnp.concatenate([v_high[kvh], ones_hi], axis=1).astype(_BF16)

    q_scale = (HEAD_DIM ** -0.5) * LOG2_E
    cos_q = cos * q_scale
    sin_q = sin * q_scale

    def project_q(b):
        cols = slice(b * P_BLOCK, (b + 1) * P_BLOCK)
        zq = _dot(h2, win_ref[:, cols]) + bq_ref[:, cols]
        for i in range(P_BLOCK // LANES):
            s = b * (P_BLOCK // LANES) + i
            q_ref[slot, :, s * LANES:(s + 1) * LANES] = rope(
                zq[:, i * LANES:(i + 1) * LANES], cos_q, sin_q).astype(_BF16)

    def project_gate(b):
        cols = slice(b * P_BLOCK, (b + 1) * P_BLOCK)
        gate_ref[slot, :, cols] = _dot(h2, win_ref[:, B_WIDTH + b * P_BLOCK:B_WIDTH + (b + 1) * P_BLOCK])

    qi = lax.broadcasted_iota(jnp.int32, (CHUNK, 2 * CHUNK), 0)
    kj = lax.broadcasted_iota(jnp.int32, (CHUNK, 2 * CHUNK), 1)
    in_window = (kj > qi) & (kj <= qi + CHUNK)
    neg_inf = jnp.float32(-jnp.inf)
    bias = jnp.where(in_window, 0.0, neg_inf)
    prev_first_in_seq = ((t + tiles_per_seq - 1) % tiles_per_seq) == 0
    bias_first = jnp.where(in_window & ((kj >= CHUNK) | jnp.logical_not(prev_first_in_seq)), 0.0, neg_inf)
    low_head_c = lax.broadcasted_iota(jnp.int32, (CHUNK, LANES), 1) < HEAD_DIM

    units = [(c, kvh) for kvh in range(N_KV_HEADS) for c in range(tm // CHUNK)]
    assert len(units) == 4 and B_WIDTH // P_BLOCK == 4, "the program order below is written for 4 + 4"

    def scores(unit):
        c, kvh = unit
        rows = slice(c * CHUNK, (c + 1) * CHUNK)
        band = slice(c * CHUNK, (c + 2) * CHUNK)
        q_stack = jnp.concatenate(
            [q_ref[prev, rows, s * LANES:(s + 1) * LANES]
             for s in range(kvh * SLABS_PER_KV, (kvh + 1) * SLABS_PER_KV)], axis=0)
        return [_dot(q_stack, kt_ref[prev, 2 * kvh + half, :, band]) for half in range(HEADS_PER_SLAB)]

    def softmax(unit, sc_halves):
        c, kvh = unit
        chunk_bias = bias_first if c == 0 else bias
        probs = []
        maxima = []
        for half, sc in enumerate(sc_halves):
            p_parts = []
            m_parts = []
            for si in range(SLABS_PER_KV):
                s = kvh * SLABS_PER_KV + si
                sink = sinks_ref[s * HEADS_PER_SLAB + half] * LOG2_E
                sb = sc[si * CHUNK:(si + 1) * CHUNK] + chunk_bias
                m = jnp.maximum(jnp.max(sb, axis=-1, keepdims=True), sink)
                p_parts.append(jnp.exp2(sb - m).astype(_BF16))
                m_parts.append(m)
            probs.append(jnp.concatenate(p_parts, axis=0))
            maxima.append(m_parts)
        return probs, maxima

    def attend(unit, probs, maxima):
        c, kvh = unit
        rows = slice(c * CHUNK, (c + 1) * CHUNK)
        band = slice(c * CHUNK, (c + 2) * CHUNK)
        od = (_dot(probs[0], vx_ref[prev, 2 * kvh, band, :])
              + _dot(probs[1], vx_ref[prev, 2 * kvh + 1, band, :]))
        for si in range(SLABS_PER_KV):
            s = kvh * SLABS_PER_KV + si
            blk = slice(si * CHUNK, (si + 1) * CHUNK)
            sink = jnp.where(low_head_c, sinks_ref[s * HEADS_PER_SLAB] * LOG2_E,
                             sinks_ref[s * HEADS_PER_SLAB + 1] * LOG2_E)
            m = jnp.where(low_head_c, maxima[0][si], maxima[1][si])
            denom = od[blk, LANES:] + jnp.exp2(sink - m)
            gate = gate_ref[prev, rows, s * LANES:(s + 1) * LANES]
            y_ref[rows, s * LANES:(s + 1) * LANES] = (od[blk, :LANES] * _silu(gate) / denom).astype(_BF16)

    def out_part(kvh):
        cols = slice(kvh * SLABS_PER_KV * LANES, (kvh + 1) * SLABS_PER_KV * LANES)
        return _dot(y_ref[:, cols], wout_ref[cols, :])

    s0 = scores(units[0])
    project_kv()
    s1 = scores(units[1])
    project_q(0)
    p0 = softmax(units[0], s0)
    attend(units[0], *p0)
    s2 = scores(units[2])
    project_q(1)
    p1 = softmax(units[1], s1)
    attend(units[1], *p1)
    s3 = scores(units[3])
    project_q(2)
    p2 = softmax(units[2], s2)
    attend(units[2], *p2)
    project_q(3)
    project_gate(0)
    acc = out_part(0)
    p3 = softmax(units[3], s3)
    attend(units[3], *p3)
    project_gate(1)
    out = hp_ref[...] + acc + out_part(1)
    o_ref[...] = out * lax.rsqrt(jnp.mean(out * out, axis=-1, keepdims=True) + EPS) * fg_ref[...]
    project_gate(2)
    project_gate(3)


def _layer_b(h2d, seq, sinks, kv_g, b_g, w_kv, b_kv, w_in, b_q, w_out, f_g, cos, sin):
    tm = B_TOKENS_PER_STEP
    n_tiles = h2d.shape[0] // tm
    tiles_per_seq = seq // tm
    const = lambda t: (0, 0)
    cur = lambda t: (jnp.minimum(t, n_tiles - 1), 0)
    prv = lambda t: (jnp.maximum(t - 1, 0), 0)
    tab = lambda t: (jnp.minimum(t, n_tiles - 1) % tiles_per_seq, 0)
    planes = 2 * N_KV_HEADS
    return pl.pallas_call(
        functools.partial(_layer_b_kernel, tiles_per_seq=tiles_per_seq),
        out_shape=jax.ShapeDtypeStruct(h2d.shape, _F32),
        grid=(n_tiles + 1,),
        in_specs=[
            pl.BlockSpec(memory_space=pltpu.SMEM),
            pl.BlockSpec((tm, D_MODEL), cur),
            pl.BlockSpec((tm, D_MODEL), prv),
            pl.BlockSpec((1, D_MODEL), const),
            pl.BlockSpec((1, D_MODEL), const),
            pl.BlockSpec(memory_space=pl.ANY),
            pl.BlockSpec((1, 2 * KV_WIDTH), const),
            pl.BlockSpec(memory_space=pl.ANY),
            pl.BlockSpec((1, B_WIDTH), const),
            pl.BlockSpec(memory_space=pl.ANY),
            pl.BlockSpec((1, D_MODEL), const),
            pl.BlockSpec((tm, LANES), tab),
            pl.BlockSpec((tm, LANES), tab),
        ],
        out_specs=pl.BlockSpec((tm, D_MODEL), prv),
        scratch_shapes=[
            pltpu.VMEM((2, planes, LANES, CHUNK + tm), _BF16),
            pltpu.VMEM((2, planes, CHUNK + tm, 2 * LANES), _BF16),
            pltpu.VMEM((2, tm, B_WIDTH), _BF16),
            pltpu.VMEM((2, tm, B_WIDTH), _F32),
            pltpu.VMEM((tm, B_WIDTH), _BF16),
            pltpu.VMEM((D_MODEL, 2 * KV_WIDTH), _BF16),
            pltpu.VMEM((D_MODEL, 2 * B_WIDTH), _BF16),
            pltpu.VMEM((B_WIDTH, D_MODEL), _BF16),
        ],
        compiler_params=pltpu.CompilerParams(
            dimension_semantics=("arbitrary",), vmem_limit_bytes=VMEM_LIMIT_BYTES),
        name="swa_layer",
    )(sinks, h2d, h2d, kv_g, b_g, w_kv, b_kv, w_in, b_q, w_out, f_g, cos, sin)


def _rope_tables(seq):
    inv_freq = ROPE_THETA ** (-np.arange(0, HEAD_DIM, 2, dtype=np.float64) / HEAD_DIM)
    ang = np.arange(seq, dtype=np.float64)[:, None] * inv_freq[None, :]
    cos = np.cos(ang)
    sin = np.sin(ang)
    cos_slab = np.tile(np.concatenate([cos, cos], axis=-1), (1, HEADS_PER_SLAB))
    sin_slab = np.tile(np.concatenate([-sin, sin], axis=-1), (1, HEADS_PER_SLAB))
    assert cos_slab.shape == (seq, LANES)
    return jnp.asarray(cos_slab, dtype=_F32), jnp.asarray(sin_slab, dtype=_F32)


def kernel(x, a_norm_g, a_w_in, a_ln_g, a_ln_b, a_ws, a_bs, a_w_out, kv_norm_g, w_kv, b_kv,
           b_norm_g, b_w_in, b_bq, b_sinks, b_w_out, final_norm_g):
    batch, seq, d = x.shape
    assert d == D_MODEL
    for tile in (A_TOKENS_PER_STEP, B_TOKENS_PER_STEP):
        assert seq % tile == 0 and tile % CHUNK == 0
    assert a_w_in.shape[0] == 1 and b_w_in.shape[0] == 1
    row = lambda p: p.reshape(1, -1)

    x2 = x.reshape(batch * seq, d)
    h = _layer_a(x2, row(a_norm_g[0]), a_w_in[0], row(a_ln_g[0]), row(a_ln_b[0]),
                 a_ws[0], a_bs[0].T, a_w_out[0])
    cos, sin = _rope_tables(seq)
    out = _layer_b(h, seq, b_sinks[0], row(kv_norm_g), row(b_norm_g[0]),
                   w_kv, row(b_kv), b_w_in[0], row(b_bq[0]),
                   b_w_out[0], row(final_norm_g), cos, sin)
    return out.reshape(batch, seq, d)
```

```python
import functools

import jax
import jax.numpy as jnp
import numpy as np
from jax import lax
from jax.experimental import pallas as pl
from jax.experimental.pallas import tpu as pltpu

D_MODEL = 1024
CHUNK = 128
A_WIDTH = 2048
A_GROUPS = 16
HEAD_DIM = 64
N_Q_HEADS = 16
N_KV_HEADS = 2
B_WIDTH = 1024
KV_WIDTH = 128
ROPE_THETA = 10000.0
EPS = 1e-5
LOG2_E = 1.4426950408889634

LANES = 128
BF16_ROWS = 16
HEADS_PER_SLAB = LANES // HEAD_DIM
N_SLABS = B_WIDTH // LANES
SLABS_PER_KV = N_SLABS // N_KV_HEADS

A_TOKENS_PER_STEP = 512
B_TOKENS_PER_STEP = 256
P_BLOCK = 256
A_COL_BLOCK = 512
A_ROW_SPLIT = 2
VMEM_LIMIT_BYTES = 56 * 1024 * 1024
WEIGHT_STAGE_BYTES = 2 * 1024 * 1024

_F32 = jnp.float32
_BF16 = jnp.bfloat16


def _dot(a, b):
    return jnp.dot(a, b, preferred_element_type=_F32)


def _silu(g):
    half_g = 0.5 * g
    return half_g * (1.0 + jnp.tanh(half_g))


def _load_weight_as_bf16(w_hbm, w_ref, row_scale_ref=None):
    k, n = w_hbm.shape
    rows = min(k, pl.next_power_of_2(WEIGHT_STAGE_BYTES // (4 * n) + 1) // 2)
    assert k % rows == 0 and rows % 16 == 0
    n_chunks = k // rows
    if row_scale_ref is not None:
        scale_col = jnp.broadcast_to(row_scale_ref[...], (LANES, k)).T

    def body(stage, sem):
        def copy(i):
            return pltpu.make_async_copy(w_hbm.at[pl.ds(i * rows, rows), :], stage.at[i % 2], sem.at[i % 2])

        copy(0).start()
        for i in range(n_chunks):
            if i + 1 < n_chunks:
                copy(i + 1).start()
            copy(i).wait()
            chunk = stage[i % 2]
            if row_scale_ref is not None:
                chunk = chunk * scale_col[i * rows:(i + 1) * rows, 0:1]
            w_ref[i * rows:(i + 1) * rows, :] = chunk.astype(_BF16)

    pl.run_scoped(body, pltpu.VMEM((2, rows, n), _F32), pltpu.SemaphoreType.DMA((2,)))


def _layer_a_kernel(x_ref, ng_ref, win_hbm, lng_ref, lnb_ref, ws_ref, bst_ref, wout_hbm,
                    o_ref, vn_ref, y_ref, win_ref, wout_ref, wsm_ref, bias_ref):
    @pl.when(pl.program_id(0) == 0)
    def _():
        _load_weight_as_bf16(win_hbm, win_ref)
        _load_weight_as_bf16(wout_hbm, wout_ref)
        row = lax.broadcasted_iota(jnp.int32, (CHUNK, CHUNK), 0)
        col = lax.broadcasted_iota(jnp.int32, (CHUNK, CHUNK), 1)
        for g in range(A_GROUPS):
            wsm_ref[g] = jnp.where(col <= row, ws_ref[g], 0.0).astype(_BF16)
            bias_ref[g] = jnp.broadcast_to(bst_ref[:, g:g + 1], (CHUNK, CHUNK))

    tm = x_ref.shape[0]
    half = tm // A_ROW_SPLIT
    n_blocks = A_WIDTH // A_COL_BLOCK
    groups_per_block = A_COL_BLOCK // CHUNK

    def normed(r0):
        x = x_ref[r0:r0 + half, :]
        return (x * lax.rsqrt(jnp.mean(x * x, axis=-1, keepdims=True) + EPS) * ng_ref[...]).astype(_BF16)

    def layer_norm_v(r0, v):
        mu = jnp.mean(v, axis=-1, keepdims=True)
        vc = v - mu
        var = jnp.mean(vc * vc, axis=-1, keepdims=True)
        vn_ref[r0:r0 + half, :] = (vc * lax.rsqrt(var + EPS) * lng_ref[...] + lnb_ref[...]).astype(_BF16)

    def gated_block(r0, h1, j):
        c0 = j * A_COL_BLOCK
        u = _dot(h1, win_ref[:, c0:c0 + A_COL_BLOCK])
        gt = _dot(h1, win_ref[:, 2 * A_WIDTH + c0:2 * A_WIDTH + c0 + A_COL_BLOCK])
        for gi in range(groups_per_block):
            g = j * groups_per_block + gi
            for c in range(half // CHUNK):
                rows = slice(c * CHUNK, (c + 1) * CHUNK)
                tile_rows = slice(r0 + c * CHUNK, r0 + (c + 1) * CHUNK)
                cols = slice(gi * CHUNK, (gi + 1) * CHUNK)
                sv = _dot(wsm_ref[g], vn_ref[tile_rows, g * CHUNK:(g + 1) * CHUNK]) + bias_ref[g]
                y_ref[tile_rows, g * CHUNK:(g + 1) * CHUNK] = (
                    u[rows, cols] * sv * _silu(gt[rows, cols])).astype(_BF16)

    starts = [i * half for i in range(A_ROW_SPLIT)]
    h1 = [normed(r0) for r0 in starts]
    v = [_dot(h, win_ref[:, A_WIDTH:2 * A_WIDTH]) for h in h1]
    for r0, vi in zip(starts, v):
        layer_norm_v(r0, vi)
    for j in range(n_blocks):
        for r0, h in zip(starts, h1):
            gated_block(r0, h, j)
    for r0 in starts:
        o_ref[r0:r0 + half, :] = x_ref[r0:r0 + half, :] + _dot(y_ref[r0:r0 + half, :], wout_ref[...])


def _layer_a(x2, norm_g, w_in, ln_g, ln_b, ws, bs_t, w_out):
    t = x2.shape[0]
    tm = A_TOKENS_PER_STEP
    const = lambda i: (0, 0)
    return pl.pallas_call(
        _layer_a_kernel,
        out_shape=jax.ShapeDtypeStruct((t, D_MODEL), _F32),
        grid=(t // tm,),
        in_specs=[
            pl.BlockSpec((tm, D_MODEL), lambda i: (i, 0)),
            pl.BlockSpec((1, D_MODEL), const),
            pl.BlockSpec(memory_space=pl.ANY),
            pl.BlockSpec((1, A_WIDTH), const),
            pl.BlockSpec((1, A_WIDTH), const),
            pl.BlockSpec((A_GROUPS, CHUNK, CHUNK), lambda i: (0, 0, 0)),
            pl.BlockSpec((CHUNK, A_GROUPS), const),
            pl.BlockSpec(memory_space=pl.ANY),
        ],
        out_specs=pl.BlockSpec((tm, D_MODEL), lambda i: (i, 0)),
        scratch_shapes=[
            pltpu.VMEM((tm, A_WIDTH), _BF16),
            pltpu.VMEM((tm, A_WIDTH), _BF16),
            pltpu.VMEM((D_MODEL, 3 * A_WIDTH), _BF16),
            pltpu.VMEM((A_WIDTH, D_MODEL), _BF16),
            pltpu.VMEM((A_GROUPS, CHUNK, CHUNK), _BF16),
            pltpu.VMEM((A_GROUPS, CHUNK, CHUNK), _F32),
        ],
        compiler_params=pltpu.CompilerParams(
            dimension_semantics=("arbitrary",), vmem_limit_bytes=VMEM_LIMIT_BYTES),
        name="gmlp_layer",
    )(x2, norm_g, w_in, ln_g, ln_b, ws, bs_t, w_out)


def _layer_b_kernel(sinks_ref, hc_ref, hp_ref, kvg_ref, bg_ref, wkv_hbm, bkv_ref, win_hbm, bq_ref,
                    wout_hbm, fg_ref, cos_ref, sin_ref,
                    o_ref, kt_ref, vx_ref, q_ref, gate_ref, y_ref, wkv_ref, win_ref, wout_ref,
                    *, tiles_per_seq):
    t = pl.program_id(0)

    @pl.when(t == 0)
    def _():
        _load_weight_as_bf16(wkv_hbm, wkv_ref, kvg_ref)
        _load_weight_as_bf16(win_hbm, win_ref, bg_ref)
        _load_weight_as_bf16(wout_hbm, wout_ref)

    refs = (sinks_ref, hc_ref, hp_ref, kvg_ref, bg_ref, wkv_ref, bkv_ref, win_ref, bq_ref,
            wout_ref, fg_ref, cos_ref, sin_ref, o_ref, kt_ref, vx_ref, q_ref, gate_ref, y_ref)

    @pl.when(t % 2 == 0)
    def _():
        _layer_b_step(*refs, t=t, slot=0, tiles_per_seq=tiles_per_seq)

    @pl.when(t % 2 == 1)
    def _():
        _layer_b_step(*refs, t=t, slot=1, tiles_per_seq=tiles_per_seq)


def _layer_b_step(sinks_ref, hc_ref, hp_ref, kvg_ref, bg_ref, wkv_ref, bkv_ref, win_ref, bq_ref,
                  wout_ref, fg_ref, cos_ref, sin_ref,
                  o_ref, kt_ref, vx_ref, q_ref, gate_ref, y_ref, *, t, slot, tiles_per_seq):
    tm = hc_ref.shape[0]
    prev = 1 - slot
    new = slice(CHUNK, CHUNK + tm)
    planes = 2 * N_KV_HEADS

    if slot == 0:
        @pl.when(t == 0)
        def _():
            kt_ref[prev] = jnp.zeros((planes, LANES, CHUNK + tm), _BF16)
            vx_ref[prev] = jnp.zeros((planes, CHUNK + tm, 2 * LANES), _BF16)
            q_ref[prev] = jnp.zeros((tm, B_WIDTH), _BF16)
            gate_ref[prev] = jnp.zeros((tm, B_WIDTH), _F32)

    first_in_seq = (t % tiles_per_seq) == 0

    @pl.when(first_in_seq)
    def _():
        kt_ref[slot, :, :, 0:CHUNK] = jnp.zeros((planes, LANES, CHUNK), _BF16)
        vx_ref[slot, :, 0:CHUNK, :] = jnp.zeros((planes, CHUNK, 2 * LANES), _BF16)

    @pl.when(jnp.logical_not(first_in_seq))
    def _():
        kt_ref[slot, :, :, 0:CHUNK] = kt_ref[prev, :, :, tm:tm + CHUNK]
        vx_ref[slot, :, 0:CHUNK, :] = vx_ref[prev, :, tm:tm + CHUNK, :]

    h = hc_ref[...]
    h2 = (h * lax.rsqrt(jnp.mean(h * h, axis=-1, keepdims=True) + EPS)).astype(_BF16)
    kvn = h2

    cos = cos_ref[...]
    sin = sin_ref[...]
    lane = lax.broadcasted_iota(jnp.int32, (tm, LANES), 1)
    first_half = (lane % HEAD_DIM) < (HEAD_DIM // 2)
    low_head = lane < HEAD_DIM

    def rope(x, cos_t, sin_t):
        swapped = jnp.where(first_half,
                            pltpu.roll(x, LANES - HEAD_DIM // 2, 1),
                            pltpu.roll(x, HEAD_DIM // 2, 1))
        return x * cos_t + swapped * sin_t

    def project_kv():
        kv = _dot(kvn, wkv_ref[...]) + bkv_ref[...]
        k_t = rope(kv[:, :KV_WIDTH], cos, sin).T
        zero_rows = jnp.zeros((HEAD_DIM, tm), _F32)
        for kvh in range(N_KV_HEADS):
            head_t = k_t[kvh * HEAD_DIM:(kvh + 1) * HEAD_DIM]
            kt_ref[slot, 2 * kvh, :, new] = jnp.concatenate([head_t, zero_rows], axis=0).astype(_BF16)
            kt_ref[slot, 2 * kvh + 1, :, new] = jnp.concatenate([zero_rows, head_t], axis=0).astype(_BF16)
        v = kv[:, KV_WIDTH:]
        v_sw = pltpu.roll(v, HEAD_DIM, 1)
        zero = jnp.zeros_like(v)
        one = jnp.ones_like(v)
        ones_lo = jnp.where(low_head, one, zero)
        ones_hi = jnp.where(low_head, zero, one)
        v_low = (jnp.where(low_head, v, zero), jnp.where(low_head, v_sw, zero))
        v_high = (jnp.where(low_head, zero, v_sw), jnp.where(low_head, zero, v))
        for kvh in range(N_KV_HEADS):
            vx_ref[slot, 2 * kvh, new, :] = jnp.concatenate([v_low[kvh], ones_lo], axis=1).astype(_BF16)
            vx_ref[slot, 2 * kvh + 1, new, :] = jnp.concatenate([v_high[kvh], ones_hi], axis=1).astype(_BF16)

    q_scale = (HEAD_DIM ** -0.5) * LOG2_E
    cos_q = cos * q_scale
    sin_q = sin * q_scale

    def project_q(b):
        cols = slice(b * P_BLOCK, (b + 1) * P_BLOCK)
        zq = _dot(h2, win_ref[:, cols]) + bq_ref[:, cols]
        for i in range(P_BLOCK // LANES):
            s = b * (P_BLOCK // LANES) + i
            q_ref[slot, :, s * LANES:(s + 1) * LANES] = rope(
                zq[:, i * LANES:(i + 1) * LANES], cos_q, sin_q).astype(_BF16)

    def project_gate(b):
        cols = slice(b * P_BLOCK, (b + 1) * P_BLOCK)
        gate_ref[slot, :, cols] = _dot(h2, win_ref[:, B_WIDTH + b * P_BLOCK:B_WIDTH + (b + 1) * P_BLOCK])

    qi = lax.broadcasted_iota(jnp.int32, (CHUNK, 2 * CHUNK), 0)
    kj = lax.broadcasted_iota(jnp.int32, (CHUNK, 2 * CHUNK), 1)
    in_window = (kj > qi) & (kj <= qi + CHUNK)
    neg_inf = jnp.float32(-jnp.inf)
    bias = jnp.where(in_window, 0.0, neg_inf)
    prev_first_in_seq = ((t + tiles_per_seq - 1) % tiles_per_seq) == 0
    bias_first = jnp.where(in_window & ((kj >= CHUNK) | jnp.logical_not(prev_first_in_seq)), 0.0, neg_inf)

    units = [(c, kvh) for kvh in range(N_KV_HEADS) for c in range(tm // CHUNK)]
    assert len(units) == 4 and B_WIDTH // P_BLOCK == 4, "the program order below is written for 4 + 4"

    def scores(unit):
        c, kvh = unit
        rows = slice(c * CHUNK, (c + 1) * CHUNK)
        band = slice(c * CHUNK, (c + 2) * CHUNK)
        q_stack = jnp.concatenate(
            [q_ref[prev, rows, s * LANES:(s + 1) * LANES]
             for s in range(kvh * SLABS_PER_KV, (kvh + 1) * SLABS_PER_KV)], axis=0)
        return [_dot(q_stack, kt_ref[prev, 2 * kvh + half, :, band]) for half in range(HEADS_PER_SLAB)]

    sink_slot = lax.broadcasted_iota(jnp.int32, (CHUNK, LANES), 1) == 0
    key_row = lax.broadcasted_iota(jnp.int32, (BF16_ROWS, 2 * LANES), 0)
    key_lane = lax.broadcasted_iota(jnp.int32, (BF16_ROWS, 2 * LANES), 1)

    def softmax(unit, sc_halves):
        c, kvh = unit
        chunk_bias = bias_first if c == 0 else bias
        probs = []
        for half, sc in enumerate(sc_halves):
            p_parts = []
            for si in range(SLABS_PER_KV):
                s = kvh * SLABS_PER_KV + si
                sink = sinks_ref[s * HEADS_PER_SLAB + half] * LOG2_E
                blk = sc[si * CHUNK:(si + 1) * CHUNK]
                sb = jnp.concatenate(
                    [jnp.where(sink_slot, sink, blk[:, :LANES] + chunk_bias[:, :LANES]),
                     blk[:, LANES:] + chunk_bias[:, LANES:]], axis=1)
                m = jnp.max(sb, axis=-1, keepdims=True)
                p_parts.append(jnp.exp2(sb - m).astype(_BF16))
            probs.append(jnp.concatenate(p_parts, axis=0))
        return probs

    def band_values(kvh, half, c):
        r0 = c * CHUNK
        ones_lo = LANES + half * HEAD_DIM
        sink_row = jnp.where((key_lane >= ones_lo) & (key_lane < ones_lo + HEAD_DIM), 1.0, 0.0)
        head = vx_ref[prev, 2 * kvh + half, r0:r0 + BF16_ROWS, :].astype(_F32)
        head = jnp.where(key_row == 0, sink_row, head).astype(_BF16)
        return jnp.concatenate(
            [head, vx_ref[prev, 2 * kvh + half, r0 + BF16_ROWS:r0 + 2 * CHUNK, :]], axis=0)

    def attend(unit, probs):
        c, kvh = unit
        rows = slice(c * CHUNK, (c + 1) * CHUNK)
        od = _dot(probs[0], band_values(kvh, 0, c)) + _dot(probs[1], band_values(kvh, 1, c))
        for si in range(SLABS_PER_KV):
            s = kvh * SLABS_PER_KV + si
            blk = slice(si * CHUNK, (si + 1) * CHUNK)
            gate = gate_ref[prev, rows, s * LANES:(s + 1) * LANES]
            y_ref[rows, s * LANES:(s + 1) * LANES] = (
                od[blk, :LANES] * _silu(gate) / od[blk, LANES:]).astype(_BF16)

    def out_part(kvh):
        cols = slice(kvh * SLABS_PER_KV * LANES, (kvh + 1) * SLABS_PER_KV * LANES)
        return _dot(y_ref[:, cols], wout_ref[cols, :])

    s0 = scores(units[0])
    project_kv()
    s1 = scores(units[1])
    project_q(0)
    p0 = softmax(units[0], s0)
    attend(units[0], p0)
    s2 = scores(units[2])
    project_q(1)
    p1 = softmax(units[1], s1)
    attend(units[1], p1)
    s3 = scores(units[3])
    project_q(2)
    p2 = softmax(units[2], s2)
    attend(units[2], p2)
    project_q(3)
    project_gate(0)
    acc = out_part(0)
    p3 = softmax(units[3], s3)
    attend(units[3], p3)
    project_gate(1)
    out = hp_ref[...] + acc + out_part(1)
    o_ref[...] = out * lax.rsqrt(jnp.mean(out * out, axis=-1, keepdims=True) + EPS) * fg_ref[...]
    project_gate(2)
    project_gate(3)


def _layer_b(h2d, seq, sinks, kv_g, b_g, w_kv, b_kv, w_in, b_q, w_out, f_g, cos, sin):
    tm = B_TOKENS_PER_STEP
    n_tiles = h2d.shape[0] // tm
    tiles_per_seq = seq // tm
    const = lambda t: (0, 0)
    cur = lambda t: (jnp.minimum(t, n_tiles - 1), 0)
    prv = lambda t: (jnp.maximum(t - 1, 0), 0)
    tab = lambda t: (jnp.minimum(t, n_tiles - 1) % tiles_per_seq, 0)
    planes = 2 * N_KV_HEADS
    return pl.pallas_call(
        functools.partial(_layer_b_kernel, tiles_per_seq=tiles_per_seq),
        out_shape=jax.ShapeDtypeStruct(h2d.shape, _F32),
        grid=(n_tiles + 1,),
        in_specs=[
            pl.BlockSpec(memory_space=pltpu.SMEM),
            pl.BlockSpec((tm, D_MODEL), cur),
            pl.BlockSpec((tm, D_MODEL), prv),
            pl.BlockSpec((1, D_MODEL), const),
            pl.BlockSpec((1, D_MODEL), const),
            pl.BlockSpec(memory_space=pl.ANY),
            pl.BlockSpec((1, 2 * KV_WIDTH), const),
            pl.BlockSpec(memory_space=pl.ANY),
            pl.BlockSpec((1, B_WIDTH), const),
            pl.BlockSpec(memory_space=pl.ANY),
            pl.BlockSpec((1, D_MODEL), const),
            pl.BlockSpec((tm, LANES), tab),
            pl.BlockSpec((tm, LANES), tab),
        ],
        out_specs=pl.BlockSpec((tm, D_MODEL), prv),
        scratch_shapes=[
            pltpu.VMEM((2, planes, LANES, CHUNK + tm), _BF16),
            pltpu.VMEM((2, planes, CHUNK + tm, 2 * LANES), _BF16),
            pltpu.VMEM((2, tm, B_WIDTH), _BF16),
            pltpu.VMEM((2, tm, B_WIDTH), _F32),
            pltpu.VMEM((tm, B_WIDTH), _BF16),
            pltpu.VMEM((D_MODEL, 2 * KV_WIDTH), _BF16),
            pltpu.VMEM((D_MODEL, 2 * B_WIDTH), _BF16),
            pltpu.VMEM((B_WIDTH, D_MODEL), _BF16),
        ],
        compiler_params=pltpu.CompilerParams(
            dimension_semantics=("arbitrary",), vmem_limit_bytes=VMEM_LIMIT_BYTES),
        name="swa_layer",
    )(sinks, h2d, h2d, kv_g, b_g, w_kv, b_kv, w_in, b_q, w_out, f_g, cos, sin)


def _rope_tables(seq):
    inv_freq = ROPE_THETA ** (-np.arange(0, HEAD_DIM, 2, dtype=np.float64) / HEAD_DIM)
    ang = np.arange(seq, dtype=np.float64)[:, None] * inv_freq[None, :]
    cos = np.cos(ang)
    sin = np.sin(ang)
    cos_slab = np.tile(np.concatenate([cos, cos], axis=-1), (1, HEADS_PER_SLAB))
    sin_slab = np.tile(np.concatenate([-sin, sin], axis=-1), (1, HEADS_PER_SLAB))
    assert cos_slab.shape == (seq, LANES)
    return jnp.asarray(cos_slab, dtype=_F32), jnp.asarray(sin_slab, dtype=_F32)


def kernel(x, a_norm_g, a_w_in, a_ln_g, a_ln_b, a_ws, a_bs, a_w_out, kv_norm_g, w_kv, b_kv,
           b_norm_g, b_w_in, b_bq, b_sinks, b_w_out, final_norm_g):
    batch, seq, d = x.shape
    assert d == D_MODEL
    for tile in (A_TOKENS_PER_STEP, B_TOKENS_PER_STEP):
        assert seq % tile == 0 and tile % CHUNK == 0
    assert a_w_in.shape[0] == 1 and b_w_in.shape[0] == 1
    row = lambda p: p.reshape(1, -1)

    x2 = x.reshape(batch * seq, d)
    h = _layer_a(x2, row(a_norm_g[0]), a_w_in[0], row(a_ln_g[0]), row(a_ln_b[0]),
                 a_ws[0], a_bs[0].T, a_w_out[0])
    cos, sin = _rope_tables(seq)
    out = _layer_b(h, seq, b_sinks[0], row(kv_norm_g), row(b_norm_g[0]),
                   w_kv, row(b_kv), b_w_in[0], row(b_bq[0]),
                   b_w_out[0], row(final_norm_g), cos, sin)
    return out.reshape(batch, seq, d)
```

```python
import functools

import jax
import jax.numpy as jnp
import numpy as np
from jax import lax
from jax.experimental import pallas as pl
from jax.experimental.pallas import tpu as pltpu

D_MODEL = 1024
CHUNK = 128
A_WIDTH = 2048
A_GROUPS = 16
HEAD_DIM = 64
N_Q_HEADS = 16
N_KV_HEADS = 2
B_WIDTH = 1024
KV_WIDTH = 128
ROPE_THETA = 10000.0
EPS = 1e-5
LOG2_E = 1.4426950408889634

LANES = 128
BF16_ROWS = 16
HEADS_PER_SLAB = LANES // HEAD_DIM
N_SLABS = B_WIDTH // LANES
SLABS_PER_KV = N_SLABS // N_KV_HEADS

A_TOKENS_PER_STEP = 512
B_TOKENS_PER_STEP = 256
P_BLOCK = 256
A_COL_BLOCK = 512
A_ROW_SPLIT = 2
VMEM_LIMIT_BYTES = 56 * 1024 * 1024
WEIGHT_STAGE_BYTES = 2 * 1024 * 1024

_F32 = jnp.float32
_BF16 = jnp.bfloat16


def _dot(a, b):
    return jnp.dot(a, b, preferred_element_type=_F32)


def _silu(g):
    half_g = 0.5 * g
    return half_g * (1.0 + jnp.tanh(half_g))


def _load_weight_as_bf16(w_hbm, w_ref, row_scale_ref=None):
    k, n = w_hbm.shape
    rows = min(k, pl.next_power_of_2(WEIGHT_STAGE_BYTES // (4 * n) + 1) // 2)
    assert k % rows == 0 and rows % 16 == 0
    n_chunks = k // rows
    if row_scale_ref is not None:
        scale_col = jnp.broadcast_to(row_scale_ref[...], (LANES, k)).T

    def body(stage, sem):
        def copy(i):
            return pltpu.make_async_copy(w_hbm.at[pl.ds(i * rows, rows), :], stage.at[i % 2], sem.at[i % 2])

        copy(0).start()
        for i in range(n_chunks):
            if i + 1 < n_chunks:
                copy(i + 1).start()
            copy(i).wait()
            chunk = stage[i % 2]
            if row_scale_ref is not None:
                chunk = chunk * scale_col[i * rows:(i + 1) * rows, 0:1]
            w_ref[i * rows:(i + 1) * rows, :] = chunk.astype(_BF16)

    pl.run_scoped(body, pltpu.VMEM((2, rows, n), _F32), pltpu.SemaphoreType.DMA((2,)))


def _layer_a_kernel(x_ref, ng_ref, win_hbm, lng_ref, lnb_ref, ws_ref, bst_ref, wout_hbm,
                    o_ref, y_ref, win_ref, wout_ref, wsm_ref, bias_ref):
    @pl.when(pl.program_id(0) == 0)
    def _():
        _load_weight_as_bf16(win_hbm, win_ref)
        _load_weight_as_bf16(wout_hbm, wout_ref)
        row = lax.broadcasted_iota(jnp.int32, (CHUNK, CHUNK), 0)
        col = lax.broadcasted_iota(jnp.int32, (CHUNK, CHUNK), 1)
        for g in range(A_GROUPS):
            wsm_ref[g] = jnp.where(col <= row, ws_ref[g], 0.0).astype(_BF16)
            bias_ref[g] = jnp.broadcast_to(bst_ref[:, g:g + 1], (CHUNK, CHUNK))

    tm = x_ref.shape[0]
    half = tm // A_ROW_SPLIT
    n_blocks = A_WIDTH // A_COL_BLOCK
    groups_per_block = A_COL_BLOCK // CHUNK

    def normed(r0):
        x = x_ref[r0:r0 + half, :]
        return (x * lax.rsqrt(jnp.mean(x * x, axis=-1, keepdims=True) + EPS) * ng_ref[...]).astype(_BF16)

    def layer_norm_stats(v):
        c = jnp.mean(v[:, :LANES], axis=-1, keepdims=True)
        d = v - c
        m1 = jnp.mean(d, axis=-1, keepdims=True)
        var = jnp.mean(d * d, axis=-1, keepdims=True) - m1 * m1
        return d, m1, lax.rsqrt(var + EPS)

    def gated_block(r0, h1, stats, j):
        c0 = j * A_COL_BLOCK
        blk = slice(c0, c0 + A_COL_BLOCK)
        d, m1, rstd = stats
        u = _dot(h1, win_ref[:, blk])
        gt = _dot(h1, win_ref[:, 2 * A_WIDTH + c0:2 * A_WIDTH + c0 + A_COL_BLOCK])
        for gi in range(groups_per_block):
            g = j * groups_per_block + gi
            gcols = slice(g * CHUNK, (g + 1) * CHUNK)
            vn = ((d[:, gcols] - m1) * rstd * lng_ref[:, gcols] + lnb_ref[:, gcols]).astype(_BF16)
            for c in range(half // CHUNK):
                rows = slice(c * CHUNK, (c + 1) * CHUNK)
                tile_rows = slice(r0 + c * CHUNK, r0 + (c + 1) * CHUNK)
                cols = slice(gi * CHUNK, (gi + 1) * CHUNK)
                sv = _dot(wsm_ref[g], vn[rows]) + bias_ref[g]
                y_ref[tile_rows, g * CHUNK:(g + 1) * CHUNK] = (
                    u[rows, cols] * sv * _silu(gt[rows, cols])).astype(_BF16)

    starts = [i * half for i in range(A_ROW_SPLIT)]
    h1 = [normed(r0) for r0 in starts]
    stats = [layer_norm_stats(_dot(h, win_ref[:, A_WIDTH:2 * A_WIDTH])) for h in h1]
    for j in range(n_blocks):
        for r0, h, st in zip(starts, h1, stats):
            gated_block(r0, h, st, j)
    for r0 in starts:
        o_ref[r0:r0 + half, :] = x_ref[r0:r0 + half, :] + _dot(y_ref[r0:r0 + half, :], wout_ref[...])


def _layer_a(x2, norm_g, w_in, ln_g, ln_b, ws, bs_t, w_out):
    t = x2.shape[0]
    tm = A_TOKENS_PER_STEP
    const = lambda i: (0, 0)
    return pl.pallas_call(
        _layer_a_kernel,
        out_shape=jax.ShapeDtypeStruct((t, D_MODEL), _F32),
        grid=(t // tm,),
        in_specs=[
            pl.BlockSpec((tm, D_MODEL), lambda i: (i, 0)),
            pl.BlockSpec((1, D_MODEL), const),
            pl.BlockSpec(memory_space=pl.ANY),
            pl.BlockSpec((1, A_WIDTH), const),
            pl.BlockSpec((1, A_WIDTH), const),
            pl.BlockSpec((A_GROUPS, CHUNK, CHUNK), lambda i: (0, 0, 0)),
            pl.BlockSpec((CHUNK, A_GROUPS), const),
            pl.BlockSpec(memory_space=pl.ANY),
        ],
        out_specs=pl.BlockSpec((tm, D_MODEL), lambda i: (i, 0)),
        scratch_shapes=[
            pltpu.VMEM((tm, A_WIDTH), _BF16),
            pltpu.VMEM((D_MODEL, 3 * A_WIDTH), _BF16),
            pltpu.VMEM((A_WIDTH, D_MODEL), _BF16),
            pltpu.VMEM((A_GROUPS, CHUNK, CHUNK), _BF16),
            pltpu.VMEM((A_GROUPS, CHUNK, CHUNK), _F32),
        ],
        compiler_params=pltpu.CompilerParams(
            dimension_semantics=("arbitrary",), vmem_limit_bytes=VMEM_LIMIT_BYTES),
        name="gmlp_layer",
    )(x2, norm_g, w_in, ln_g, ln_b, ws, bs_t, w_out)


def _layer_b_kernel(sinks_ref, hc_ref, hp_ref, kvg_ref, bg_ref, wkv_hbm, bkv_ref, win_hbm, bq_ref,
                    wout_hbm, fg_ref, cos_ref, sin_ref,
                    o_ref, kt_ref, vx_ref, q_ref, gate_ref, y_ref, wkv_ref, win_ref, wout_ref,
                    *, tiles_per_seq):
    t = pl.program_id(0)

    @pl.when(t == 0)
    def _():
        _load_weight_as_bf16(wkv_hbm, wkv_ref, kvg_ref)
        _load_weight_as_bf16(win_hbm, win_ref, bg_ref)
        _load_weight_as_bf16(wout_hbm, wout_ref)

    refs = (sinks_ref, hc_ref, hp_ref, kvg_ref, bg_ref, wkv_ref, bkv_ref, win_ref, bq_ref,
            wout_ref, fg_ref, cos_ref, sin_ref, o_ref, kt_ref, vx_ref, q_ref, gate_ref, y_ref)

    @pl.when(t % 2 == 0)
    def _():
        _layer_b_step(*refs, t=t, slot=0, tiles_per_seq=tiles_per_seq)

    @pl.when(t % 2 == 1)
    def _():
        _layer_b_step(*refs, t=t, slot=1, tiles_per_seq=tiles_per_seq)


def _layer_b_step(sinks_ref, hc_ref, hp_ref, kvg_ref, bg_ref, wkv_ref, bkv_ref, win_ref, bq_ref,
                  wout_ref, fg_ref, cos_ref, sin_ref,
                  o_ref, kt_ref, vx_ref, q_ref, gate_ref, y_ref, *, t, slot, tiles_per_seq):
    tm = hc_ref.shape[0]
    prev = 1 - slot
    new = slice(CHUNK, CHUNK + tm)
    planes = 2 * N_KV_HEADS

    if slot == 0:
        @pl.when(t == 0)
        def _():
            kt_ref[prev] = jnp.zeros((planes, LANES, CHUNK + tm), _BF16)
            vx_ref[prev] = jnp.zeros((planes, CHUNK + tm, 2 * LANES), _BF16)
            q_ref[prev] = jnp.zeros((tm, B_WIDTH), _BF16)
            gate_ref[prev] = jnp.zeros((tm, B_WIDTH), _F32)

    first_in_seq = (t % tiles_per_seq) == 0

    @pl.when(first_in_seq)
    def _():
        kt_ref[slot, :, :, 0:CHUNK] = jnp.zeros((planes, LANES, CHUNK), _BF16)
        vx_ref[slot, :, 0:CHUNK, :] = jnp.zeros((planes, CHUNK, 2 * LANES), _BF16)

    @pl.when(jnp.logical_not(first_in_seq))
    def _():
        kt_ref[slot, :, :, 0:CHUNK] = kt_ref[prev, :, :, tm:tm + CHUNK]
        vx_ref[slot, :, 0:CHUNK, :] = vx_ref[prev, :, tm:tm + CHUNK, :]

    h = hc_ref[...]
    h2 = (h * lax.rsqrt(jnp.mean(h * h, axis=-1, keepdims=True) + EPS)).astype(_BF16)
    kvn = h2

    cos = cos_ref[...]
    sin = sin_ref[...]
    lane = lax.broadcasted_iota(jnp.int32, (tm, LANES), 1)
    first_half = (lane % HEAD_DIM) < (HEAD_DIM // 2)
    low_head = lane < HEAD_DIM

    def rope(x, cos_t, sin_t):
        swapped = jnp.where(first_half,
                            pltpu.roll(x, LANES - HEAD_DIM // 2, 1),
                            pltpu.roll(x, HEAD_DIM // 2, 1))
        return x * cos_t + swapped * sin_t

    def project_kv():
        kv = _dot(kvn, wkv_ref[...]) + bkv_ref[...]
        k_t = rope(kv[:, :KV_WIDTH], cos, sin).T
        zero_rows = jnp.zeros((HEAD_DIM, tm), _F32)
        for kvh in range(N_KV_HEADS):
            head_t = k_t[kvh * HEAD_DIM:(kvh + 1) * HEAD_DIM]
            kt_ref[slot, 2 * kvh, :, new] = jnp.concatenate([head_t, zero_rows], axis=0).astype(_BF16)
            kt_ref[slot, 2 * kvh + 1, :, new] = jnp.concatenate([zero_rows, head_t], axis=0).astype(_BF16)
        v = kv[:, KV_WIDTH:]
        v_sw = pltpu.roll(v, HEAD_DIM, 1)
        zero = jnp.zeros_like(v)
        one = jnp.ones_like(v)
        ones_lo = jnp.where(low_head, one, zero)
        ones_hi = jnp.where(low_head, zero, one)
        v_low = (jnp.where(low_head, v, zero), jnp.where(low_head, v_sw, zero))
        v_high = (jnp.where(low_head, zero, v_sw), jnp.where(low_head, zero, v))
        for kvh in range(N_KV_HEADS):
            vx_ref[slot, 2 * kvh, new, :] = jnp.concatenate([v_low[kvh], ones_lo], axis=1).astype(_BF16)
            vx_ref[slot, 2 * kvh + 1, new, :] = jnp.concatenate([v_high[kvh], ones_hi], axis=1).astype(_BF16)

    q_scale = (HEAD_DIM ** -0.5) * LOG2_E
    cos_q = cos * q_scale
    sin_q = sin * q_scale

    def project_q(b):
        cols = slice(b * P_BLOCK, (b + 1) * P_BLOCK)
        zq = _dot(h2, win_ref[:, cols]) + bq_ref[:, cols]
        for i in range(P_BLOCK // LANES):
            s = b * (P_BLOCK // LANES) + i
            q_ref[slot, :, s * LANES:(s + 1) * LANES] = rope(
                zq[:, i * LANES:(i + 1) * LANES], cos_q, sin_q).astype(_BF16)

    def project_gate(b):
        cols = slice(b * P_BLOCK, (b + 1) * P_BLOCK)
        gate_ref[slot, :, cols] = _dot(h2, win_ref[:, B_WIDTH + b * P_BLOCK:B_WIDTH + (b + 1) * P_BLOCK])

    qi = lax.broadcasted_iota(jnp.int32, (CHUNK, 2 * CHUNK), 0)
    kj = lax.broadcasted_iota(jnp.int32, (CHUNK, 2 * CHUNK), 1)
    in_window = (kj > qi) & (kj <= qi + CHUNK)
    neg_inf = jnp.float32(-jnp.inf)
    bias = jnp.where(in_window, 0.0, neg_inf)
    prev_first_in_seq = ((t + tiles_per_seq - 1) % tiles_per_seq) == 0
    bias_first = jnp.where(in_window & ((kj >= CHUNK) | jnp.logical_not(prev_first_in_seq)), 0.0, neg_inf)

    units = [(c, kvh) for kvh in range(N_KV_HEADS) for c in range(tm // CHUNK)]
    assert len(units) == 4 and B_WIDTH // P_BLOCK == 4, "the program order below is written for 4 + 4"

    def scores(unit):
        c, kvh = unit
        rows = slice(c * CHUNK, (c + 1) * CHUNK)
        band = slice(c * CHUNK, (c + 2) * CHUNK)
        q_stack = jnp.concatenate(
            [q_ref[prev, rows, s * LANES:(s + 1) * LANES]
             for s in range(kvh * SLABS_PER_KV, (kvh + 1) * SLABS_PER_KV)], axis=0)
        return [_dot(q_stack, kt_ref[prev, 2 * kvh + half, :, band]) for half in range(HEADS_PER_SLAB)]

    sink_slot = lax.broadcasted_iota(jnp.int32, (CHUNK, LANES), 1) == 0
    key_row = lax.broadcasted_iota(jnp.int32, (BF16_ROWS, 2 * LANES), 0)
    key_lane = lax.broadcasted_iota(jnp.int32, (BF16_ROWS, 2 * LANES), 1)

    def softmax(unit, sc_halves):
        c, kvh = unit
        chunk_bias = bias_first if c == 0 else bias
        probs = []
        for half, sc in enumerate(sc_halves):
            p_parts = []
            for si in range(SLABS_PER_KV):
                s = kvh * SLABS_PER_KV + si
                sink = sinks_ref[s * HEADS_PER_SLAB + half] * LOG2_E
                blk = sc[si * CHUNK:(si + 1) * CHUNK]
                sb = jnp.concatenate(
                    [jnp.where(sink_slot, sink, blk[:, :LANES] + chunk_bias[:, :LANES]),
                     blk[:, LANES:] + chunk_bias[:, LANES:]], axis=1)
                m = jnp.max(sb, axis=-1, keepdims=True)
                p_parts.append(jnp.exp2(sb - m).astype(_BF16))
            probs.append(jnp.concatenate(p_parts, axis=0))
        return probs

    def band_values(kvh, half, c):
        r0 = c * CHUNK
        ones_lo = LANES + half * HEAD_DIM
        sink_row = jnp.where((key_lane >= ones_lo) & (key_lane < ones_lo + HEAD_DIM), 1.0, 0.0)
        head = vx_ref[prev, 2 * kvh + half, r0:r0 + BF16_ROWS, :].astype(_F32)
        head = jnp.where(key_row == 0, sink_row, head).astype(_BF16)
        return jnp.concatenate(
            [head, vx_ref[prev, 2 * kvh + half, r0 + BF16_ROWS:r0 + 2 * CHUNK, :]], axis=0)

    def attend(unit, probs):
        c, kvh = unit
        rows = slice(c * CHUNK, (c + 1) * CHUNK)
        od = _dot(probs[0], band_values(kvh, 0, c)) + _dot(probs[1], band_values(kvh, 1, c))
        for si in range(SLABS_PER_KV):
            s = kvh * SLABS_PER_KV + si
            blk = slice(si * CHUNK, (si + 1) * CHUNK)
            gate = gate_ref[prev, rows, s * LANES:(s + 1) * LANES]
            y_ref[rows, s * LANES:(s + 1) * LANES] = (
                od[blk, :LANES] * _silu(gate) / od[blk, LANES:]).astype(_BF16)

    def out_part(kvh):
        cols = slice(kvh * SLABS_PER_KV * LANES, (kvh + 1) * SLABS_PER_KV * LANES)
        return _dot(y_ref[:, cols], wout_ref[cols, :])

    s0 = scores(units[0])
    project_kv()
    s1 = scores(units[1])
    project_q(0)
    p0 = softmax(units[0], s0)
    attend(units[0], p0)
    s2 = scores(units[2])
    project_q(1)
    p1 = softmax(units[1], s1)
    attend(units[1], p1)
    s3 = scores(units[3])
    project_q(2)
    p2 = softmax(units[2], s2)
    attend(units[2], p2)
    project_q(3)
    project_gate(0)
    acc = out_part(0)
    p3 = softmax(units[3], s3)
    attend(units[3], p3)
    project_gate(1)
    out = hp_ref[...] + acc + out_part(1)
    o_ref[...] = out * lax.rsqrt(jnp.mean(out * out, axis=-1, keepdims=True) + EPS) * fg_ref[...]
    project_gate(2)
    project_gate(3)


def _layer_b(h2d, seq, sinks, kv_g, b_g, w_kv, b_kv, w_in, b_q, w_out, f_g, cos, sin):
    tm = B_TOKENS_PER_STEP
    n_tiles = h2d.shape[0] // tm
    tiles_per_seq = seq // tm
    const = lambda t: (0, 0)
    cur = lambda t: (jnp.minimum(t, n_tiles - 1), 0)
    prv = lambda t: (jnp.maximum(t - 1, 0), 0)
    tab = lambda t: (jnp.minimum(t, n_tiles - 1) % tiles_per_seq, 0)
    planes = 2 * N_KV_HEADS
    return pl.pallas_call(
        functools.partial(_layer_b_kernel, tiles_per_seq=tiles_per_seq),
        out_shape=jax.ShapeDtypeStruct(h2d.shape, _F32),
        grid=(n_tiles + 1,),
        in_specs=[
            pl.BlockSpec(memory_space=pltpu.SMEM),
            pl.BlockSpec((tm, D_MODEL), cur),
            pl.BlockSpec((tm, D_MODEL), prv),
            pl.BlockSpec((1, D_MODEL), const),
            pl.BlockSpec((1, D_MODEL), const),
            pl.BlockSpec(memory_space=pl.ANY),
            pl.BlockSpec((1, 2 * KV_WIDTH), const),
            pl.BlockSpec(memory_space=pl.ANY),
            pl.BlockSpec((1, B_WIDTH), const),
            pl.BlockSpec(memory_space=pl.ANY),
            pl.BlockSpec((1, D_MODEL), const),
            pl.BlockSpec((tm, LANES), tab),
            pl.BlockSpec((tm, LANES), tab),
        ],
        out_specs=pl.BlockSpec((tm, D_MODEL), prv),
        scratch_shapes=[
            pltpu.VMEM((2, planes, LANES, CHUNK + tm), _BF16),
            pltpu.VMEM((2, planes, CHUNK + tm, 2 * LANES), _BF16),
            pltpu.VMEM((2, tm, B_WIDTH), _BF16),
            pltpu.VMEM((2, tm, B_WIDTH), _F32),
            pltpu.VMEM((tm, B_WIDTH), _BF16),
            pltpu.VMEM((D_MODEL, 2 * KV_WIDTH), _BF16),
            pltpu.VMEM((D_MODEL, 2 * B_WIDTH), _BF16),
            pltpu.VMEM((B_WIDTH, D_MODEL), _BF16),
        ],
        compiler_params=pltpu.CompilerParams(
            dimension_semantics=("arbitrary",), vmem_limit_bytes=VMEM_LIMIT_BYTES),
        name="swa_layer",
    )(sinks, h2d, h2d, kv_g, b_g, w_kv, b_kv, w_in, b_q, w_out, f_g, cos, sin)


def _rope_tables(seq):
    inv_freq = ROPE_THETA ** (-np.arange(0, HEAD_DIM, 2, dtype=np.float64) / HEAD_DIM)
    ang = np.arange(seq, dtype=np.float64)[:, None] * inv_freq[None, :]
    cos = np.cos(ang)
    sin = np.sin(ang)
    cos_slab = np.tile(np.concatenate([cos, cos], axis=-1), (1, HEADS_PER_SLAB))
    sin_slab = np.tile(np.concatenate([-sin, sin], axis=-1), (1, HEADS_PER_SLAB))
    assert cos_slab.shape == (seq, LANES)
    return jnp.asarray(cos_slab, dtype=_F32), jnp.asarray(sin_slab, dtype=_F32)


def kernel(x, a_norm_g, a_w_in, a_ln_g, a_ln_b, a_ws, a_bs, a_w_out, kv_norm_g, w_kv, b_kv,
           b_norm_g, b_w_in, b_bq, b_sinks, b_w_out, final_norm_g):
    batch, seq, d = x.shape
    assert d == D_MODEL
    for tile in (A_TOKENS_PER_STEP, B_TOKENS_PER_STEP):
        assert seq % tile == 0 and tile % CHUNK == 0
    assert a_w_in.shape[0] == 1 and b_w_in.shape[0] == 1
    row = lambda p: p.reshape(1, -1)

    x2 = x.reshape(batch * seq, d)
    h = _layer_a(x2, row(a_norm_g[0]), a_w_in[0], row(a_ln_g[0]), row(a_ln_b[0]),
                 a_ws[0], a_bs[0].T, a_w_out[0])
    cos, sin = _rope_tables(seq)
    out = _layer_b(h, seq, b_sinks[0], row(kv_norm_g), row(b_norm_g[0]),
                   w_kv, row(b_kv), b_w_in[0], row(b_bq[0]),
                   b_w_out[0], row(final_norm_g), cos, sin)
    return out.reshape(batch, seq, d)
```

```python
import functools

import jax
import jax.numpy as jnp
import numpy as np
from jax import lax
from jax.experimental import pallas as pl
from jax.experimental.pallas import tpu as pltpu

D_MODEL = 1024
CHUNK = 128
A_WIDTH = 2048
A_GROUPS = 16
HEAD_DIM = 64
N_Q_HEADS = 16
N_KV_HEADS = 2
B_WIDTH = 1024
KV_WIDTH = 128
ROPE_THETA = 10000.0
EPS = 1e-5
LOG2_E = 1.4426950408889634

LANES = 128
BF16_ROWS = 16
HEADS_PER_SLAB = LANES // HEAD_DIM
N_SLABS = B_WIDTH // LANES
SLABS_PER_KV = N_SLABS // N_KV_HEADS

A_TOKENS_PER_STEP = 512
B_TOKENS_PER_STEP = 256
P_BLOCK = 256
A_COL_BLOCK = 512
A_ROW_SPLIT = 2
VMEM_LIMIT_BYTES = 56 * 1024 * 1024
WEIGHT_STAGE_BYTES = 2 * 1024 * 1024

_F32 = jnp.float32
_BF16 = jnp.bfloat16


def _dot(a, b):
    return jnp.dot(a, b, preferred_element_type=_F32)


def _silu(g):
    half_g = 0.5 * g
    return half_g * (1.0 + jnp.tanh(half_g))


def _load_weight_as_bf16(w_hbm, w_ref, row_scale_ref=None):
    k, n = w_hbm.shape
    rows = min(k, pl.next_power_of_2(WEIGHT_STAGE_BYTES // (4 * n) + 1) // 2)
    assert k % rows == 0 and rows % 16 == 0
    n_chunks = k // rows
    if row_scale_ref is not None:
        scale_col = jnp.broadcast_to(row_scale_ref[...], (LANES, k)).T

    def body(stage, sem):
        def copy(i):
            return pltpu.make_async_copy(w_hbm.at[pl.ds(i * rows, rows), :], stage.at[i % 2], sem.at[i % 2])

        copy(0).start()
        for i in range(n_chunks):
            if i + 1 < n_chunks:
                copy(i + 1).start()
            copy(i).wait()
            chunk = stage[i % 2]
            if row_scale_ref is not None:
                chunk = chunk * scale_col[i * rows:(i + 1) * rows, 0:1]
            w_ref[i * rows:(i + 1) * rows, :] = chunk.astype(_BF16)

    pl.run_scoped(body, pltpu.VMEM((2, rows, n), _F32), pltpu.SemaphoreType.DMA((2,)))


def _layer_a_kernel(x_ref, ng_ref, win_hbm, lng_ref, lnb_ref, ws_ref, bs_ref, wout_hbm,
                    o_ref, y_ref, win_ref, wout_ref, wsm_ref, bias_ref):
    @pl.when(pl.program_id(0) == 0)
    def _():
        _load_weight_as_bf16(win_hbm, win_ref)
        _load_weight_as_bf16(wout_hbm, wout_ref)
        row = lax.broadcasted_iota(jnp.int32, (CHUNK, CHUNK), 0)
        col = lax.broadcasted_iota(jnp.int32, (CHUNK, CHUNK), 1)
        for g in range(A_GROUPS):
            wsm_ref[g] = jnp.where(col <= row, ws_ref[g], 0.0).astype(_BF16)
            bias_ref[g] = jnp.broadcast_to(bs_ref[g:g + 1, :], (CHUNK, CHUNK)).T

    tm = x_ref.shape[0]
    half = tm // A_ROW_SPLIT
    n_blocks = A_WIDTH // A_COL_BLOCK
    groups_per_block = A_COL_BLOCK // CHUNK

    def normed(r0):
        x = x_ref[r0:r0 + half, :]
        return (x * lax.rsqrt(jnp.mean(x * x, axis=-1, keepdims=True) + EPS) * ng_ref[...]).astype(_BF16)

    def layer_norm_stats(v):
        c = jnp.mean(v[:, :LANES], axis=-1, keepdims=True)
        d = v - c
        m1 = jnp.mean(d, axis=-1, keepdims=True)
        var = jnp.mean(d * d, axis=-1, keepdims=True) - m1 * m1
        return d, m1, lax.rsqrt(var + EPS)

    def gated_block(r0, h1, stats, j):
        c0 = j * A_COL_BLOCK
        blk = slice(c0, c0 + A_COL_BLOCK)
        d, m1, rstd = stats
        u = _dot(h1, win_ref[:, blk])
        gt = _dot(h1, win_ref[:, 2 * A_WIDTH + c0:2 * A_WIDTH + c0 + A_COL_BLOCK])
        for gi in range(groups_per_block):
            g = j * groups_per_block + gi
            gcols = slice(g * CHUNK, (g + 1) * CHUNK)
            vn = ((d[:, gcols] - m1) * rstd * lng_ref[:, gcols] + lnb_ref[:, gcols]).astype(_BF16)
            for c in range(half // CHUNK):
                rows = slice(c * CHUNK, (c + 1) * CHUNK)
                tile_rows = slice(r0 + c * CHUNK, r0 + (c + 1) * CHUNK)
                cols = slice(gi * CHUNK, (gi + 1) * CHUNK)
                sv = _dot(wsm_ref[g], vn[rows]) + bias_ref[g]
                y_ref[tile_rows, g * CHUNK:(g + 1) * CHUNK] = (
                    u[rows, cols] * sv * _silu(gt[rows, cols])).astype(_BF16)

    starts = [i * half for i in range(A_ROW_SPLIT)]
    h1 = [normed(r0) for r0 in starts]
    stats = [layer_norm_stats(_dot(h, win_ref[:, A_WIDTH:2 * A_WIDTH])) for h in h1]
    for j in range(n_blocks):
        for r0, h, st in zip(starts, h1, stats):
            gated_block(r0, h, st, j)
    for r0 in starts:
        o_ref[r0:r0 + half, :] = x_ref[r0:r0 + half, :] + _dot(y_ref[r0:r0 + half, :], wout_ref[...])


def _layer_a(x2, norm_g, w_in, ln_g, ln_b, ws, bs, w_out):
    t = x2.shape[0]
    tm = A_TOKENS_PER_STEP
    const = lambda i: (0, 0)
    return pl.pallas_call(
        _layer_a_kernel,
        out_shape=jax.ShapeDtypeStruct((t, D_MODEL), _F32),
        grid=(t // tm,),
        in_specs=[
            pl.BlockSpec((tm, D_MODEL), lambda i: (i, 0)),
            pl.BlockSpec((1, D_MODEL), const),
            pl.BlockSpec(memory_space=pl.ANY),
            pl.BlockSpec((1, A_WIDTH), const),
            pl.BlockSpec((1, A_WIDTH), const),
            pl.BlockSpec((A_GROUPS, CHUNK, CHUNK), lambda i: (0, 0, 0)),
            pl.BlockSpec((A_GROUPS, CHUNK), const),
            pl.BlockSpec(memory_space=pl.ANY),
        ],
        out_specs=pl.BlockSpec((tm, D_MODEL), lambda i: (i, 0)),
        scratch_shapes=[
            pltpu.VMEM((tm, A_WIDTH), _BF16),
            pltpu.VMEM((D_MODEL, 3 * A_WIDTH), _BF16),
            pltpu.VMEM((A_WIDTH, D_MODEL), _BF16),
            pltpu.VMEM((A_GROUPS, CHUNK, CHUNK), _BF16),
            pltpu.VMEM((A_GROUPS, CHUNK, CHUNK), _F32),
        ],
        compiler_params=pltpu.CompilerParams(
            dimension_semantics=("arbitrary",), vmem_limit_bytes=VMEM_LIMIT_BYTES),
        name="gmlp_layer",
    )(x2, norm_g, w_in, ln_g, ln_b, ws, bs, w_out)


def _layer_b_kernel(sinks_ref, hc_ref, hp_ref, kvg_ref, bg_ref, wkv_hbm, bkv_ref, win_hbm, bq_ref,
                    wout_hbm, fg_ref, cos_ref, sin_ref,
                    o_ref, kt_ref, vx_ref, q_ref, gate_ref, y_ref, wkv_ref, win_ref, wout_ref,
                    *, tiles_per_seq):
    t = pl.program_id(0)

    @pl.when(t == 0)
    def _():
        _load_weight_as_bf16(wkv_hbm, wkv_ref, kvg_ref)
        _load_weight_as_bf16(win_hbm, win_ref, bg_ref)
        _load_weight_as_bf16(wout_hbm, wout_ref)

    refs = (sinks_ref, hc_ref, hp_ref, kvg_ref, bg_ref, wkv_ref, bkv_ref, win_ref, bq_ref,
            wout_ref, fg_ref, cos_ref, sin_ref, o_ref, kt_ref, vx_ref, q_ref, gate_ref, y_ref)

    @pl.when(t % 2 == 0)
    def _():
        _layer_b_step(*refs, t=t, slot=0, tiles_per_seq=tiles_per_seq)

    @pl.when(t % 2 == 1)
    def _():
        _layer_b_step(*refs, t=t, slot=1, tiles_per_seq=tiles_per_seq)


def _layer_b_step(sinks_ref, hc_ref, hp_ref, kvg_ref, bg_ref, wkv_ref, bkv_ref, win_ref, bq_ref,
                  wout_ref, fg_ref, cos_ref, sin_ref,
                  o_ref, kt_ref, vx_ref, q_ref, gate_ref, y_ref, *, t, slot, tiles_per_seq):
    tm = hc_ref.shape[0]
    prev = 1 - slot
    new = slice(CHUNK, CHUNK + tm)
    planes = 2 * N_KV_HEADS

    if slot == 0:
        @pl.when(t == 0)
        def _():
            kt_ref[prev] = jnp.zeros((planes, LANES, CHUNK + tm), _BF16)
            vx_ref[prev] = jnp.zeros((planes, CHUNK + tm, 2 * LANES), _BF16)
            q_ref[prev] = jnp.zeros((tm, B_WIDTH), _BF16)
            gate_ref[prev] = jnp.zeros((tm, B_WIDTH), _F32)

    first_in_seq = (t % tiles_per_seq) == 0

    @pl.when(first_in_seq)
    def _():
        kt_ref[slot, :, :, 0:CHUNK] = jnp.zeros((planes, LANES, CHUNK), _BF16)
        vx_ref[slot, :, 0:CHUNK, :] = jnp.zeros((planes, CHUNK, 2 * LANES), _BF16)

    @pl.when(jnp.logical_not(first_in_seq))
    def _():
        kt_ref[slot, :, :, 0:CHUNK] = kt_ref[prev, :, :, tm:tm + CHUNK]
        vx_ref[slot, :, 0:CHUNK, :] = vx_ref[prev, :, tm:tm + CHUNK, :]

    h = hc_ref[...]
    h2 = (h * lax.rsqrt(jnp.mean(h * h, axis=-1, keepdims=True) + EPS)).astype(_BF16)
    kvn = h2

    cos = cos_ref[...]
    sin = sin_ref[...]
    lane = lax.broadcasted_iota(jnp.int32, (tm, LANES), 1)
    first_half = (lane % HEAD_DIM) < (HEAD_DIM // 2)
    low_head = lane < HEAD_DIM

    def rope(x, cos_t, sin_t):
        swapped = jnp.where(first_half,
                            pltpu.roll(x, LANES - HEAD_DIM // 2, 1),
                            pltpu.roll(x, HEAD_DIM // 2, 1))
        return x * cos_t + swapped * sin_t

    def project_kv():
        kv = _dot(kvn, wkv_ref[...]) + bkv_ref[...]
        k_t = rope(kv[:, :KV_WIDTH], cos, sin).T
        zero_rows = jnp.zeros((HEAD_DIM, tm), _F32)
        for kvh in range(N_KV_HEADS):
            head_t = k_t[kvh * HEAD_DIM:(kvh + 1) * HEAD_DIM]
            kt_ref[slot, 2 * kvh, :, new] = jnp.concatenate([head_t, zero_rows], axis=0).astype(_BF16)
            kt_ref[slot, 2 * kvh + 1, :, new] = jnp.concatenate([zero_rows, head_t], axis=0).astype(_BF16)
        v = kv[:, KV_WIDTH:]
        v_sw = pltpu.roll(v, HEAD_DIM, 1)
        zero = jnp.zeros_like(v)
        one = jnp.ones_like(v)
        ones_lo = jnp.where(low_head, one, zero)
        ones_hi = jnp.where(low_head, zero, one)
        v_low = (jnp.where(low_head, v, zero), jnp.where(low_head, v_sw, zero))
        v_high = (jnp.where(low_head, zero, v_sw), jnp.where(low_head, zero, v))
        for kvh in range(N_KV_HEADS):
            vx_ref[slot, 2 * kvh, new, :] = jnp.concatenate([v_low[kvh], ones_lo], axis=1).astype(_BF16)
            vx_ref[slot, 2 * kvh + 1, new, :] = jnp.concatenate([v_high[kvh], ones_hi], axis=1).astype(_BF16)

    q_scale = (HEAD_DIM ** -0.5) * LOG2_E
    cos_q = cos * q_scale
    sin_q = sin * q_scale

    def project_q(b):
        cols = slice(b * P_BLOCK, (b + 1) * P_BLOCK)
        zq = _dot(h2, win_ref[:, cols]) + bq_ref[:, cols]
        for i in range(P_BLOCK // LANES):
            s = b * (P_BLOCK // LANES) + i
            q_ref[slot, :, s * LANES:(s + 1) * LANES] = rope(
                zq[:, i * LANES:(i + 1) * LANES], cos_q, sin_q).astype(_BF16)

    def project_gate(b):
        cols = slice(b * P_BLOCK, (b + 1) * P_BLOCK)
        gate_ref[slot, :, cols] = _dot(h2, win_ref[:, B_WIDTH + b * P_BLOCK:B_WIDTH + (b + 1) * P_BLOCK])

    qi = lax.broadcasted_iota(jnp.int32, (CHUNK, 2 * CHUNK), 0)
    kj = lax.broadcasted_iota(jnp.int32, (CHUNK, 2 * CHUNK), 1)
    in_window = (kj > qi) & (kj <= qi + CHUNK)
    neg_inf = jnp.float32(-jnp.inf)
    bias = jnp.where(in_window, 0.0, neg_inf)
    prev_first_in_seq = ((t + tiles_per_seq - 1) % tiles_per_seq) == 0
    bias_first = jnp.where(in_window & ((kj >= CHUNK) | jnp.logical_not(prev_first_in_seq)), 0.0, neg_inf)

    units = [(c, kvh) for kvh in range(N_KV_HEADS) for c in range(tm // CHUNK)]
    assert len(units) == 4 and B_WIDTH // P_BLOCK == 4, "the program order below is written for 4 + 4"

    def scores(unit):
        c, kvh = unit
        rows = slice(c * CHUNK, (c + 1) * CHUNK)
        band = slice(c * CHUNK, (c + 2) * CHUNK)
        q_stack = jnp.concatenate(
            [q_ref[prev, rows, s * LANES:(s + 1) * LANES]
             for s in range(kvh * SLABS_PER_KV, (kvh + 1) * SLABS_PER_KV)], axis=0)
        return [_dot(q_stack, kt_ref[prev, 2 * kvh + half, :, band]) for half in range(HEADS_PER_SLAB)]

    sink_slot = lax.broadcasted_iota(jnp.int32, (CHUNK, LANES), 1) == 0
    key_row = lax.broadcasted_iota(jnp.int32, (BF16_ROWS, 2 * LANES), 0)
    key_lane = lax.broadcasted_iota(jnp.int32, (BF16_ROWS, 2 * LANES), 1)

    def softmax(unit, sc_halves):
        c, kvh = unit
        chunk_bias = bias_first if c == 0 else bias
        probs = []
        for half, sc in enumerate(sc_halves):
            p_parts = []
            for si in range(SLABS_PER_KV):
                s = kvh * SLABS_PER_KV + si
                sink = sinks_ref[s * HEADS_PER_SLAB + half] * LOG2_E
                blk = sc[si * CHUNK:(si + 1) * CHUNK]
                sb = jnp.concatenate(
                    [jnp.where(sink_slot, sink, blk[:, :LANES] + chunk_bias[:, :LANES]),
                     blk[:, LANES:] + chunk_bias[:, LANES:]], axis=1)
                m = jnp.max(sb, axis=-1, keepdims=True)
                p_parts.append(jnp.exp2(sb - m).astype(_BF16))
            probs.append(jnp.concatenate(p_parts, axis=0))
        return probs

    def band_values(kvh, half, c):
        r0 = c * CHUNK
        ones_lo = LANES + half * HEAD_DIM
        sink_row = jnp.where((key_lane >= ones_lo) & (key_lane < ones_lo + HEAD_DIM), 1.0, 0.0)
        head = vx_ref[prev, 2 * kvh + half, r0:r0 + BF16_ROWS, :].astype(_F32)
        head = jnp.where(key_row == 0, sink_row, head).astype(_BF16)
        return jnp.concatenate(
            [head, vx_ref[prev, 2 * kvh + half, r0 + BF16_ROWS:r0 + 2 * CHUNK, :]], axis=0)

    def attend(unit, probs):
        c, kvh = unit
        rows = slice(c * CHUNK, (c + 1) * CHUNK)
        od = _dot(probs[0], band_values(kvh, 0, c)) + _dot(probs[1], band_values(kvh, 1, c))
        for si in range(SLABS_PER_KV):
            s = kvh * SLABS_PER_KV + si
            blk = slice(si * CHUNK, (si + 1) * CHUNK)
            gate = gate_ref[prev, rows, s * LANES:(s + 1) * LANES]
            y_ref[rows, s * LANES:(s + 1) * LANES] = (
                od[blk, :LANES] * _silu(gate) / od[blk, LANES:]).astype(_BF16)

    def out_part(kvh):
        cols = slice(kvh * SLABS_PER_KV * LANES, (kvh + 1) * SLABS_PER_KV * LANES)
        return _dot(y_ref[:, cols], wout_ref[cols, :])

    s0 = scores(units[0])
    project_kv()
    s1 = scores(units[1])
    project_q(0)
    p0 = softmax(units[0], s0)
    attend(units[0], p0)
    s2 = scores(units[2])
    project_q(1)
    p1 = softmax(units[1], s1)
    attend(units[1], p1)
    s3 = scores(units[3])
    project_q(2)
    p2 = softmax(units[2], s2)
    attend(units[2], p2)
    project_q(3)
    project_gate(0)
    acc = out_part(0)
    p3 = softmax(units[3], s3)
    attend(units[3], p3)
    project_gate(1)
    out = hp_ref[...] + acc + out_part(1)
    o_ref[...] = out * lax.rsqrt(jnp.mean(out * out, axis=-1, keepdims=True) + EPS) * fg_ref[...]
    project_gate(2)
    project_gate(3)


def _layer_b(h2d, seq, sinks, kv_g, b_g, w_kv, b_kv, w_in, b_q, w_out, f_g, cos, sin):
    tm = B_TOKENS_PER_STEP
    n_tiles = h2d.shape[0] // tm
    tiles_per_seq = seq // tm
    const = lambda t: (0, 0)
    cur = lambda t: (jnp.minimum(t, n_tiles - 1), 0)
    prv = lambda t: (jnp.maximum(t - 1, 0), 0)
    tab = lambda t: (jnp.minimum(t, n_tiles - 1) % tiles_per_seq, 0)
    planes = 2 * N_KV_HEADS
    return pl.pallas_call(
        functools.partial(_layer_b_kernel, tiles_per_seq=tiles_per_seq),
        out_shape=jax.ShapeDtypeStruct(h2d.shape, _F32),
        grid=(n_tiles + 1,),
        in_specs=[
            pl.BlockSpec(memory_space=pltpu.SMEM),
            pl.BlockSpec((tm, D_MODEL), cur),
            pl.BlockSpec((tm, D_MODEL), prv),
            pl.BlockSpec((1, D_MODEL), const),
            pl.BlockSpec((1, D_MODEL), const),
            pl.BlockSpec(memory_space=pl.ANY),
            pl.BlockSpec((1, 2 * KV_WIDTH), const),
            pl.BlockSpec(memory_space=pl.ANY),
            pl.BlockSpec((1, B_WIDTH), const),
            pl.BlockSpec(memory_space=pl.ANY),
            pl.BlockSpec((1, D_MODEL), const),
            pl.BlockSpec((tm, LANES), tab),
            pl.BlockSpec((tm, LANES), tab),
        ],
        out_specs=pl.BlockSpec((tm, D_MODEL), prv),
        scratch_shapes=[
            pltpu.VMEM((2, planes, LANES, CHUNK + tm), _BF16),
            pltpu.VMEM((2, planes, CHUNK + tm, 2 * LANES), _BF16),
            pltpu.VMEM((2, tm, B_WIDTH), _BF16),
            pltpu.VMEM((2, tm, B_WIDTH), _F32),
            pltpu.VMEM((tm, B_WIDTH), _BF16),
            pltpu.VMEM((D_MODEL, 2 * KV_WIDTH), _BF16),
            pltpu.VMEM((D_MODEL, 2 * B_WIDTH), _BF16),
            pltpu.VMEM((B_WIDTH, D_MODEL), _BF16),
        ],
        compiler_params=pltpu.CompilerParams(
            dimension_semantics=("arbitrary",), vmem_limit_bytes=VMEM_LIMIT_BYTES),
        name="swa_layer",
    )(sinks, h2d, h2d, kv_g, b_g, w_kv, b_kv, w_in, b_q, w_out, f_g, cos, sin)


def _rope_tables(seq):
    inv_freq = ROPE_THETA ** (-np.arange(0, HEAD_DIM, 2, dtype=np.float64) / HEAD_DIM)
    ang = np.arange(seq, dtype=np.float64)[:, None] * inv_freq[None, :]
    cos = np.cos(ang)
    sin = np.sin(ang)
    cos_slab = np.tile(np.concatenate([cos, cos], axis=-1), (1, HEADS_PER_SLAB))
    sin_slab = np.tile(np.concatenate([-sin, sin], axis=-1), (1, HEADS_PER_SLAB))
    assert cos_slab.shape == (seq, LANES)
    return jnp.asarray(cos_slab, dtype=_F32), jnp.asarray(sin_slab, dtype=_F32)


def kernel(x, a_norm_g, a_w_in, a_ln_g, a_ln_b, a_ws, a_bs, a_w_out, kv_norm_g, w_kv, b_kv,
           b_norm_g, b_w_in, b_bq, b_sinks, b_w_out, final_norm_g):
    batch, seq, d = x.shape
    assert d == D_MODEL
    for tile in (A_TOKENS_PER_STEP, B_TOKENS_PER_STEP):
        assert seq % tile == 0 and tile % CHUNK == 0
    assert a_w_in.shape[0] == 1 and b_w_in.shape[0] == 1
    row = lambda p: p.reshape(1, -1)

    x2 = x.reshape(batch * seq, d)
    h = _layer_a(x2, row(a_norm_g[0]), a_w_in[0], row(a_ln_g[0]), row(a_ln_b[0]),
                 a_ws[0], a_bs[0], a_w_out[0])
    cos, sin = _rope_tables(seq)
    out = _layer_b(h, seq, b_sinks[0], row(kv_norm_g), row(b_norm_g[0]),
                   w_kv, row(b_kv), b_w_in[0], row(b_bq[0]),
                   b_w_out[0], row(final_norm_g), cos, sin)
    return out.reshape(batch, seq, d)
```

```python
import functools

import jax
import jax.numpy as jnp
import numpy as np
from jax import lax
from jax.experimental import pallas as pl
from jax.experimental.pallas import tpu as pltpu

D_MODEL = 1024
CHUNK = 128
A_WIDTH = 2048
A_GROUPS = 16
HEAD_DIM = 64
N_Q_HEADS = 16
N_KV_HEADS = 2
B_WIDTH = 1024
KV_WIDTH = 128
ROPE_THETA = 10000.0
EPS = 1e-5
LOG2_E = 1.4426950408889634

LANES = 128
BF16_ROWS = 16
HEADS_PER_SLAB = LANES // HEAD_DIM
N_SLABS = B_WIDTH // LANES
SLABS_PER_KV = N_SLABS // N_KV_HEADS

A_TOKENS_PER_STEP = 1024
B_TOKENS_PER_STEP = 256
P_BLOCK = 256
A_COL_BLOCK = 512
A_ROW_SPLIT = 1
VMEM_LIMIT_BYTES = 56 * 1024 * 1024
WEIGHT_STAGE_BYTES = 2 * 1024 * 1024

_F32 = jnp.float32
_BF16 = jnp.bfloat16


def _dot(a, b):
    return jnp.dot(a, b, preferred_element_type=_F32)


def _silu(g):
    half_g = 0.5 * g
    return half_g * (1.0 + jnp.tanh(half_g))


def _load_weight_as_bf16(w_hbm, w_ref, row_scale_ref=None):
    k, n = w_hbm.shape
    rows = min(k, pl.next_power_of_2(WEIGHT_STAGE_BYTES // (4 * n) + 1) // 2)
    assert k % rows == 0 and rows % 16 == 0
    n_chunks = k // rows
    if row_scale_ref is not None:
        scale_col = jnp.broadcast_to(row_scale_ref[...], (LANES, k)).T

    def body(stage, sem):
        def copy(i):
            return pltpu.make_async_copy(w_hbm.at[pl.ds(i * rows, rows), :], stage.at[i % 2], sem.at[i % 2])

        copy(0).start()
        for i in range(n_chunks):
            if i + 1 < n_chunks:
                copy(i + 1).start()
            copy(i).wait()
            chunk = stage[i % 2]
            if row_scale_ref is not None:
                chunk = chunk * scale_col[i * rows:(i + 1) * rows, 0:1]
            w_ref[i * rows:(i + 1) * rows, :] = chunk.astype(_BF16)

    pl.run_scoped(body, pltpu.VMEM((2, rows, n), _F32), pltpu.SemaphoreType.DMA((2,)))


def _layer_a_kernel(x_ref, ng_ref, win_hbm, lng_ref, lnb_ref, ws_ref, bs_ref, wout_hbm,
                    o_ref, y_ref, win_ref, wout_ref, wsm_ref, bias_ref):
    @pl.when(pl.program_id(0) == 0)
    def _():
        _load_weight_as_bf16(win_hbm, win_ref)
        _load_weight_as_bf16(wout_hbm, wout_ref)
        row = lax.broadcasted_iota(jnp.int32, (CHUNK, CHUNK), 0)
        col = lax.broadcasted_iota(jnp.int32, (CHUNK, CHUNK), 1)
        for g in range(A_GROUPS):
            wsm_ref[g] = jnp.where(col <= row, ws_ref[g], 0.0).astype(_BF16)
            bias_ref[g] = jnp.broadcast_to(bs_ref[g:g + 1, :], (CHUNK, CHUNK)).T

    tm = x_ref.shape[0]
    half = tm // A_ROW_SPLIT
    n_blocks = A_WIDTH // A_COL_BLOCK
    groups_per_block = A_COL_BLOCK // CHUNK

    def normed(r0):
        x = x_ref[r0:r0 + half, :]
        return (x * lax.rsqrt(jnp.mean(x * x, axis=-1, keepdims=True) + EPS) * ng_ref[...]).astype(_BF16)

    def layer_norm_stats(v):
        c = jnp.mean(v[:, :LANES], axis=-1, keepdims=True)
        d = v - c
        m1 = jnp.mean(d, axis=-1, keepdims=True)
        var = jnp.mean(d * d, axis=-1, keepdims=True) - m1 * m1
        return d, m1, lax.rsqrt(var + EPS)

    def gated_block(r0, h1, stats, j):
        c0 = j * A_COL_BLOCK
        blk = slice(c0, c0 + A_COL_BLOCK)
        d, m1, rstd = stats
        u = _dot(h1, win_ref[:, blk])
        gt = _dot(h1, win_ref[:, 2 * A_WIDTH + c0:2 * A_WIDTH + c0 + A_COL_BLOCK])
        for gi in range(groups_per_block):
            g = j * groups_per_block + gi
            gcols = slice(g * CHUNK, (g + 1) * CHUNK)
            vn = ((d[:, gcols] - m1) * rstd * lng_ref[:, gcols] + lnb_ref[:, gcols]).astype(_BF16)
            for c in range(half // CHUNK):
                rows = slice(c * CHUNK, (c + 1) * CHUNK)
                tile_rows = slice(r0 + c * CHUNK, r0 + (c + 1) * CHUNK)
                cols = slice(gi * CHUNK, (gi + 1) * CHUNK)
                sv = _dot(wsm_ref[g], vn[rows]) + bias_ref[g]
                y_ref[tile_rows, g * CHUNK:(g + 1) * CHUNK] = (
                    u[rows, cols] * sv * _silu(gt[rows, cols])).astype(_BF16)

    starts = [i * half for i in range(A_ROW_SPLIT)]
    h1 = [normed(r0) for r0 in starts]
    stats = [layer_norm_stats(_dot(h, win_ref[:, A_WIDTH:2 * A_WIDTH])) for h in h1]
    for j in range(n_blocks):
        for r0, h, st in zip(starts, h1, stats):
            gated_block(r0, h, st, j)
    for r0 in starts:
        o_ref[r0:r0 + half, :] = x_ref[r0:r0 + half, :] + _dot(y_ref[r0:r0 + half, :], wout_ref[...])


def _layer_a(x2, norm_g, w_in, ln_g, ln_b, ws, bs, w_out):
    t = x2.shape[0]
    tm = A_TOKENS_PER_STEP
    const = lambda i: (0, 0)
    return pl.pallas_call(
        _layer_a_kernel,
        out_shape=jax.ShapeDtypeStruct((t, D_MODEL), _F32),
        grid=(t // tm,),
        in_specs=[
            pl.BlockSpec((tm, D_MODEL), lambda i: (i, 0)),
            pl.BlockSpec((1, D_MODEL), const),
            pl.BlockSpec(memory_space=pl.ANY),
            pl.BlockSpec((1, A_WIDTH), const),
            pl.BlockSpec((1, A_WIDTH), const),
            pl.BlockSpec((A_GROUPS, CHUNK, CHUNK), lambda i: (0, 0, 0)),
            pl.BlockSpec((A_GROUPS, CHUNK), const),
            pl.BlockSpec(memory_space=pl.ANY),
        ],
        out_specs=pl.BlockSpec((tm, D_MODEL), lambda i: (i, 0)),
        scratch_shapes=[
            pltpu.VMEM((tm, A_WIDTH), _BF16),
            pltpu.VMEM((D_MODEL, 3 * A_WIDTH), _BF16),
            pltpu.VMEM((A_WIDTH, D_MODEL), _BF16),
            pltpu.VMEM((A_GROUPS, CHUNK, CHUNK), _BF16),
            pltpu.VMEM((A_GROUPS, CHUNK, CHUNK), _F32),
        ],
        compiler_params=pltpu.CompilerParams(
            dimension_semantics=("arbitrary",), vmem_limit_bytes=VMEM_LIMIT_BYTES),
        name="gmlp_layer",
    )(x2, norm_g, w_in, ln_g, ln_b, ws, bs, w_out)


def _layer_b_kernel(sinks_ref, hc_ref, hp_ref, kvg_ref, bg_ref, wkv_hbm, bkv_ref, win_hbm, bq_ref,
                    wout_hbm, fg_ref, cos_ref, sin_ref,
                    o_ref, kt_ref, vx_ref, q_ref, gate_ref, y_ref, wkv_ref, win_ref, wout_ref,
                    *, tiles_per_seq):
    t = pl.program_id(0)

    @pl.when(t == 0)
    def _():
        _load_weight_as_bf16(wkv_hbm, wkv_ref, kvg_ref)
        _load_weight_as_bf16(win_hbm, win_ref, bg_ref)
        _load_weight_as_bf16(wout_hbm, wout_ref)

    refs = (sinks_ref, hc_ref, hp_ref, kvg_ref, bg_ref, wkv_ref, bkv_ref, win_ref, bq_ref,
            wout_ref, fg_ref, cos_ref, sin_ref, o_ref, kt_ref, vx_ref, q_ref, gate_ref, y_ref)

    @pl.when(t % 2 == 0)
    def _():
        _layer_b_step(*refs, t=t, slot=0, tiles_per_seq=tiles_per_seq)

    @pl.when(t % 2 == 1)
    def _():
        _layer_b_step(*refs, t=t, slot=1, tiles_per_seq=tiles_per_seq)


def _layer_b_step(sinks_ref, hc_ref, hp_ref, kvg_ref, bg_ref, wkv_ref, bkv_ref, win_ref, bq_ref,
                  wout_ref, fg_ref, cos_ref, sin_ref,
                  o_ref, kt_ref, vx_ref, q_ref, gate_ref, y_ref, *, t, slot, tiles_per_seq):
    tm = hc_ref.shape[0]
    prev = 1 - slot
    new = slice(CHUNK, CHUNK + tm)
    planes = 2 * N_KV_HEADS

    if slot == 0:
        @pl.when(t == 0)
        def _():
            kt_ref[prev] = jnp.zeros((planes, LANES, CHUNK + tm), _BF16)
            vx_ref[prev] = jnp.zeros((planes, CHUNK + tm, 2 * LANES), _BF16)
            q_ref[prev] = jnp.zeros((tm, B_WIDTH), _BF16)
            gate_ref[prev] = jnp.zeros((tm, B_WIDTH), _F32)

    first_in_seq = (t % tiles_per_seq) == 0

    @pl.when(first_in_seq)
    def _():
        kt_ref[slot, :, :, 0:CHUNK] = jnp.zeros((planes, LANES, CHUNK), _BF16)
        vx_ref[slot, :, 0:CHUNK, :] = jnp.zeros((planes, CHUNK, 2 * LANES), _BF16)

    @pl.when(jnp.logical_not(first_in_seq))
    def _():
        kt_ref[slot, :, :, 0:CHUNK] = kt_ref[prev, :, :, tm:tm + CHUNK]
        vx_ref[slot, :, 0:CHUNK, :] = vx_ref[prev, :, tm:tm + CHUNK, :]

    h = hc_ref[...]
    h2 = (h * lax.rsqrt(jnp.mean(h * h, axis=-1, keepdims=True) + EPS)).astype(_BF16)
    kvn = h2

    cos = cos_ref[...]
    sin = sin_ref[...]
    lane = lax.broadcasted_iota(jnp.int32, (tm, LANES), 1)
    first_half = (lane % HEAD_DIM) < (HEAD_DIM // 2)
    low_head = lane < HEAD_DIM

    def rope(x, cos_t, sin_t):
        swapped = jnp.where(first_half,
                            pltpu.roll(x, LANES - HEAD_DIM // 2, 1),
                            pltpu.roll(x, HEAD_DIM // 2, 1))
        return x * cos_t + swapped * sin_t

    def project_kv():
        kv = _dot(kvn, wkv_ref[...]) + bkv_ref[...]
        k_t = rope(kv[:, :KV_WIDTH], cos, sin).T
        zero_rows = jnp.zeros((HEAD_DIM, tm), _F32)
        for kvh in range(N_KV_HEADS):
            head_t = k_t[kvh * HEAD_DIM:(kvh + 1) * HEAD_DIM]
            kt_ref[slot, 2 * kvh, :, new] = jnp.concatenate([head_t, zero_rows], axis=0).astype(_BF16)
            kt_ref[slot, 2 * kvh + 1, :, new] = jnp.concatenate([zero_rows, head_t], axis=0).astype(_BF16)
        v = kv[:, KV_WIDTH:]
        v_sw = pltpu.roll(v, HEAD_DIM, 1)
        zero = jnp.zeros_like(v)
        one = jnp.ones_like(v)
        ones_lo = jnp.where(low_head, one, zero)
        ones_hi = jnp.where(low_head, zero, one)
        v_low = (jnp.where(low_head, v, zero), jnp.where(low_head, v_sw, zero))
        v_high = (jnp.where(low_head, zero, v_sw), jnp.where(low_head, zero, v))
        for kvh in range(N_KV_HEADS):
            vx_ref[slot, 2 * kvh, new, :] = jnp.concatenate([v_low[kvh], ones_lo], axis=1).astype(_BF16)
            vx_ref[slot, 2 * kvh + 1, new, :] = jnp.concatenate([v_high[kvh], ones_hi], axis=1).astype(_BF16)

    q_scale = (HEAD_DIM ** -0.5) * LOG2_E
    cos_q = cos * q_scale
    sin_q = sin * q_scale

    def project_q(b):
        cols = slice(b * P_BLOCK, (b + 1) * P_BLOCK)
        zq = _dot(h2, win_ref[:, cols]) + bq_ref[:, cols]
        for i in range(P_BLOCK // LANES):
            s = b * (P_BLOCK // LANES) + i
            q_ref[slot, :, s * LANES:(s + 1) * LANES] = rope(
                zq[:, i * LANES:(i + 1) * LANES], cos_q, sin_q).astype(_BF16)

    def project_gate(b):
        cols = slice(b * P_BLOCK, (b + 1) * P_BLOCK)
        gate_ref[slot, :, cols] = _dot(h2, win_ref[:, B_WIDTH + b * P_BLOCK:B_WIDTH + (b + 1) * P_BLOCK])

    qi = lax.broadcasted_iota(jnp.int32, (CHUNK, 2 * CHUNK), 0)
    kj = lax.broadcasted_iota(jnp.int32, (CHUNK, 2 * CHUNK), 1)
    in_window = (kj > qi) & (kj <= qi + CHUNK)
    neg_inf = jnp.float32(-jnp.inf)
    bias = jnp.where(in_window, 0.0, neg_inf)
    prev_first_in_seq = ((t + tiles_per_seq - 1) % tiles_per_seq) == 0
    bias_first = jnp.where(in_window & ((kj >= CHUNK) | jnp.logical_not(prev_first_in_seq)), 0.0, neg_inf)

    units = [(c, kvh) for kvh in range(N_KV_HEADS) for c in range(tm // CHUNK)]
    assert len(units) == 4 and B_WIDTH // P_BLOCK == 4, "the program order below is written for 4 + 4"

    def scores(unit):
        c, kvh = unit
        rows = slice(c * CHUNK, (c + 1) * CHUNK)
        band = slice(c * CHUNK, (c + 2) * CHUNK)
        q_stack = jnp.concatenate(
            [q_ref[prev, rows, s * LANES:(s + 1) * LANES]
             for s in range(kvh * SLABS_PER_KV, (kvh + 1) * SLABS_PER_KV)], axis=0)
        return [_dot(q_stack, kt_ref[prev, 2 * kvh + half, :, band]) for half in range(HEADS_PER_SLAB)]

    sink_slot = lax.broadcasted_iota(jnp.int32, (CHUNK, LANES), 1) == 0
    key_row = lax.broadcasted_iota(jnp.int32, (BF16_ROWS, 2 * LANES), 0)
    key_lane = lax.broadcasted_iota(jnp.int32, (BF16_ROWS, 2 * LANES), 1)

    def softmax(unit, sc_halves):
        c, kvh = unit
        chunk_bias = bias_first if c == 0 else bias
        probs = []
        for half, sc in enumerate(sc_halves):
            p_parts = []
            for si in range(SLABS_PER_KV):
                s = kvh * SLABS_PER_KV + si
                sink = sinks_ref[s * HEADS_PER_SLAB + half] * LOG2_E
                blk = sc[si * CHUNK:(si + 1) * CHUNK]
                sb = jnp.concatenate(
                    [jnp.where(sink_slot, sink, blk[:, :LANES] + chunk_bias[:, :LANES]),
                     blk[:, LANES:] + chunk_bias[:, LANES:]], axis=1)
                m = jnp.max(sb, axis=-1, keepdims=True)
                p_parts.append(jnp.exp2(sb - m).astype(_BF16))
            probs.append(jnp.concatenate(p_parts, axis=0))
        return probs

    def band_values(kvh, half, c):
        r0 = c * CHUNK
        ones_lo = LANES + half * HEAD_DIM
        sink_row = jnp.where((key_lane >= ones_lo) & (key_lane < ones_lo + HEAD_DIM), 1.0, 0.0)
        head = vx_ref[prev, 2 * kvh + half, r0:r0 + BF16_ROWS, :].astype(_F32)
        head = jnp.where(key_row == 0, sink_row, head).astype(_BF16)
        return jnp.concatenate(
            [head, vx_ref[prev, 2 * kvh + half, r0 + BF16_ROWS:r0 + 2 * CHUNK, :]], axis=0)

    def attend(unit, probs):
        c, kvh = unit
        rows = slice(c * CHUNK, (c + 1) * CHUNK)
        od = _dot(probs[0], band_values(kvh, 0, c)) + _dot(probs[1], band_values(kvh, 1, c))
        for si in range(SLABS_PER_KV):
            s = kvh * SLABS_PER_KV + si
            blk = slice(si * CHUNK, (si + 1) * CHUNK)
            gate = gate_ref[prev, rows, s * LANES:(s + 1) * LANES]
            y_ref[rows, s * LANES:(s + 1) * LANES] = (
                od[blk, :LANES] * _silu(gate) / od[blk, LANES:]).astype(_BF16)

    def out_part(kvh):
        cols = slice(kvh * SLABS_PER_KV * LANES, (kvh + 1) * SLABS_PER_KV * LANES)
        return _dot(y_ref[:, cols], wout_ref[cols, :])

    s0 = scores(units[0])
    project_kv()
    s1 = scores(units[1])
    project_q(0)
    p0 = softmax(units[0], s0)
    attend(units[0], p0)
    s2 = scores(units[2])
    project_q(1)
    p1 = softmax(units[1], s1)
    attend(units[1], p1)
    s3 = scores(units[3])
    project_q(2)
    p2 = softmax(units[2], s2)
    attend(units[2], p2)
    project_q(3)
    project_gate(0)
    acc = out_part(0)
    p3 = softmax(units[3], s3)
    attend(units[3], p3)
    project_gate(1)
    out = hp_ref[...] + acc + out_part(1)
    o_ref[...] = out * lax.rsqrt(jnp.mean(out * out, axis=-1, keepdims=True) + EPS) * fg_ref[...]
    project_gate(2)
    project_gate(3)


def _layer_b(h2d, seq, sinks, kv_g, b_g, w_kv, b_kv, w_in, b_q, w_out, f_g, cos, sin):
    tm = B_TOKENS_PER_STEP
    n_tiles = h2d.shape[0] // tm
    tiles_per_seq = seq // tm
    const = lambda t: (0, 0)
    cur = lambda t: (jnp.minimum(t, n_tiles - 1), 0)
    prv = lambda t: (jnp.maximum(t - 1, 0), 0)
    tab = lambda t: (jnp.minimum(t, n_tiles - 1) % tiles_per_seq, 0)
    planes = 2 * N_KV_HEADS
    return pl.pallas_call(
        functools.partial(_layer_b_kernel, tiles_per_seq=tiles_per_seq),
        out_shape=jax.ShapeDtypeStruct(h2d.shape, _F32),
        grid=(n_tiles + 1,),
        in_specs=[
            pl.BlockSpec(memory_space=pltpu.SMEM),
            pl.BlockSpec((tm, D_MODEL), cur),
            pl.BlockSpec((tm, D_MODEL), prv),
            pl.BlockSpec((1, D_MODEL), const),
            pl.BlockSpec((1, D_MODEL), const),
            pl.BlockSpec(memory_space=pl.ANY),
            pl.BlockSpec((1, 2 * KV_WIDTH), const),
            pl.BlockSpec(memory_space=pl.ANY),
            pl.BlockSpec((1, B_WIDTH), const),
            pl.BlockSpec(memory_space=pl.ANY),
            pl.BlockSpec((1, D_MODEL), const),
            pl.BlockSpec((tm, LANES), tab),
            pl.BlockSpec((tm, LANES), tab),
        ],
        out_specs=pl.BlockSpec((tm, D_MODEL), prv),
        scratch_shapes=[
            pltpu.VMEM((2, planes, LANES, CHUNK + tm), _BF16),
            pltpu.VMEM((2, planes, CHUNK + tm, 2 * LANES), _BF16),
            pltpu.VMEM((2, tm, B_WIDTH), _BF16),
            pltpu.VMEM((2, tm, B_WIDTH), _F32),
            pltpu.VMEM((tm, B_WIDTH), _BF16),
            pltpu.VMEM((D_MODEL, 2 * KV_WIDTH), _BF16),
            pltpu.VMEM((D_MODEL, 2 * B_WIDTH), _BF16),
            pltpu.VMEM((B_WIDTH, D_MODEL), _BF16),
        ],
        compiler_params=pltpu.CompilerParams(
            dimension_semantics=("arbitrary",), vmem_limit_bytes=VMEM_LIMIT_BYTES),
        name="swa_layer",
    )(sinks, h2d, h2d, kv_g, b_g, w_kv, b_kv, w_in, b_q, w_out, f_g, cos, sin)


def _rope_tables(seq):
    inv_freq = ROPE_THETA ** (-np.arange(0, HEAD_DIM, 2, dtype=np.float64) / HEAD_DIM)
    ang = np.arange(seq, dtype=np.float64)[:, None] * inv_freq[None, :]
    cos = np.cos(ang)
    sin = np.sin(ang)
    cos_slab = np.tile(np.concatenate([cos, cos], axis=-1), (1, HEADS_PER_SLAB))
    sin_slab = np.tile(np.concatenate([-sin, sin], axis=-1), (1, HEADS_PER_SLAB))
    assert cos_slab.shape == (seq, LANES)
    return jnp.asarray(cos_slab, dtype=_F32), jnp.asarray(sin_slab, dtype=_F32)


def kernel(x, a_norm_g, a_w_in, a_ln_g, a_ln_b, a_ws, a_bs, a_w_out, kv_norm_g, w_kv, b_kv,
           b_norm_g, b_w_in, b_bq, b_sinks, b_w_out, final_norm_g):
    batch, seq, d = x.shape
    assert d == D_MODEL
    for tile in (A_TOKENS_PER_STEP, B_TOKENS_PER_STEP):
        assert seq % tile == 0 and tile % CHUNK == 0
    assert a_w_in.shape[0] == 1 and b_w_in.shape[0] == 1
    row = lambda p: p.reshape(1, -1)

    x2 = x.reshape(batch * seq, d)
    h = _layer_a(x2, row(a_norm_g[0]), a_w_in[0], row(a_ln_g[0]), row(a_ln_b[0]),
                 a_ws[0], a_bs[0], a_w_out[0])
    cos, sin = _rope_tables(seq)
    out = _layer_b(h, seq, b_sinks[0], row(kv_norm_g), row(b_norm_g[0]),
                   w_kv, row(b_kv), b_w_in[0], row(b_bq[0]),
                   b_w_out[0], row(final_norm_g), cos, sin)
    return out.reshape(batch, seq, d)
```

```python
import functools

import jax
import jax.numpy as jnp
import numpy as np
from jax import lax
from jax.experimental import pallas as pl
from jax.experimental.pallas import tpu as pltpu

D_MODEL = 1024
CHUNK = 128
A_WIDTH = 2048
A_GROUPS = 16
HEAD_DIM = 64
N_Q_HEADS = 16
N_KV_HEADS = 2
B_WIDTH = 1024
KV_WIDTH = 128
ROPE_THETA = 10000.0
EPS = 1e-5
LOG2_E = 1.4426950408889634

LANES = 128
BF16_ROWS = 16
HEADS_PER_SLAB = LANES // HEAD_DIM
N_SLABS = B_WIDTH // LANES
SLABS_PER_KV = N_SLABS // N_KV_HEADS

A_TOKENS_PER_STEP = 1024
B_TOKENS_PER_STEP = 256
P_BLOCK = 256
A_COL_BLOCK = 512
VMEM_LIMIT_BYTES = 56 * 1024 * 1024
WEIGHT_STAGE_BYTES = 2 * 1024 * 1024

_F32 = jnp.float32
_BF16 = jnp.bfloat16


def _dot(a, b):
    return jnp.dot(a, b, preferred_element_type=_F32)


def _silu(g):
    half_g = 0.5 * g
    return half_g * (1.0 + jnp.tanh(half_g))


def _load_weight_as_bf16(w_hbm, w_ref, row_scale_ref=None):
    k, n = w_hbm.shape
    rows = min(k, pl.next_power_of_2(WEIGHT_STAGE_BYTES // (4 * n) + 1) // 2)
    assert k % rows == 0 and rows % 16 == 0
    n_chunks = k // rows
    if row_scale_ref is not None:
        scale_col = jnp.broadcast_to(row_scale_ref[...], (LANES, k)).T

    def body(stage, sem):
        def copy(i):
            return pltpu.make_async_copy(w_hbm.at[pl.ds(i * rows, rows), :], stage.at[i % 2], sem.at[i % 2])

        copy(0).start()
        for i in range(n_chunks):
            if i + 1 < n_chunks:
                copy(i + 1).start()
            copy(i).wait()
            chunk = stage[i % 2]
            if row_scale_ref is not None:
                chunk = chunk * scale_col[i * rows:(i + 1) * rows, 0:1]
            w_ref[i * rows:(i + 1) * rows, :] = chunk.astype(_BF16)

    pl.run_scoped(body, pltpu.VMEM((2, rows, n), _F32), pltpu.SemaphoreType.DMA((2,)))


def _layer_a_kernel(x_ref, ng_ref, win_hbm, lng_ref, lnb_ref, ws_ref, bs_ref, wout_hbm,
                    o_ref, y_ref, win_ref, wout_ref, wsm_ref, bias_ref):
    @pl.when(pl.program_id(0) == 0)
    def _():
        _load_weight_as_bf16(win_hbm, win_ref)
        _load_weight_as_bf16(wout_hbm, wout_ref)
        row = lax.broadcasted_iota(jnp.int32, (CHUNK, CHUNK), 0)
        col = lax.broadcasted_iota(jnp.int32, (CHUNK, CHUNK), 1)
        for g in range(A_GROUPS):
            wsm_ref[g] = jnp.where(col <= row, ws_ref[g], 0.0).astype(_BF16)
            bias_ref[g] = jnp.broadcast_to(bs_ref[g:g + 1, :], (CHUNK, CHUNK)).T

    tm = x_ref.shape[0]
    groups_per_block = A_COL_BLOCK // CHUNK

    x = x_ref[...]
    h1 = (x * lax.rsqrt(jnp.mean(x * x, axis=-1, keepdims=True) + EPS) * ng_ref[...]).astype(_BF16)

    v = _dot(h1, win_ref[:, A_WIDTH:2 * A_WIDTH])
    d = v - jnp.mean(v[:, :LANES], axis=-1, keepdims=True)
    m1 = jnp.mean(d, axis=-1, keepdims=True)
    rstd = lax.rsqrt(jnp.mean(d * d, axis=-1, keepdims=True) - m1 * m1 + EPS)

    for j in range(A_WIDTH // A_COL_BLOCK):
        c0 = j * A_COL_BLOCK
        u = _dot(h1, win_ref[:, c0:c0 + A_COL_BLOCK])
        gt = _dot(h1, win_ref[:, 2 * A_WIDTH + c0:2 * A_WIDTH + c0 + A_COL_BLOCK])
        for gi in range(groups_per_block):
            g = j * groups_per_block + gi
            gcols = slice(g * CHUNK, (g + 1) * CHUNK)
            cols = slice(gi * CHUNK, (gi + 1) * CHUNK)
            vn = ((d[:, gcols] - m1) * rstd * lng_ref[:, gcols] + lnb_ref[:, gcols]).astype(_BF16)
            for c in range(tm // CHUNK):
                rows = slice(c * CHUNK, (c + 1) * CHUNK)
                sv = _dot(wsm_ref[g], vn[rows]) + bias_ref[g]
                y_ref[rows, gcols] = (u[rows, cols] * sv * _silu(gt[rows, cols])).astype(_BF16)

    o_ref[...] = x + _dot(y_ref[...], wout_ref[...])


def _layer_a(x2, norm_g, w_in, ln_g, ln_b, ws, bs, w_out):
    t = x2.shape[0]
    tm = A_TOKENS_PER_STEP
    const = lambda i: (0, 0)
    return pl.pallas_call(
        _layer_a_kernel,
        out_shape=jax.ShapeDtypeStruct((t, D_MODEL), _F32),
        grid=(t // tm,),
        in_specs=[
            pl.BlockSpec((tm, D_MODEL), lambda i: (i, 0)),
            pl.BlockSpec((1, D_MODEL), const),
            pl.BlockSpec(memory_space=pl.ANY),
            pl.BlockSpec((1, A_WIDTH), const),
            pl.BlockSpec((1, A_WIDTH), const),
            pl.BlockSpec((A_GROUPS, CHUNK, CHUNK), lambda i: (0, 0, 0)),
            pl.BlockSpec((A_GROUPS, CHUNK), const),
            pl.BlockSpec(memory_space=pl.ANY),
        ],
        out_specs=pl.BlockSpec((tm, D_MODEL), lambda i: (i, 0)),
        scratch_shapes=[
            pltpu.VMEM((tm, A_WIDTH), _BF16),
            pltpu.VMEM((D_MODEL, 3 * A_WIDTH), _BF16),
            pltpu.VMEM((A_WIDTH, D_MODEL), _BF16),
            pltpu.VMEM((A_GROUPS, CHUNK, CHUNK), _BF16),
            pltpu.VMEM((A_GROUPS, CHUNK, CHUNK), _F32),
        ],
        compiler_params=pltpu.CompilerParams(
            dimension_semantics=("arbitrary",), vmem_limit_bytes=VMEM_LIMIT_BYTES),
        name="gmlp_layer",
    )(x2, norm_g, w_in, ln_g, ln_b, ws, bs, w_out)


def _layer_b_kernel(sinks_ref, hc_ref, hp_ref, kvg_ref, bg_ref, wkv_hbm, bkv_ref, win_hbm, bq_ref,
                    wout_hbm, fg_ref, cos_ref, sin_ref,
                    o_ref, kt_ref, vx_ref, q_ref, gate_ref, y_ref, wkv_ref, win_ref, wout_ref,
                    *, tiles_per_seq):
    t = pl.program_id(0)

    @pl.when(t == 0)
    def _():
        _load_weight_as_bf16(wkv_hbm, wkv_ref, kvg_ref)
        _load_weight_as_bf16(win_hbm, win_ref, bg_ref)
        _load_weight_as_bf16(wout_hbm, wout_ref)

    refs = (sinks_ref, hc_ref, hp_ref, wkv_ref, bkv_ref, win_ref, bq_ref,
            wout_ref, fg_ref, cos_ref, sin_ref, o_ref, kt_ref, vx_ref, q_ref, gate_ref, y_ref)

    @pl.when(t % 2 == 0)
    def _():
        _layer_b_step(*refs, t=t, slot=0, tiles_per_seq=tiles_per_seq)

    @pl.when(t % 2 == 1)
    def _():
        _layer_b_step(*refs, t=t, slot=1, tiles_per_seq=tiles_per_seq)


def _layer_b_step(sinks_ref, hc_ref, hp_ref, wkv_ref, bkv_ref, win_ref, bq_ref,
                  wout_ref, fg_ref, cos_ref, sin_ref,
                  o_ref, kt_ref, vx_ref, q_ref, gate_ref, y_ref, *, t, slot, tiles_per_seq):
    tm = hc_ref.shape[0]
    prev = 1 - slot
    new = slice(CHUNK, CHUNK + tm)
    planes = 2 * N_KV_HEADS

    if slot == 0:
        @pl.when(t == 0)
        def _():
            kt_ref[prev] = jnp.zeros((planes, LANES, CHUNK + tm), _BF16)
            vx_ref[prev] = jnp.zeros((planes, CHUNK + tm, 2 * LANES), _BF16)
            q_ref[prev] = jnp.zeros((tm, B_WIDTH), _BF16)
            gate_ref[prev] = jnp.zeros((tm, B_WIDTH), _F32)

    first_in_seq = (t % tiles_per_seq) == 0

    @pl.when(first_in_seq)
    def _():
        kt_ref[slot, :, :, 0:CHUNK] = jnp.zeros((planes, LANES, CHUNK), _BF16)
        vx_ref[slot, :, 0:CHUNK, :] = jnp.zeros((planes, CHUNK, 2 * LANES), _BF16)

    @pl.when(jnp.logical_not(first_in_seq))
    def _():
        kt_ref[slot, :, :, 0:CHUNK] = kt_ref[prev, :, :, tm:tm + CHUNK]
        vx_ref[slot, :, 0:CHUNK, :] = vx_ref[prev, :, tm:tm + CHUNK, :]

    h = hc_ref[...]
    h2 = (h * lax.rsqrt(jnp.mean(h * h, axis=-1, keepdims=True) + EPS)).astype(_BF16)

    cos = cos_ref[...]
    sin = sin_ref[...]
    lane = lax.broadcasted_iota(jnp.int32, (tm, LANES), 1)
    first_half = (lane % HEAD_DIM) < (HEAD_DIM // 2)
    low_head = lane < HEAD_DIM

    def rope(x, cos_t, sin_t):
        swapped = jnp.where(first_half,
                            pltpu.roll(x, LANES - HEAD_DIM // 2, 1),
                            pltpu.roll(x, HEAD_DIM // 2, 1))
        return x * cos_t + swapped * sin_t

    def project_kv():
        kv = _dot(h2, wkv_ref[...]) + bkv_ref[...]
        k_t = rope(kv[:, :KV_WIDTH], cos, sin).T
        zero_rows = jnp.zeros((HEAD_DIM, tm), _F32)
        for kvh in range(N_KV_HEADS):
            head_t = k_t[kvh * HEAD_DIM:(kvh + 1) * HEAD_DIM]
            kt_ref[slot, 2 * kvh, :, new] = jnp.concatenate([head_t, zero_rows], axis=0).astype(_BF16)
            kt_ref[slot, 2 * kvh + 1, :, new] = jnp.concatenate([zero_rows, head_t], axis=0).astype(_BF16)
        v = kv[:, KV_WIDTH:]
        v_sw = pltpu.roll(v, HEAD_DIM, 1)
        zero = jnp.zeros_like(v)
        one = jnp.ones_like(v)
        ones_lo = jnp.where(low_head, one, zero)
        ones_hi = jnp.where(low_head, zero, one)
        v_low = (jnp.where(low_head, v, zero), jnp.where(low_head, v_sw, zero))
        v_high = (jnp.where(low_head, zero, v_sw), jnp.where(low_head, zero, v))
        for kvh in range(N_KV_HEADS):
            vx_ref[slot, 2 * kvh, new, :] = jnp.concatenate([v_low[kvh], ones_lo], axis=1).astype(_BF16)
            vx_ref[slot, 2 * kvh + 1, new, :] = jnp.concatenate([v_high[kvh], ones_hi], axis=1).astype(_BF16)

    q_scale = (HEAD_DIM ** -0.5) * LOG2_E
    cos_q = cos * q_scale
    sin_q = sin * q_scale

    def project_q(b):
        cols = slice(b * P_BLOCK, (b + 1) * P_BLOCK)
        zq = _dot(h2, win_ref[:, cols]) + bq_ref[:, cols]
        for i in range(P_BLOCK // LANES):
            s = b * (P_BLOCK // LANES) + i
            q_ref[slot, :, s * LANES:(s + 1) * LANES] = rope(
                zq[:, i * LANES:(i + 1) * LANES], cos_q, sin_q).astype(_BF16)

    def project_gate(b):
        cols = slice(b * P_BLOCK, (b + 1) * P_BLOCK)
        gate_ref[slot, :, cols] = _dot(h2, win_ref[:, B_WIDTH + b * P_BLOCK:B_WIDTH + (b + 1) * P_BLOCK])

    qi = lax.broadcasted_iota(jnp.int32, (CHUNK, 2 * CHUNK), 0)
    kj = lax.broadcasted_iota(jnp.int32, (CHUNK, 2 * CHUNK), 1)
    in_window = (kj > qi) & (kj <= qi + CHUNK)
    neg_inf = jnp.float32(-jnp.inf)
    bias = jnp.where(in_window, 0.0, neg_inf)
    prev_first_in_seq = ((t + tiles_per_seq - 1) % tiles_per_seq) == 0
    bias_first = jnp.where(in_window & ((kj >= CHUNK) | jnp.logical_not(prev_first_in_seq)), 0.0, neg_inf)

    units = [(c, kvh) for kvh in range(N_KV_HEADS) for c in range(tm // CHUNK)]
    assert len(units) == 4 and B_WIDTH // P_BLOCK == 4, "the program order below is written for 4 + 4"

    def scores(unit):
        c, kvh = unit
        rows = slice(c * CHUNK, (c + 1) * CHUNK)
        band = slice(c * CHUNK, (c + 2) * CHUNK)
        q_stack = jnp.concatenate(
            [q_ref[prev, rows, s * LANES:(s + 1) * LANES]
             for s in range(kvh * SLABS_PER_KV, (kvh + 1) * SLABS_PER_KV)], axis=0)
        return [_dot(q_stack, kt_ref[prev, 2 * kvh + half, :, band]) for half in range(HEADS_PER_SLAB)]

    sink_slot = lax.broadcasted_iota(jnp.int32, (CHUNK, LANES), 1) == 0
    key_row = lax.broadcasted_iota(jnp.int32, (BF16_ROWS, 2 * LANES), 0)
    key_lane = lax.broadcasted_iota(jnp.int32, (BF16_ROWS, 2 * LANES), 1)

    def softmax(unit, sc_halves):
        c, kvh = unit
        chunk_bias = bias_first if c == 0 else bias
        probs = []
        for half, sc in enumerate(sc_halves):
            p_parts = []
            for si in range(SLABS_PER_KV):
                s = kvh * SLABS_PER_KV + si
                sink = sinks_ref[s * HEADS_PER_SLAB + half] * LOG2_E
                blk = sc[si * CHUNK:(si + 1) * CHUNK]
                sb = jnp.concatenate(
                    [jnp.where(sink_slot, sink, blk[:, :LANES] + chunk_bias[:, :LANES]),
                     blk[:, LANES:] + chunk_bias[:, LANES:]], axis=1)
                m = jnp.max(sb, axis=-1, keepdims=True)
                p_parts.append(jnp.exp2(sb - m).astype(_BF16))
            probs.append(jnp.concatenate(p_parts, axis=0))
        return probs

    def band_values(kvh, half, c):
        r0 = c * CHUNK
        ones_lo = LANES + half * HEAD_DIM
        sink_row = jnp.where((key_lane >= ones_lo) & (key_lane < ones_lo + HEAD_DIM), 1.0, 0.0)
        head = vx_ref[prev, 2 * kvh + half, r0:r0 + BF16_ROWS, :].astype(_F32)
        head = jnp.where(key_row == 0, sink_row, head).astype(_BF16)
        return jnp.concatenate(
            [head, vx_ref[prev, 2 * kvh + half, r0 + BF16_ROWS:r0 + 2 * CHUNK, :]], axis=0)

    def attend(unit, probs):
        c, kvh = unit
        rows = slice(c * CHUNK, (c + 1) * CHUNK)
        od = _dot(probs[0], band_values(kvh, 0, c)) + _dot(probs[1], band_values(kvh, 1, c))
        for si in range(SLABS_PER_KV):
            s = kvh * SLABS_PER_KV + si
            blk = slice(si * CHUNK, (si + 1) * CHUNK)
            gate = gate_ref[prev, rows, s * LANES:(s + 1) * LANES]
            y_ref[rows, s * LANES:(s + 1) * LANES] = (
                od[blk, :LANES] * _silu(gate) / od[blk, LANES:]).astype(_BF16)

    def out_part(kvh):
        cols = slice(kvh * SLABS_PER_KV * LANES, (kvh + 1) * SLABS_PER_KV * LANES)
        return _dot(y_ref[:, cols], wout_ref[cols, :])

    s0 = scores(units[0])
    project_kv()
    s1 = scores(units[1])
    project_q(0)
    p0 = softmax(units[0], s0)
    attend(units[0], p0)
    s2 = scores(units[2])
    project_q(1)
    p1 = softmax(units[1], s1)
    attend(units[1], p1)
    s3 = scores(units[3])
    project_q(2)
    p2 = softmax(units[2], s2)
    attend(units[2], p2)
    project_q(3)
    project_gate(0)
    acc = out_part(0)
    p3 = softmax(units[3], s3)
    attend(units[3], p3)
    project_gate(1)
    out = hp_ref[...] + acc + out_part(1)
    o_ref[...] = out * lax.rsqrt(jnp.mean(out * out, axis=-1, keepdims=True) + EPS) * fg_ref[...]
    project_gate(2)
    project_gate(3)


def _layer_b(h2d, seq, sinks, kv_g, b_g, w_kv, b_kv, w_in, b_q, w_out, f_g, cos, sin):
    tm = B_TOKENS_PER_STEP
    n_tiles = h2d.shape[0] // tm
    tiles_per_seq = seq // tm
    const = lambda t: (0, 0)
    cur = lambda t: (jnp.minimum(t, n_tiles - 1), 0)
    prv = lambda t: (jnp.maximum(t - 1, 0), 0)
    tab = lambda t: (jnp.minimum(t, n_tiles - 1) % tiles_per_seq, 0)
    planes = 2 * N_KV_HEADS
    return pl.pallas_call(
        functools.partial(_layer_b_kernel, tiles_per_seq=tiles_per_seq),
        out_shape=jax.ShapeDtypeStruct(h2d.shape, _F32),
        grid=(n_tiles + 1,),
        in_specs=[
            pl.BlockSpec(memory_space=pltpu.SMEM),
            pl.BlockSpec((tm, D_MODEL), cur),
            pl.BlockSpec((tm, D_MODEL), prv),
            pl.BlockSpec((1, D_MODEL), const),
            pl.BlockSpec((1, D_MODEL), const),
            pl.BlockSpec(memory_space=pl.ANY),
            pl.BlockSpec((1, 2 * KV_WIDTH), const),
            pl.BlockSpec(memory_space=pl.ANY),
            pl.BlockSpec((1, B_WIDTH), const),
            pl.BlockSpec(memory_space=pl.ANY),
            pl.BlockSpec((1, D_MODEL), const),
            pl.BlockSpec((tm, LANES), tab),
            pl.BlockSpec((tm, LANES), tab),
        ],
        out_specs=pl.BlockSpec((tm, D_MODEL), prv),
        scratch_shapes=[
            pltpu.VMEM((2, planes, LANES, CHUNK + tm), _BF16),
            pltpu.VMEM((2, planes, CHUNK + tm, 2 * LANES), _BF16),
            pltpu.VMEM((2, tm, B_WIDTH), _BF16),
            pltpu.VMEM((2, tm, B_WIDTH), _F32),
            pltpu.VMEM((tm, B_WIDTH), _BF16),
            pltpu.VMEM((D_MODEL, 2 * KV_WIDTH), _BF16),
            pltpu.VMEM((D_MODEL, 2 * B_WIDTH), _BF16),
            pltpu.VMEM((B_WIDTH, D_MODEL), _BF16),
        ],
        compiler_params=pltpu.CompilerParams(
            dimension_semantics=("arbitrary",), vmem_limit_bytes=VMEM_LIMIT_BYTES),
        name="swa_layer",
    )(sinks, h2d, h2d, kv_g, b_g, w_kv, b_kv, w_in, b_q, w_out, f_g, cos, sin)


def _rope_tables(seq):
    inv_freq = ROPE_THETA ** (-np.arange(0, HEAD_DIM, 2, dtype=np.float64) / HEAD_DIM)
    ang = np.arange(seq, dtype=np.float64)[:, None] * inv_freq[None, :]
    cos = np.cos(ang)
    sin = np.sin(ang)
    cos_slab = np.tile(np.concatenate([cos, cos], axis=-1), (1, HEADS_PER_SLAB))
    sin_slab = np.tile(np.concatenate([-sin, sin], axis=-1), (1, HEADS_PER_SLAB))
    assert cos_slab.shape == (seq, LANES)
    return jnp.asarray(cos_slab, dtype=_F32), jnp.asarray(sin_slab, dtype=_F32)


def kernel(x, a_norm_g, a_w_in, a_ln_g, a_ln_b, a_ws, a_bs, a_w_out, kv_norm_g, w_kv, b_kv,
           b_norm_g, b_w_in, b_bq, b_sinks, b_w_out, final_norm_g):
    batch, seq, d = x.shape
    assert d == D_MODEL
    for tile in (A_TOKENS_PER_STEP, B_TOKENS_PER_STEP):
        assert seq % tile == 0 and tile % CHUNK == 0
    assert a_w_in.shape[0] == 1 and b_w_in.shape[0] == 1
    row = lambda p: p.reshape(1, -1)

    x2 = x.reshape(batch * seq, d)
    h = _layer_a(x2, row(a_norm_g[0]), a_w_in[0], row(a_ln_g[0]), row(a_ln_b[0]),
                 a_ws[0], a_bs[0], a_w_out[0])
    cos, sin = _rope_tables(seq)
    out = _layer_b(h, seq, b_sinks[0], row(kv_norm_g), row(b_norm_g[0]),
                   w_kv, row(b_kv), b_w_in[0], row(b_bq[0]),
                   b_w_out[0], row(final_norm_g), cos, sin)
    return out.reshape(batch, seq, d)
```

```python
import functools

import jax
import jax.numpy as jnp
import numpy as np
from jax import lax
from jax.experimental import pallas as pl
from jax.experimental.pallas import tpu as pltpu

D_MODEL = 1024
CHUNK = 128
A_WIDTH = 2048
A_GROUPS = 16
HEAD_DIM = 64
N_Q_HEADS = 16
N_KV_HEADS = 2
B_WIDTH = 1024
KV_WIDTH = 128
ROPE_THETA = 10000.0
EPS = 1e-5
LOG2_E = 1.4426950408889634

LANES = 128
BF16_ROWS = 16
HEADS_PER_SLAB = LANES // HEAD_DIM
N_SLABS = B_WIDTH // LANES
SLABS_PER_KV = N_SLABS // N_KV_HEADS

A_TOKENS_PER_STEP = 1024
B_TOKENS_PER_STEP = 256
P_BLOCK = 256
A_COL_BLOCK = 512
VMEM_LIMIT_BYTES = 56 * 1024 * 1024
WEIGHT_STAGE_BYTES = 2 * 1024 * 1024

_F32 = jnp.float32
_BF16 = jnp.bfloat16


def _dot(a, b):
    return jnp.dot(a, b, preferred_element_type=_F32)


def _silu(g):
    half_g = 0.5 * g
    return half_g * (1.0 + jnp.tanh(half_g))


def _load_weights_as_bf16(weights, slots):
    plans = []
    for w_hbm, w_ref, row_scale_ref in weights:
        k, n = w_hbm.shape
        rows = min(k, pl.next_power_of_2(WEIGHT_STAGE_BYTES // (4 * n) + 1) // 2)
        assert k % rows == 0 and rows % 16 == 0
        n_chunks = k // rows
        n_slots = n_chunks if slots is None else min(slots, n_chunks)
        scale_col = None
        if row_scale_ref is not None:
            scale_col = jnp.broadcast_to(row_scale_ref[...], (LANES, k)).T
        plans.append((w_hbm, w_ref, scale_col, rows, n_chunks, n_slots))

    def body(*scratch):
        stages, sems = scratch[0::2], scratch[1::2]

        def copy(j, i):
            w_hbm, _, _, rows, _, n_slots = plans[j]
            return pltpu.make_async_copy(
                w_hbm.at[pl.ds(i * rows, rows), :], stages[j].at[i % n_slots], sems[j].at[i % n_slots])

        for j, plan in enumerate(plans):
            for i in range(plan[5]):
                copy(j, i).start()
        for j, (_, w_ref, scale_col, rows, n_chunks, n_slots) in enumerate(plans):
            for i in range(n_chunks):
                copy(j, i).wait()
                chunk = stages[j][i % n_slots]
                if scale_col is not None:
                    chunk = chunk * scale_col[i * rows:(i + 1) * rows, 0:1]
                w_ref[i * rows:(i + 1) * rows, :] = chunk.astype(_BF16)
                if i + n_slots < n_chunks:
                    copy(j, i + n_slots).start()

    scratch_types = []
    for w_hbm, _, _, rows, _, n_slots in plans:
        scratch_types += [pltpu.VMEM((n_slots, rows, w_hbm.shape[1]), _F32), pltpu.SemaphoreType.DMA((n_slots,))]
    pl.run_scoped(body, *scratch_types)


def _layer_a_kernel(x_ref, ng_ref, win_hbm, lng_ref, lnb_ref, ws_ref, bs_ref, wout_hbm,
                    o_ref, y_ref, win_ref, wout_ref, wsm_ref, bias_ref):
    @pl.when(pl.program_id(0) == 0)
    def _():
        _load_weights_as_bf16([(win_hbm, win_ref, None)], slots=2)
        _load_weights_as_bf16([(wout_hbm, wout_ref, None)], slots=2)
        row = lax.broadcasted_iota(jnp.int32, (CHUNK, CHUNK), 0)
        col = lax.broadcasted_iota(jnp.int32, (CHUNK, CHUNK), 1)
        for g in range(A_GROUPS):
            wsm_ref[g] = jnp.where(col <= row, ws_ref[g], 0.0).astype(_BF16)
            bias_ref[g] = jnp.broadcast_to(bs_ref[g:g + 1, :], (CHUNK, CHUNK)).T

    tm = x_ref.shape[0]
    groups_per_block = A_COL_BLOCK // CHUNK

    x = x_ref[...]
    h1 = (x * lax.rsqrt(jnp.mean(x * x, axis=-1, keepdims=True) + EPS) * ng_ref[...]).astype(_BF16)

    v = _dot(h1, win_ref[:, A_WIDTH:2 * A_WIDTH])
    d = v - jnp.mean(v[:, :LANES], axis=-1, keepdims=True)
    m1 = jnp.mean(d, axis=-1, keepdims=True)
    rstd = lax.rsqrt(jnp.mean(d * d, axis=-1, keepdims=True) - m1 * m1 + EPS)

    for j in range(A_WIDTH // A_COL_BLOCK):
        c0 = j * A_COL_BLOCK
        u = _dot(h1, win_ref[:, c0:c0 + A_COL_BLOCK])
        gt = _dot(h1, win_ref[:, 2 * A_WIDTH + c0:2 * A_WIDTH + c0 + A_COL_BLOCK])
        for gi in range(groups_per_block):
            g = j * groups_per_block + gi
            gcols = slice(g * CHUNK, (g + 1) * CHUNK)
            cols = slice(gi * CHUNK, (gi + 1) * CHUNK)
            vn = ((d[:, gcols] - m1) * rstd * lng_ref[:, gcols] + lnb_ref[:, gcols]).astype(_BF16)
            for c in range(tm // CHUNK):
                rows = slice(c * CHUNK, (c + 1) * CHUNK)
                sv = _dot(wsm_ref[g], vn[rows]) + bias_ref[g]
                y_ref[rows, gcols] = (u[rows, cols] * sv * _silu(gt[rows, cols])).astype(_BF16)

    o_ref[...] = x + _dot(y_ref[...], wout_ref[...])


def _layer_a(x2, norm_g, w_in, ln_g, ln_b, ws, bs, w_out):
    t = x2.shape[0]
    tm = A_TOKENS_PER_STEP
    const = lambda i: (0, 0)
    return pl.pallas_call(
        _layer_a_kernel,
        out_shape=jax.ShapeDtypeStruct((t, D_MODEL), _F32),
        grid=(t // tm,),
        in_specs=[
            pl.BlockSpec((tm, D_MODEL), lambda i: (i, 0)),
            pl.BlockSpec((1, D_MODEL), const),
            pl.BlockSpec(memory_space=pl.ANY),
            pl.BlockSpec((1, A_WIDTH), const),
            pl.BlockSpec((1, A_WIDTH), const),
            pl.BlockSpec((A_GROUPS, CHUNK, CHUNK), lambda i: (0, 0, 0)),
            pl.BlockSpec((A_GROUPS, CHUNK), const),
            pl.BlockSpec(memory_space=pl.ANY),
        ],
        out_specs=pl.BlockSpec((tm, D_MODEL), lambda i: (i, 0)),
        scratch_shapes=[
            pltpu.VMEM((tm, A_WIDTH), _BF16),
            pltpu.VMEM((D_MODEL, 3 * A_WIDTH), _BF16),
            pltpu.VMEM((A_WIDTH, D_MODEL), _BF16),
            pltpu.VMEM((A_GROUPS, CHUNK, CHUNK), _BF16),
            pltpu.VMEM((A_GROUPS, CHUNK, CHUNK), _F32),
        ],
        compiler_params=pltpu.CompilerParams(
            dimension_semantics=("arbitrary",), vmem_limit_bytes=VMEM_LIMIT_BYTES),
        name="gmlp_layer",
    )(x2, norm_g, w_in, ln_g, ln_b, ws, bs, w_out)


def _layer_b_kernel(sinks_ref, hc_ref, hp_ref, kvg_ref, bg_ref, wkv_hbm, bkv_ref, win_hbm, bq_ref,
                    wout_hbm, fg_ref, cos_ref, sin_ref,
                    o_ref, kt_ref, vx_ref, q_ref, gate_ref, y_ref, wkv_ref, win_ref, wout_ref,
                    *, tiles_per_seq):
    t = pl.program_id(0)

    @pl.when(t == 0)
    def _():
        _load_weights_as_bf16([(wkv_hbm, wkv_ref, kvg_ref), (win_hbm, win_ref, bg_ref),
                               (wout_hbm, wout_ref, None)], slots=None)

    refs = (sinks_ref, hc_ref, hp_ref, wkv_ref, bkv_ref, win_ref, bq_ref,
            wout_ref, fg_ref, cos_ref, sin_ref, o_ref, kt_ref, vx_ref, q_ref, gate_ref, y_ref)

    @pl.when(t % 2 == 0)
    def _():
        _layer_b_step(*refs, t=t, slot=0, tiles_per_seq=tiles_per_seq)

    @pl.when(t % 2 == 1)
    def _():
        _layer_b_step(*refs, t=t, slot=1, tiles_per_seq=tiles_per_seq)


def _layer_b_step(sinks_ref, hc_ref, hp_ref, wkv_ref, bkv_ref, win_ref, bq_ref,
                  wout_ref, fg_ref, cos_ref, sin_ref,
                  o_ref, kt_ref, vx_ref, q_ref, gate_ref, y_ref, *, t, slot, tiles_per_seq):
    tm = hc_ref.shape[0]
    prev = 1 - slot
    new = slice(CHUNK, CHUNK + tm)
    planes = 2 * N_KV_HEADS

    if slot == 0:
        @pl.when(t == 0)
        def _():
            kt_ref[prev] = jnp.zeros((planes, LANES, CHUNK + tm), _BF16)
            vx_ref[prev] = jnp.zeros((planes, CHUNK + tm, 2 * LANES), _BF16)
            q_ref[prev] = jnp.zeros((tm, B_WIDTH), _BF16)
            gate_ref[prev] = jnp.zeros((tm, B_WIDTH), _F32)

    first_in_seq = (t % tiles_per_seq) == 0

    @pl.when(first_in_seq)
    def _():
        kt_ref[slot, :, :, 0:CHUNK] = jnp.zeros((planes, LANES, CHUNK), _BF16)
        vx_ref[slot, :, 0:CHUNK, :] = jnp.zeros((planes, CHUNK, 2 * LANES), _BF16)

    @pl.when(jnp.logical_not(first_in_seq))
    def _():
        kt_ref[slot, :, :, 0:CHUNK] = kt_ref[prev, :, :, tm:tm + CHUNK]
        vx_ref[slot, :, 0:CHUNK, :] = vx_ref[prev, :, tm:tm + CHUNK, :]

    h = hc_ref[...]
    h2 = (h * lax.rsqrt(jnp.mean(h * h, axis=-1, keepdims=True) + EPS)).astype(_BF16)

    cos = cos_ref[...]
    sin = sin_ref[...]
    lane = lax.broadcasted_iota(jnp.int32, (tm, LANES), 1)
    first_half = (lane % HEAD_DIM) < (HEAD_DIM // 2)
    low_head = lane < HEAD_DIM

    def rope(x, cos_t, sin_t):
        swapped = jnp.where(first_half,
                            pltpu.roll(x, LANES - HEAD_DIM // 2, 1),
                            pltpu.roll(x, HEAD_DIM // 2, 1))
        return x * cos_t + swapped * sin_t

    def project_kv():
        kv = _dot(h2, wkv_ref[...]) + bkv_ref[...]
        k_t = rope(kv[:, :KV_WIDTH], cos, sin).T
        zero_rows = jnp.zeros((HEAD_DIM, tm), _F32)
        for kvh in range(N_KV_HEADS):
            head_t = k_t[kvh * HEAD_DIM:(kvh + 1) * HEAD_DIM]
            kt_ref[slot, 2 * kvh, :, new] = jnp.concatenate([head_t, zero_rows], axis=0).astype(_BF16)
            kt_ref[slot, 2 * kvh + 1, :, new] = jnp.concatenate([zero_rows, head_t], axis=0).astype(_BF16)
        v = kv[:, KV_WIDTH:]
        v_sw = pltpu.roll(v, HEAD_DIM, 1)
        zero = jnp.zeros_like(v)
        one = jnp.ones_like(v)
        ones_lo = jnp.where(low_head, one, zero)
        ones_hi = jnp.where(low_head, zero, one)
        v_low = (jnp.where(low_head, v, zero), jnp.where(low_head, v_sw, zero))
        v_high = (jnp.where(low_head, zero, v_sw), jnp.where(low_head, zero, v))
        for kvh in range(N_KV_HEADS):
            vx_ref[slot, 2 * kvh, new, :] = jnp.concatenate([v_low[kvh], ones_lo], axis=1).astype(_BF16)
            vx_ref[slot, 2 * kvh + 1, new, :] = jnp.concatenate([v_high[kvh], ones_hi], axis=1).astype(_BF16)

    q_scale = (HEAD_DIM ** -0.5) * LOG2_E
    cos_q = cos * q_scale
    sin_q = sin * q_scale

    def project_q(b):
        cols = slice(b * P_BLOCK, (b + 1) * P_BLOCK)
        zq = _dot(h2, win_ref[:, cols]) + bq_ref[:, cols]
        for i in range(P_BLOCK // LANES):
            s = b * (P_BLOCK // LANES) + i
            q_ref[slot, :, s * LANES:(s + 1) * LANES] = rope(
                zq[:, i * LANES:(i + 1) * LANES], cos_q, sin_q).astype(_BF16)

    def project_gate(b):
        cols = slice(b * P_BLOCK, (b + 1) * P_BLOCK)
        gate_ref[slot, :, cols] = _dot(h2, win_ref[:, B_WIDTH + b * P_BLOCK:B_WIDTH + (b + 1) * P_BLOCK])

    qi = lax.broadcasted_iota(jnp.int32, (CHUNK, 2 * CHUNK), 0)
    kj = lax.broadcasted_iota(jnp.int32, (CHUNK, 2 * CHUNK), 1)
    in_window = (kj > qi) & (kj <= qi + CHUNK)
    neg_inf = jnp.float32(-jnp.inf)
    bias = jnp.where(in_window, 0.0, neg_inf)
    prev_first_in_seq = ((t + tiles_per_seq - 1) % tiles_per_seq) == 0
    bias_first = jnp.where(in_window & ((kj >= CHUNK) | jnp.logical_not(prev_first_in_seq)), 0.0, neg_inf)

    units = [(c, kvh) for kvh in range(N_KV_HEADS) for c in range(tm // CHUNK)]
    assert len(units) == 4 and B_WIDTH // P_BLOCK == 4, "the program order below is written for 4 + 4"

    def scores(unit):
        c, kvh = unit
        rows = slice(c * CHUNK, (c + 1) * CHUNK)
        band = slice(c * CHUNK, (c + 2) * CHUNK)
        q_stack = jnp.concatenate(
            [q_ref[prev, rows, s * LANES:(s + 1) * LANES]
             for s in range(kvh * SLABS_PER_KV, (kvh + 1) * SLABS_PER_KV)], axis=0)
        return [_dot(q_stack, kt_ref[prev, 2 * kvh + half, :, band]) for half in range(HEADS_PER_SLAB)]

    sink_slot = lax.broadcasted_iota(jnp.int32, (CHUNK, LANES), 1) == 0
    key_row = lax.broadcasted_iota(jnp.int32, (BF16_ROWS, 2 * LANES), 0)
    key_lane = lax.broadcasted_iota(jnp.int32, (BF16_ROWS, 2 * LANES), 1)

    def softmax(unit, sc_halves):
        c, kvh = unit
        chunk_bias = bias_first if c == 0 else bias
        probs = []
        for half, sc in enumerate(sc_halves):
            p_parts = []
            for si in range(SLABS_PER_KV):
                s = kvh * SLABS_PER_KV + si
                sink = sinks_ref[s * HEADS_PER_SLAB + half] * LOG2_E
                blk = sc[si * CHUNK:(si + 1) * CHUNK]
                sb = jnp.concatenate(
                    [jnp.where(sink_slot, sink, blk[:, :LANES] + chunk_bias[:, :LANES]),
                     blk[:, LANES:] + chunk_bias[:, LANES:]], axis=1)
                m = jnp.max(sb, axis=-1, keepdims=True)
                p_parts.append(jnp.exp2(sb - m).astype(_BF16))
            probs.append(jnp.concatenate(p_parts, axis=0))
        return probs

    def band_values(kvh, half, c):
        r0 = c * CHUNK
        ones_lo = LANES + half * HEAD_DIM
        sink_row = jnp.where((key_lane >= ones_lo) & (key_lane < ones_lo + HEAD_DIM), 1.0, 0.0)
        head = vx_ref[prev, 2 * kvh + half, r0:r0 + BF16_ROWS, :].astype(_F32)
        head = jnp.where(key_row == 0, sink_row, head).astype(_BF16)
        return jnp.concatenate(
            [head, vx_ref[prev, 2 * kvh + half, r0 + BF16_ROWS:r0 + 2 * CHUNK, :]], axis=0)

    def attend(unit, probs):
        c, kvh = unit
        rows = slice(c * CHUNK, (c + 1) * CHUNK)
        od = _dot(probs[0], band_values(kvh, 0, c)) + _dot(probs[1], band_values(kvh, 1, c))
        for si in range(SLABS_PER_KV):
            s = kvh * SLABS_PER_KV + si
            blk = slice(si * CHUNK, (si + 1) * CHUNK)
            gate = gate_ref[prev, rows, s * LANES:(s + 1) * LANES]
            y_ref[rows, s * LANES:(s + 1) * LANES] = (
                od[blk, :LANES] * _silu(gate) / od[blk, LANES:]).astype(_BF16)

    def out_part(kvh):
        cols = slice(kvh * SLABS_PER_KV * LANES, (kvh + 1) * SLABS_PER_KV * LANES)
        return _dot(y_ref[:, cols], wout_ref[cols, :])

    s0 = scores(units[0])
    project_kv()
    s1 = scores(units[1])
    project_q(0)
    p0 = softmax(units[0], s0)
    attend(units[0], p0)
    s2 = scores(units[2])
    project_q(1)
    p1 = softmax(units[1], s1)
    attend(units[1], p1)
    s3 = scores(units[3])
    project_q(2)
    p2 = softmax(units[2], s2)
    attend(units[2], p2)
    project_q(3)
    project_gate(0)
    acc = out_part(0)
    p3 = softmax(units[3], s3)
    attend(units[3], p3)
    project_gate(1)
    out = hp_ref[...] + acc + out_part(1)
    o_ref[...] = out * lax.rsqrt(jnp.mean(out * out, axis=-1, keepdims=True) + EPS) * fg_ref[...]
    project_gate(2)
    project_gate(3)


def _layer_b(h2d, seq, sinks, kv_g, b_g, w_kv, b_kv, w_in, b_q, w_out, f_g, cos, sin):
    tm = B_TOKENS_PER_STEP
    n_tiles = h2d.shape[0] // tm
    tiles_per_seq = seq // tm
    const = lambda t: (0, 0)
    cur = lambda t: (jnp.minimum(t, n_tiles - 1), 0)
    prv = lambda t: (jnp.maximum(t - 1, 0), 0)
    tab = lambda t: (jnp.minimum(t, n_tiles - 1) % tiles_per_seq, 0)
    planes = 2 * N_KV_HEADS
    return pl.pallas_call(
        functools.partial(_layer_b_kernel, tiles_per_seq=tiles_per_seq),
        out_shape=jax.ShapeDtypeStruct(h2d.shape, _F32),
        grid=(n_tiles + 1,),
        in_specs=[
            pl.BlockSpec(memory_space=pltpu.SMEM),
            pl.BlockSpec((tm, D_MODEL), cur),
            pl.BlockSpec((tm, D_MODEL), prv),
            pl.BlockSpec((1, D_MODEL), const),
            pl.BlockSpec((1, D_MODEL), const),
            pl.BlockSpec(memory_space=pl.ANY),
            pl.BlockSpec((1, 2 * KV_WIDTH), const),
            pl.BlockSpec(memory_space=pl.ANY),
            pl.BlockSpec((1, B_WIDTH), const),
            pl.BlockSpec(memory_space=pl.ANY),
            pl.BlockSpec((1, D_MODEL), const),
            pl.BlockSpec((tm, LANES), tab),
            pl.BlockSpec((tm, LANES), tab),
        ],
        out_specs=pl.BlockSpec((tm, D_MODEL), prv),
        scratch_shapes=[
            pltpu.VMEM((2, planes, LANES, CHUNK + tm), _BF16),
            pltpu.VMEM((2, planes, CHUNK + tm, 2 * LANES), _BF16),
            pltpu.VMEM((2, tm, B_WIDTH), _BF16),
            pltpu.VMEM((2, tm, B_WIDTH), _F32),
            pltpu.VMEM((tm, B_WIDTH), _BF16),
            pltpu.VMEM((D_MODEL, 2 * KV_WIDTH), _BF16),
            pltpu.VMEM((D_MODEL, 2 * B_WIDTH), _BF16),
            pltpu.VMEM((B_WIDTH, D_MODEL), _BF16),
        ],
        compiler_params=pltpu.CompilerParams(
            dimension_semantics=("arbitrary",), vmem_limit_bytes=VMEM_LIMIT_BYTES),
        name="swa_layer",
    )(sinks, h2d, h2d, kv_g, b_g, w_kv, b_kv, w_in, b_q, w_out, f_g, cos, sin)


def _rope_tables(seq):
    inv_freq = ROPE_THETA ** (-np.arange(0, HEAD_DIM, 2, dtype=np.float64) / HEAD_DIM)
    ang = np.arange(seq, dtype=np.float64)[:, None] * inv_freq[None, :]
    cos = np.cos(ang)
    sin = np.sin(ang)
    cos_slab = np.tile(np.concatenate([cos, cos], axis=-1), (1, HEADS_PER_SLAB))
    sin_slab = np.tile(np.concatenate([-sin, sin], axis=-1), (1, HEADS_PER_SLAB))
    assert cos_slab.shape == (seq, LANES)
    return jnp.asarray(cos_slab, dtype=_F32), jnp.asarray(sin_slab, dtype=_F32)


def kernel(x, a_norm_g, a_w_in, a_ln_g, a_ln_b, a_ws, a_bs, a_w_out, kv_norm_g, w_kv, b_kv,
           b_norm_g, b_w_in, b_bq, b_sinks, b_w_out, final_norm_g):
    batch, seq, d = x.shape
    assert d == D_MODEL
    for tile in (A_TOKENS_PER_STEP, B_TOKENS_PER_STEP):
        assert seq % tile == 0 and tile % CHUNK == 0
    assert a_w_in.shape[0] == 1 and b_w_in.shape[0] == 1
    row = lambda p: p.reshape(1, -1)

    x2 = x.reshape(batch * seq, d)
    h = _layer_a(x2, row(a_norm_g[0]), a_w_in[0], row(a_ln_g[0]), row(a_ln_b[0]),
                 a_ws[0], a_bs[0], a_w_out[0])
    cos, sin = _rope_tables(seq)
    out = _layer_b(h, seq, b_sinks[0], row(kv_norm_g), row(b_norm_g[0]),
                   w_kv, row(b_kv), b_w_in[0], row(b_bq[0]),
                   b_w_out[0], row(final_norm_g), cos, sin)
    return out.reshape(batch, seq, d)
```

```python
import functools

import jax
import jax.numpy as jnp
import numpy as np
from jax import lax
from jax.experimental import pallas as pl
from jax.experimental.pallas import tpu as pltpu

D_MODEL = 1024
CHUNK = 128
A_WIDTH = 2048
A_GROUPS = 16
HEAD_DIM = 64
N_Q_HEADS = 16
N_KV_HEADS = 2
B_WIDTH = 1024
KV_WIDTH = 128
ROPE_THETA = 10000.0
EPS = 1e-5
LOG2_E = 1.4426950408889634

LANES = 128
BF16_ROWS = 16
HEADS_PER_SLAB = LANES // HEAD_DIM
N_SLABS = B_WIDTH // LANES
SLABS_PER_KV = N_SLABS // N_KV_HEADS

A_TOKENS_PER_STEP = 1024
B_TOKENS_PER_STEP = 256
P_BLOCK = 256
A_COL_BLOCK = 512
VMEM_LIMIT_BYTES = 56 * 1024 * 1024
WEIGHT_STAGE_BYTES = 1024 * 1024

_F32 = jnp.float32
_BF16 = jnp.bfloat16


def _dot(a, b):
    return jnp.dot(a, b, preferred_element_type=_F32)


def _silu(g):
    half_g = 0.5 * g
    return half_g * (1.0 + jnp.tanh(half_g))


def _load_weights_as_bf16(weights, slots):
    plans = []
    for w_hbm, w_ref, row_scale_ref in weights:
        k, n = w_hbm.shape
        rows = min(k, pl.next_power_of_2(WEIGHT_STAGE_BYTES // (4 * n) + 1) // 2)
        assert k % rows == 0 and rows % 16 == 0
        n_chunks = k // rows
        n_slots = n_chunks if slots is None else min(slots, n_chunks)
        scale_col = None
        if row_scale_ref is not None:
            scale_col = jnp.broadcast_to(row_scale_ref[...], (LANES, k)).T
        plans.append((w_hbm, w_ref, scale_col, rows, n_chunks, n_slots))

    def body(*scratch):
        stages, sems = scratch[0::2], scratch[1::2]

        def copy(j, i):
            w_hbm, _, _, rows, _, n_slots = plans[j]
            return pltpu.make_async_copy(
                w_hbm.at[pl.ds(i * rows, rows), :], stages[j].at[i % n_slots], sems[j].at[i % n_slots])

        for j, plan in enumerate(plans):
            for i in range(plan[5]):
                copy(j, i).start()
        for j, (_, w_ref, scale_col, rows, n_chunks, n_slots) in enumerate(plans):
            for i in range(n_chunks):
                copy(j, i).wait()
                chunk = stages[j][i % n_slots]
                if scale_col is not None:
                    chunk = chunk * scale_col[i * rows:(i + 1) * rows, 0:1]
                w_ref[i * rows:(i + 1) * rows, :] = chunk.astype(_BF16)
                if i + n_slots < n_chunks:
                    copy(j, i + n_slots).start()

    scratch_types = []
    for w_hbm, _, _, rows, _, n_slots in plans:
        scratch_types += [pltpu.VMEM((n_slots, rows, w_hbm.shape[1]), _F32), pltpu.SemaphoreType.DMA((n_slots,))]
    pl.run_scoped(body, *scratch_types)


def _layer_a_kernel(x_ref, ng_ref, win_hbm, lng_ref, lnb_ref, ws_ref, bs_ref, wout_hbm,
                    o_ref, y_ref, win_ref, wout_ref, wsm_ref, bias_ref):
    @pl.when(pl.program_id(0) == 0)
    def _():
        _load_weights_as_bf16([(win_hbm, win_ref, None)], slots=4)
        _load_weights_as_bf16([(wout_hbm, wout_ref, None)], slots=4)
        row = lax.broadcasted_iota(jnp.int32, (CHUNK, CHUNK), 0)
        col = lax.broadcasted_iota(jnp.int32, (CHUNK, CHUNK), 1)
        for g in range(A_GROUPS):
            wsm_ref[g] = jnp.where(col <= row, ws_ref[g], 0.0).astype(_BF16)
            bias_ref[g] = jnp.broadcast_to(bs_ref[g:g + 1, :], (CHUNK, CHUNK)).T

    tm = x_ref.shape[0]
    groups_per_block = A_COL_BLOCK // CHUNK

    x = x_ref[...]
    h1 = (x * lax.rsqrt(jnp.mean(x * x, axis=-1, keepdims=True) + EPS) * ng_ref[...]).astype(_BF16)

    v = _dot(h1, win_ref[:, A_WIDTH:2 * A_WIDTH])
    d = v - jnp.mean(v[:, :LANES], axis=-1, keepdims=True)
    m1 = jnp.mean(d, axis=-1, keepdims=True)
    rstd = lax.rsqrt(jnp.mean(d * d, axis=-1, keepdims=True) - m1 * m1 + EPS)

    for j in range(A_WIDTH // A_COL_BLOCK):
        c0 = j * A_COL_BLOCK
        u = _dot(h1, win_ref[:, c0:c0 + A_COL_BLOCK])
        gt = _dot(h1, win_ref[:, 2 * A_WIDTH + c0:2 * A_WIDTH + c0 + A_COL_BLOCK])
        for gi in range(groups_per_block):
            g = j * groups_per_block + gi
            gcols = slice(g * CHUNK, (g + 1) * CHUNK)
            cols = slice(gi * CHUNK, (gi + 1) * CHUNK)
            vn = ((d[:, gcols] - m1) * rstd * lng_ref[:, gcols] + lnb_ref[:, gcols]).astype(_BF16)
            for c in range(tm // CHUNK):
                rows = slice(c * CHUNK, (c + 1) * CHUNK)
                sv = _dot(wsm_ref[g], vn[rows]) + bias_ref[g]
                y_ref[rows, gcols] = (u[rows, cols] * sv * _silu(gt[rows, cols])).astype(_BF16)

    o_ref[...] = x + _dot(y_ref[...], wout_ref[...])


def _layer_a(x2, norm_g, w_in, ln_g, ln_b, ws, bs, w_out):
    t = x2.shape[0]
    tm = A_TOKENS_PER_STEP
    const = lambda i: (0, 0)
    return pl.pallas_call(
        _layer_a_kernel,
        out_shape=jax.ShapeDtypeStruct((t, D_MODEL), _F32),
        grid=(t // tm,),
        in_specs=[
            pl.BlockSpec((tm, D_MODEL), lambda i: (i, 0)),
            pl.BlockSpec((1, D_MODEL), const),
            pl.BlockSpec(memory_space=pl.ANY),
            pl.BlockSpec((1, A_WIDTH), const),
            pl.BlockSpec((1, A_WIDTH), const),
            pl.BlockSpec((A_GROUPS, CHUNK, CHUNK), lambda i: (0, 0, 0)),
            pl.BlockSpec((A_GROUPS, CHUNK), const),
            pl.BlockSpec(memory_space=pl.ANY),
        ],
        out_specs=pl.BlockSpec((tm, D_MODEL), lambda i: (i, 0)),
        scratch_shapes=[
            pltpu.VMEM((tm, A_WIDTH), _BF16),
            pltpu.VMEM((D_MODEL, 3 * A_WIDTH), _BF16),
            pltpu.VMEM((A_WIDTH, D_MODEL), _BF16),
            pltpu.VMEM((A_GROUPS, CHUNK, CHUNK), _BF16),
            pltpu.VMEM((A_GROUPS, CHUNK, CHUNK), _F32),
        ],
        compiler_params=pltpu.CompilerParams(
            dimension_semantics=("arbitrary",), vmem_limit_bytes=VMEM_LIMIT_BYTES),
        name="gmlp_layer",
    )(x2, norm_g, w_in, ln_g, ln_b, ws, bs, w_out)


def _layer_b_kernel(sinks_ref, hc_ref, hp_ref, kvg_ref, bg_ref, wkv_hbm, bkv_ref, win_hbm, bq_ref,
                    wout_hbm, fg_ref, cos_ref, sin_ref,
                    o_ref, kt_ref, vx_ref, q_ref, gate_ref, y_ref, wkv_ref, win_ref, wout_ref,
                    *, tiles_per_seq):
    t = pl.program_id(0)

    @pl.when(t == 0)
    def _():
        _load_weights_as_bf16([(wkv_hbm, wkv_ref, kvg_ref), (win_hbm, win_ref, bg_ref),
                               (wout_hbm, wout_ref, None)], slots=None)

    refs = (sinks_ref, hc_ref, hp_ref, wkv_ref, bkv_ref, win_ref, bq_ref,
            wout_ref, fg_ref, cos_ref, sin_ref, o_ref, kt_ref, vx_ref, q_ref, gate_ref, y_ref)

    @pl.when(t % 2 == 0)
    def _():
        _layer_b_step(*refs, t=t, slot=0, tiles_per_seq=tiles_per_seq)

    @pl.when(t % 2 == 1)
    def _():
        _layer_b_step(*refs, t=t, slot=1, tiles_per_seq=tiles_per_seq)


def _layer_b_step(sinks_ref, hc_ref, hp_ref, wkv_ref, bkv_ref, win_ref, bq_ref,
                  wout_ref, fg_ref, cos_ref, sin_ref,
                  o_ref, kt_ref, vx_ref, q_ref, gate_ref, y_ref, *, t, slot, tiles_per_seq):
    tm = hc_ref.shape[0]
    prev = 1 - slot
    new = slice(CHUNK, CHUNK + tm)
    planes = 2 * N_KV_HEADS

    if slot == 0:
        @pl.when(t == 0)
        def _():
            kt_ref[prev] = jnp.zeros((planes, LANES, CHUNK + tm), _BF16)
            vx_ref[prev] = jnp.zeros((planes, CHUNK + tm, 2 * LANES), _BF16)
            q_ref[prev] = jnp.zeros((tm, B_WIDTH), _BF16)
            gate_ref[prev] = jnp.zeros((tm, B_WIDTH), _F32)

    first_in_seq = (t % tiles_per_seq) == 0

    @pl.when(first_in_seq)
    def _():
        kt_ref[slot, :, :, 0:CHUNK] = jnp.zeros((planes, LANES, CHUNK), _BF16)
        vx_ref[slot, :, 0:CHUNK, :] = jnp.zeros((planes, CHUNK, 2 * LANES), _BF16)

    @pl.when(jnp.logical_not(first_in_seq))
    def _():
        kt_ref[slot, :, :, 0:CHUNK] = kt_ref[prev, :, :, tm:tm + CHUNK]
        vx_ref[slot, :, 0:CHUNK, :] = vx_ref[prev, :, tm:tm + CHUNK, :]

    h = hc_ref[...]
    h2 = (h * lax.rsqrt(jnp.mean(h * h, axis=-1, keepdims=True) + EPS)).astype(_BF16)

    cos = cos_ref[...]
    sin = sin_ref[...]
    lane = lax.broadcasted_iota(jnp.int32, (tm, LANES), 1)
    first_half = (lane % HEAD_DIM) < (HEAD_DIM // 2)
    low_head = lane < HEAD_DIM

    def rope(x, cos_t, sin_t):
        swapped = jnp.where(first_half,
                            pltpu.roll(x, LANES - HEAD_DIM // 2, 1),
                            pltpu.roll(x, HEAD_DIM // 2, 1))
        return x * cos_t + swapped * sin_t

    def project_kv():
        kv = _dot(h2, wkv_ref[...]) + bkv_ref[...]
        k_t = rope(kv[:, :KV_WIDTH], cos, sin).T
        zero_rows = jnp.zeros((HEAD_DIM, tm), _F32)
        for kvh in range(N_KV_HEADS):
            head_t = k_t[kvh * HEAD_DIM:(kvh + 1) * HEAD_DIM]
            kt_ref[slot, 2 * kvh, :, new] = jnp.concatenate([head_t, zero_rows], axis=0).astype(_BF16)
            kt_ref[slot, 2 * kvh + 1, :, new] = jnp.concatenate([zero_rows, head_t], axis=0).astype(_BF16)
        v = kv[:, KV_WIDTH:]
        v_sw = pltpu.roll(v, HEAD_DIM, 1)
        zero = jnp.zeros_like(v)
        one = jnp.ones_like(v)
        ones_lo = jnp.where(low_head, one, zero)
        ones_hi = jnp.where(low_head, zero, one)
        v_low = (jnp.where(low_head, v, zero), jnp.where(low_head, v_sw, zero))
        v_high = (jnp.where(low_head, zero, v_sw), jnp.where(low_head, zero, v))
        for kvh in range(N_KV_HEADS):
            vx_ref[slot, 2 * kvh, new, :] = jnp.concatenate([v_low[kvh], ones_lo], axis=1).astype(_BF16)
            vx_ref[slot, 2 * kvh + 1, new, :] = jnp.concatenate([v_high[kvh], ones_hi], axis=1).astype(_BF16)

    q_scale = (HEAD_DIM ** -0.5) * LOG2_E
    cos_q = cos * q_scale
    sin_q = sin * q_scale

    def project_q(b):
        cols = slice(b * P_BLOCK, (b + 1) * P_BLOCK)
        zq = _dot(h2, win_ref[:, cols]) + bq_ref[:, cols]
        for i in range(P_BLOCK // LANES):
            s = b * (P_BLOCK // LANES) + i
            q_ref[slot, :, s * LANES:(s + 1) * LANES] = rope(
                zq[:, i * LANES:(i + 1) * LANES], cos_q, sin_q).astype(_BF16)

    def project_gate(b):
        cols = slice(b * P_BLOCK, (b + 1) * P_BLOCK)
        gate_ref[slot, :, cols] = _dot(h2, win_ref[:, B_WIDTH + b * P_BLOCK:B_WIDTH + (b + 1) * P_BLOCK])

    qi = lax.broadcasted_iota(jnp.int32, (CHUNK, 2 * CHUNK), 0)
    kj = lax.broadcasted_iota(jnp.int32, (CHUNK, 2 * CHUNK), 1)
    in_window = (kj > qi) & (kj <= qi + CHUNK)
    neg_inf = jnp.float32(-jnp.inf)
    bias = jnp.where(in_window, 0.0, neg_inf)
    prev_first_in_seq = ((t + tiles_per_seq - 1) % tiles_per_seq) == 0
    bias_first = jnp.where(in_window & ((kj >= CHUNK) | jnp.logical_not(prev_first_in_seq)), 0.0, neg_inf)

    units = [(c, kvh) for kvh in range(N_KV_HEADS) for c in range(tm // CHUNK)]
    assert len(units) == 4 and B_WIDTH // P_BLOCK == 4, "the program order below is written for 4 + 4"

    def scores(unit):
        c, kvh = unit
        rows = slice(c * CHUNK, (c + 1) * CHUNK)
        band = slice(c * CHUNK, (c + 2) * CHUNK)
        q_stack = jnp.concatenate(
            [q_ref[prev, rows, s * LANES:(s + 1) * LANES]
             for s in range(kvh * SLABS_PER_KV, (kvh + 1) * SLABS_PER_KV)], axis=0)
        return [_dot(q_stack, kt_ref[prev, 2 * kvh + half, :, band]) for half in range(HEADS_PER_SLAB)]

    sink_slot = lax.broadcasted_iota(jnp.int32, (CHUNK, LANES), 1) == 0
    key_row = lax.broadcasted_iota(jnp.int32, (BF16_ROWS, 2 * LANES), 0)
    key_lane = lax.broadcasted_iota(jnp.int32, (BF16_ROWS, 2 * LANES), 1)

    def softmax(unit, sc_halves):
        c, kvh = unit
        chunk_bias = bias_first if c == 0 else bias
        probs = []
        for half, sc in enumerate(sc_halves):
            p_parts = []
            for si in range(SLABS_PER_KV):
                s = kvh * SLABS_PER_KV + si
                sink = sinks_ref[s * HEADS_PER_SLAB + half] * LOG2_E
                blk = sc[si * CHUNK:(si + 1) * CHUNK]
                sb = jnp.concatenate(
                    [jnp.where(sink_slot, sink, blk[:, :LANES] + chunk_bias[:, :LANES]),
                     blk[:, LANES:] + chunk_bias[:, LANES:]], axis=1)
                m = jnp.max(sb, axis=-1, keepdims=True)
                p_parts.append(jnp.exp2(sb - m).astype(_BF16))
            probs.append(jnp.concatenate(p_parts, axis=0))
        return probs

    def band_values(kvh, half, c):
        r0 = c * CHUNK
        ones_lo = LANES + half * HEAD_DIM
        sink_row = jnp.where((key_lane >= ones_lo) & (key_lane < ones_lo + HEAD_DIM), 1.0, 0.0)
        head = vx_ref[prev, 2 * kvh + half, r0:r0 + BF16_ROWS, :].astype(_F32)
        head = jnp.where(key_row == 0, sink_row, head).astype(_BF16)
        return jnp.concatenate(
            [head, vx_ref[prev, 2 * kvh + half, r0 + BF16_ROWS:r0 + 2 * CHUNK, :]], axis=0)

    def attend(unit, probs):
        c, kvh = unit
        rows = slice(c * CHUNK, (c + 1) * CHUNK)
        od = _dot(probs[0], band_values(kvh, 0, c)) + _dot(probs[1], band_values(kvh, 1, c))
        for si in range(SLABS_PER_KV):
            s = kvh * SLABS_PER_KV + si
            blk = slice(si * CHUNK, (si + 1) * CHUNK)
            gate = gate_ref[prev, rows, s * LANES:(s + 1) * LANES]
            y_ref[rows, s * LANES:(s + 1) * LANES] = (
                od[blk, :LANES] * _silu(gate) / od[blk, LANES:]).astype(_BF16)

    def out_part(kvh):
        cols = slice(kvh * SLABS_PER_KV * LANES, (kvh + 1) * SLABS_PER_KV * LANES)
        return _dot(y_ref[:, cols], wout_ref[cols, :])

    s0 = scores(units[0])
    project_kv()
    s1 = scores(units[1])
    project_q(0)
    p0 = softmax(units[0], s0)
    attend(units[0], p0)
    s2 = scores(units[2])
    project_q(1)
    p1 = softmax(units[1], s1)
    attend(units[1], p1)
    s3 = scores(units[3])
    project_q(2)
    p2 = softmax(units[2], s2)
    attend(units[2], p2)
    project_q(3)
    project_gate(0)
    acc = out_part(0)
    p3 = softmax(units[3], s3)
    attend(units[3], p3)
    project_gate(1)
    out = hp_ref[...] + acc + out_part(1)
    o_ref[...] = out * lax.rsqrt(jnp.mean(out * out, axis=-1, keepdims=True) + EPS) * fg_ref[...]
    project_gate(2)
    project_gate(3)


def _layer_b(h2d, seq, sinks, kv_g, b_g, w_kv, b_kv, w_in, b_q, w_out, f_g, cos, sin):
    tm = B_TOKENS_PER_STEP
    n_tiles = h2d.shape[0] // tm
    tiles_per_seq = seq // tm
    const = lambda t: (0, 0)
    cur = lambda t: (jnp.minimum(t, n_tiles - 1), 0)
    prv = lambda t: (jnp.maximum(t - 1, 0), 0)
    tab = lambda t: (jnp.minimum(t, n_tiles - 1) % tiles_per_seq, 0)
    planes = 2 * N_KV_HEADS
    return pl.pallas_call(
        functools.partial(_layer_b_kernel, tiles_per_seq=tiles_per_seq),
        out_shape=jax.ShapeDtypeStruct(h2d.shape, _F32),
        grid=(n_tiles + 1,),
        in_specs=[
            pl.BlockSpec(memory_space=pltpu.SMEM),
            pl.BlockSpec((tm, D_MODEL), cur),
            pl.BlockSpec((tm, D_MODEL), prv),
            pl.BlockSpec((1, D_MODEL), const),
            pl.BlockSpec((1, D_MODEL), const),
            pl.BlockSpec(memory_space=pl.ANY),
            pl.BlockSpec((1, 2 * KV_WIDTH), const),
            pl.BlockSpec(memory_space=pl.ANY),
            pl.BlockSpec((1, B_WIDTH), const),
            pl.BlockSpec(memory_space=pl.ANY),
            pl.BlockSpec((1, D_MODEL), const),
            pl.BlockSpec((tm, LANES), tab),
            pl.BlockSpec((tm, LANES), tab),
        ],
        out_specs=pl.BlockSpec((tm, D_MODEL), prv),
        scratch_shapes=[
            pltpu.VMEM((2, planes, LANES, CHUNK + tm), _BF16),
            pltpu.VMEM((2, planes, CHUNK + tm, 2 * LANES), _BF16),
            pltpu.VMEM((2, tm, B_WIDTH), _BF16),
            pltpu.VMEM((2, tm, B_WIDTH), _F32),
            pltpu.VMEM((tm, B_WIDTH), _BF16),
            pltpu.VMEM((D_MODEL, 2 * KV_WIDTH), _BF16),
            pltpu.VMEM((D_MODEL, 2 * B_WIDTH), _BF16),
            pltpu.VMEM((B_WIDTH, D_MODEL), _BF16),
        ],
        compiler_params=pltpu.CompilerParams(
            dimension_semantics=("arbitrary",), vmem_limit_bytes=VMEM_LIMIT_BYTES),
        name="swa_layer",
    )(sinks, h2d, h2d, kv_g, b_g, w_kv, b_kv, w_in, b_q, w_out, f_g, cos, sin)


def _rope_tables(seq):
    inv_freq = ROPE_THETA ** (-np.arange(0, HEAD_DIM, 2, dtype=np.float64) / HEAD_DIM)
    ang = np.arange(seq, dtype=np.float64)[:, None] * inv_freq[None, :]
    cos = np.cos(ang)
    sin = np.sin(ang)
    cos_slab = np.tile(np.concatenate([cos, cos], axis=-1), (1, HEADS_PER_SLAB))
    sin_slab = np.tile(np.concatenate([-sin, sin], axis=-1), (1, HEADS_PER_SLAB))
    assert cos_slab.shape == (seq, LANES)
    return jnp.asarray(cos_slab, dtype=_F32), jnp.asarray(sin_slab, dtype=_F32)


def kernel(x, a_norm_g, a_w_in, a_ln_g, a_ln_b, a_ws, a_bs, a_w_out, kv_norm_g, w_kv, b_kv,
           b_norm_g, b_w_in, b_bq, b_sinks, b_w_out, final_norm_g):
    batch, seq, d = x.shape
    assert d == D_MODEL
    for tile in (A_TOKENS_PER_STEP, B_TOKENS_PER_STEP):
        assert seq % tile == 0 and tile % CHUNK == 0
    assert a_w_in.shape[0] == 1 and b_w_in.shape[0] == 1
    row = lambda p: p.reshape(1, -1)

    x2 = x.reshape(batch * seq, d)
    h = _layer_a(x2, row(a_norm_g[0]), a_w_in[0], row(a_ln_g[0]), row(a_ln_b[0]),
                 a_ws[0], a_bs[0], a_w_out[0])
    cos, sin = _rope_tables(seq)
    out = _layer_b(h, seq, b_sinks[0], row(kv_norm_g), row(b_norm_g[0]),
                   w_kv, row(b_kv), b_w_in[0], row(b_bq[0]),
                   b_w_out[0], row(final_norm_g), cos, sin)
    return out.reshape(batch, seq, d)
```

```python
import functools

import jax
import jax.numpy as jnp
import numpy as np
from jax import lax
from jax.experimental import pallas as pl
from jax.experimental.pallas import tpu as pltpu

D_MODEL = 1024
CHUNK = 128
A_WIDTH = 2048
A_GROUPS = 16
HEAD_DIM = 64
N_Q_HEADS = 16
N_KV_HEADS = 2
B_WIDTH = 1024
KV_WIDTH = 128
ROPE_THETA = 10000.0
EPS = 1e-5
LOG2_E = 1.4426950408889634

LANES = 128
BF16_ROWS = 16
HEADS_PER_SLAB = LANES // HEAD_DIM
N_SLABS = B_WIDTH // LANES
SLABS_PER_KV = N_SLABS // N_KV_HEADS

A_TOKENS_PER_STEP = 1024
B_TOKENS_PER_STEP = 256
P_BLOCK = 256
A_COL_BLOCK = 512
VMEM_LIMIT_BYTES = 56 * 1024 * 1024
WEIGHT_STAGE_BYTES = 512 * 1024

_F32 = jnp.float32
_BF16 = jnp.bfloat16


def _dot(a, b):
    return jnp.dot(a, b, preferred_element_type=_F32)


def _silu(g):
    half_g = 0.5 * g
    return half_g * (1.0 + jnp.tanh(half_g))


def _load_weights_as_bf16(weights, slots):
    plans = []
    for w_hbm, w_ref, row_scale_ref in weights:
        k, n = w_hbm.shape
        rows = min(k, pl.next_power_of_2(WEIGHT_STAGE_BYTES // (4 * n) + 1) // 2)
        assert k % rows == 0 and rows % 16 == 0
        n_chunks = k // rows
        n_slots = n_chunks if slots is None else min(slots, n_chunks)
        scale_col = None
        if row_scale_ref is not None:
            scale_col = jnp.broadcast_to(row_scale_ref[...], (LANES, k)).T
        plans.append((w_hbm, w_ref, scale_col, rows, n_chunks, n_slots))

    def body(*scratch):
        stages, sems = scratch[0::2], scratch[1::2]

        def copy(j, i):
            w_hbm, _, _, rows, _, n_slots = plans[j]
            return pltpu.make_async_copy(
                w_hbm.at[pl.ds(i * rows, rows), :], stages[j].at[i % n_slots], sems[j].at[i % n_slots])

        for j, plan in enumerate(plans):
            for i in range(plan[5]):
                copy(j, i).start()
        for j, (_, w_ref, scale_col, rows, n_chunks, n_slots) in enumerate(plans):
            for i in range(n_chunks):
                copy(j, i).wait()
                chunk = stages[j][i % n_slots]
                if scale_col is not None:
                    chunk = chunk * scale_col[i * rows:(i + 1) * rows, 0:1]
                w_ref[i * rows:(i + 1) * rows, :] = chunk.astype(_BF16)
                if i + n_slots < n_chunks:
                    copy(j, i + n_slots).start()

    scratch_types = []
    for w_hbm, _, _, rows, _, n_slots in plans:
        scratch_types += [pltpu.VMEM((n_slots, rows, w_hbm.shape[1]), _F32), pltpu.SemaphoreType.DMA((n_slots,))]
    pl.run_scoped(body, *scratch_types)


def _layer_a_kernel(x_ref, ng_ref, win_hbm, lng_ref, lnb_ref, ws_ref, bs_ref, wout_hbm,
                    o_ref, y_ref, win_ref, wout_ref, wsm_ref, bias_ref):
    @pl.when(pl.program_id(0) == 0)
    def _():
        _load_weights_as_bf16([(win_hbm, win_ref, None)], slots=8)
        _load_weights_as_bf16([(wout_hbm, wout_ref, None)], slots=8)
        row = lax.broadcasted_iota(jnp.int32, (CHUNK, CHUNK), 0)
        col = lax.broadcasted_iota(jnp.int32, (CHUNK, CHUNK), 1)
        for g in range(A_GROUPS):
            wsm_ref[g] = jnp.where(col <= row, ws_ref[g], 0.0).astype(_BF16)
            bias_ref[g] = jnp.broadcast_to(bs_ref[g:g + 1, :], (CHUNK, CHUNK)).T

    tm = x_ref.shape[0]
    groups_per_block = A_COL_BLOCK // CHUNK

    x = x_ref[...]
    h1 = (x * lax.rsqrt(jnp.mean(x * x, axis=-1, keepdims=True) + EPS) * ng_ref[...]).astype(_BF16)

    v = _dot(h1, win_ref[:, A_WIDTH:2 * A_WIDTH])
    d = v - jnp.mean(v[:, :LANES], axis=-1, keepdims=True)
    m1 = jnp.mean(d, axis=-1, keepdims=True)
    rstd = lax.rsqrt(jnp.mean(d * d, axis=-1, keepdims=True) - m1 * m1 + EPS)

    for j in range(A_WIDTH // A_COL_BLOCK):
        c0 = j * A_COL_BLOCK
        u = _dot(h1, win_ref[:, c0:c0 + A_COL_BLOCK])
        gt = _dot(h1, win_ref[:, 2 * A_WIDTH + c0:2 * A_WIDTH + c0 + A_COL_BLOCK])
        for gi in range(groups_per_block):
            g = j * groups_per_block + gi
            gcols = slice(g * CHUNK, (g + 1) * CHUNK)
            cols = slice(gi * CHUNK, (gi + 1) * CHUNK)
            vn = ((d[:, gcols] - m1) * rstd * lng_ref[:, gcols] + lnb_ref[:, gcols]).astype(_BF16)
            for c in range(tm // CHUNK):
                rows = slice(c * CHUNK, (c + 1) * CHUNK)
                sv = _dot(wsm_ref[g], vn[rows]) + bias_ref[g]
                y_ref[rows, gcols] = (u[rows, cols] * sv * _silu(gt[rows, cols])).astype(_BF16)

    o_ref[...] = x + _dot(y_ref[...], wout_ref[...])


def _layer_a(x2, norm_g, w_in, ln_g, ln_b, ws, bs, w_out):
    t = x2.shape[0]
    tm = A_TOKENS_PER_STEP
    const = lambda i: (0, 0)
    return pl.pallas_call(
        _layer_a_kernel,
        out_shape=jax.ShapeDtypeStruct((t, D_MODEL), _F32),
        grid=(t // tm,),
        in_specs=[
            pl.BlockSpec((tm, D_MODEL), lambda i: (i, 0)),
            pl.BlockSpec((1, D_MODEL), const),
            pl.BlockSpec(memory_space=pl.ANY),
            pl.BlockSpec((1, A_WIDTH), const),
            pl.BlockSpec((1, A_WIDTH), const),
            pl.BlockSpec((A_GROUPS, CHUNK, CHUNK), lambda i: (0, 0, 0)),
            pl.BlockSpec((A_GROUPS, CHUNK), const),
            pl.BlockSpec(memory_space=pl.ANY),
        ],
        out_specs=pl.BlockSpec((tm, D_MODEL), lambda i: (i, 0)),
        scratch_shapes=[
            pltpu.VMEM((tm, A_WIDTH), _BF16),
            pltpu.VMEM((D_MODEL, 3 * A_WIDTH), _BF16),
            pltpu.VMEM((A_WIDTH, D_MODEL), _BF16),
            pltpu.VMEM((A_GROUPS, CHUNK, CHUNK), _BF16),
            pltpu.VMEM((A_GROUPS, CHUNK, CHUNK), _F32),
        ],
        compiler_params=pltpu.CompilerParams(
            dimension_semantics=("arbitrary",), vmem_limit_bytes=VMEM_LIMIT_BYTES),
        name="gmlp_layer",
    )(x2, norm_g, w_in, ln_g, ln_b, ws, bs, w_out)


def _layer_b_kernel(sinks_ref, hc_ref, hp_ref, kvg_ref, bg_ref, wkv_hbm, bkv_ref, win_hbm, bq_ref,
                    wout_hbm, fg_ref, cos_ref, sin_ref,
                    o_ref, kt_ref, vx_ref, q_ref, gate_ref, y_ref, wkv_ref, win_ref, wout_ref,
                    *, tiles_per_seq):
    t = pl.program_id(0)

    @pl.when(t == 0)
    def _():
        _load_weights_as_bf16([(wkv_hbm, wkv_ref, kvg_ref), (win_hbm, win_ref, bg_ref),
                               (wout_hbm, wout_ref, None)], slots=None)

    refs = (sinks_ref, hc_ref, hp_ref, wkv_ref, bkv_ref, win_ref, bq_ref,
            wout_ref, fg_ref, cos_ref, sin_ref, o_ref, kt_ref, vx_ref, q_ref, gate_ref, y_ref)

    @pl.when(t % 2 == 0)
    def _():
        _layer_b_step(*refs, t=t, slot=0, tiles_per_seq=tiles_per_seq)

    @pl.when(t % 2 == 1)
    def _():
        _layer_b_step(*refs, t=t, slot=1, tiles_per_seq=tiles_per_seq)


def _layer_b_step(sinks_ref, hc_ref, hp_ref, wkv_ref, bkv_ref, win_ref, bq_ref,
                  wout_ref, fg_ref, cos_ref, sin_ref,
                  o_ref, kt_ref, vx_ref, q_ref, gate_ref, y_ref, *, t, slot, tiles_per_seq):
    tm = hc_ref.shape[0]
    prev = 1 - slot
    new = slice(CHUNK, CHUNK + tm)
    planes = 2 * N_KV_HEADS

    if slot == 0:
        @pl.when(t == 0)
        def _():
            kt_ref[prev] = jnp.zeros((planes, LANES, CHUNK + tm), _BF16)
            vx_ref[prev] = jnp.zeros((planes, CHUNK + tm, 2 * LANES), _BF16)
            q_ref[prev] = jnp.zeros((tm, B_WIDTH), _BF16)
            gate_ref[prev] = jnp.zeros((tm, B_WIDTH), _F32)

    first_in_seq = (t % tiles_per_seq) == 0

    @pl.when(first_in_seq)
    def _():
        kt_ref[slot, :, :, 0:CHUNK] = jnp.zeros((planes, LANES, CHUNK), _BF16)
        vx_ref[slot, :, 0:CHUNK, :] = jnp.zeros((planes, CHUNK, 2 * LANES), _BF16)

    @pl.when(jnp.logical_not(first_in_seq))
    def _():
        kt_ref[slot, :, :, 0:CHUNK] = kt_ref[prev, :, :, tm:tm + CHUNK]
        vx_ref[slot, :, 0:CHUNK, :] = vx_ref[prev, :, tm:tm + CHUNK, :]

    h = hc_ref[...]
    h2 = (h * lax.rsqrt(jnp.mean(h * h, axis=-1, keepdims=True) + EPS)).astype(_BF16)

    cos = cos_ref[...]
    sin = sin_ref[...]
    lane = lax.broadcasted_iota(jnp.int32, (tm, LANES), 1)
    first_half = (lane % HEAD_DIM) < (HEAD_DIM // 2)
    low_head = lane < HEAD_DIM

    def rope(x, cos_t, sin_t):
        swapped = jnp.where(first_half,
                            pltpu.roll(x, LANES - HEAD_DIM // 2, 1),
                            pltpu.roll(x, HEAD_DIM // 2, 1))
        return x * cos_t + swapped * sin_t

    def project_kv():
        kv = _dot(h2, wkv_ref[...]) + bkv_ref[...]
        k_t = rope(kv[:, :KV_WIDTH], cos, sin).T
        zero_rows = jnp.zeros((HEAD_DIM, tm), _F32)
        for kvh in range(N_KV_HEADS):
            head_t = k_t[kvh * HEAD_DIM:(kvh + 1) * HEAD_DIM]
            kt_ref[slot, 2 * kvh, :, new] = jnp.concatenate([head_t, zero_rows], axis=0).astype(_BF16)
            kt_ref[slot, 2 * kvh + 1, :, new] = jnp.concatenate([zero_rows, head_t], axis=0).astype(_BF16)
        v = kv[:, KV_WIDTH:]
        v_sw = pltpu.roll(v, HEAD_DIM, 1)
        zero = jnp.zeros_like(v)
        one = jnp.ones_like(v)
        ones_lo = jnp.where(low_head, one, zero)
        ones_hi = jnp.where(low_head, zero, one)
        v_low = (jnp.where(low_head, v, zero), jnp.where(low_head, v_sw, zero))
        v_high = (jnp.where(low_head, zero, v_sw), jnp.where(low_head, zero, v))
        for kvh in range(N_KV_HEADS):
            vx_ref[slot, 2 * kvh, new, :] = jnp.concatenate([v_low[kvh], ones_lo], axis=1).astype(_BF16)
            vx_ref[slot, 2 * kvh + 1, new, :] = jnp.concatenate([v_high[kvh], ones_hi], axis=1).astype(_BF16)

    q_scale = (HEAD_DIM ** -0.5) * LOG2_E
    cos_q = cos * q_scale
    sin_q = sin * q_scale

    def project_q(b):
        cols = slice(b * P_BLOCK, (b + 1) * P_BLOCK)
        zq = _dot(h2, win_ref[:, cols]) + bq_ref[:, cols]
        for i in range(P_BLOCK // LANES):
            s = b * (P_BLOCK // LANES) + i
            q_ref[slot, :, s * LANES:(s + 1) * LANES] = rope(
                zq[:, i * LANES:(i + 1) * LANES], cos_q, sin_q).astype(_BF16)

    def project_gate(b):
        cols = slice(b * P_BLOCK, (b + 1) * P_BLOCK)
        gate_ref[slot, :, cols] = _dot(h2, win_ref[:, B_WIDTH + b * P_BLOCK:B_WIDTH + (b + 1) * P_BLOCK])

    qi = lax.broadcasted_iota(jnp.int32, (CHUNK, 2 * CHUNK), 0)
    kj = lax.broadcasted_iota(jnp.int32, (CHUNK, 2 * CHUNK), 1)
    in_window = (kj > qi) & (kj <= qi + CHUNK)
    neg_inf = jnp.float32(-jnp.inf)
    bias = jnp.where(in_window, 0.0, neg_inf)
    prev_first_in_seq = ((t + tiles_per_seq - 1) % tiles_per_seq) == 0
    bias_first = jnp.where(in_window & ((kj >= CHUNK) | jnp.logical_not(prev_first_in_seq)), 0.0, neg_inf)

    units = [(c, kvh) for kvh in range(N_KV_HEADS) for c in range(tm // CHUNK)]
    assert len(units) == 4 and B_WIDTH // P_BLOCK == 4, "the program order below is written for 4 + 4"

    def scores(unit):
        c, kvh = unit
        rows = slice(c * CHUNK, (c + 1) * CHUNK)
        band = slice(c * CHUNK, (c + 2) * CHUNK)
        q_stack = jnp.concatenate(
            [q_ref[prev, rows, s * LANES:(s + 1) * LANES]
             for s in range(kvh * SLABS_PER_KV, (kvh + 1) * SLABS_PER_KV)], axis=0)
        return [_dot(q_stack, kt_ref[prev, 2 * kvh + half, :, band]) for half in range(HEADS_PER_SLAB)]

    sink_slot = lax.broadcasted_iota(jnp.int32, (CHUNK, LANES), 1) == 0
    key_row = lax.broadcasted_iota(jnp.int32, (BF16_ROWS, 2 * LANES), 0)
    key_lane = lax.broadcasted_iota(jnp.int32, (BF16_ROWS, 2 * LANES), 1)

    def softmax(unit, sc_halves):
        c, kvh = unit
        chunk_bias = bias_first if c == 0 else bias
        probs = []
        for half, sc in enumerate(sc_halves):
            p_parts = []
            for si in range(SLABS_PER_KV):
                s = kvh * SLABS_PER_KV + si
                sink = sinks_ref[s * HEADS_PER_SLAB + half] * LOG2_E
                blk = sc[si * CHUNK:(si + 1) * CHUNK]
                sb = jnp.concatenate(
                    [jnp.where(sink_slot, sink, blk[:, :LANES] + chunk_bias[:, :LANES]),
                     blk[:, LANES:] + chunk_bias[:, LANES:]], axis=1)
                m = jnp.max(sb, axis=-1, keepdims=True)
                p_parts.append(jnp.exp2(sb - m).astype(_BF16))
            probs.append(jnp.concatenate(p_parts, axis=0))
        return probs

    def band_values(kvh, half, c):
        r0 = c * CHUNK
        ones_lo = LANES + half * HEAD_DIM
        sink_row = jnp.where((key_lane >= ones_lo) & (key_lane < ones_lo + HEAD_DIM), 1.0, 0.0)
        head = vx_ref[prev, 2 * kvh + half, r0:r0 + BF16_ROWS, :].astype(_F32)
        head = jnp.where(key_row == 0, sink_row, head).astype(_BF16)
        return jnp.concatenate(
            [head, vx_ref[prev, 2 * kvh + half, r0 + BF16_ROWS:r0 + 2 * CHUNK, :]], axis=0)

    def attend(unit, probs):
        c, kvh = unit
        rows = slice(c * CHUNK, (c + 1) * CHUNK)
        od = _dot(probs[0], band_values(kvh, 0, c)) + _dot(probs[1], band_values(kvh, 1, c))
        for si in range(SLABS_PER_KV):
            s = kvh * SLABS_PER_KV + si
            blk = slice(si * CHUNK, (si + 1) * CHUNK)
            gate = gate_ref[prev, rows, s * LANES:(s + 1) * LANES]
            y_ref[rows, s * LANES:(s + 1) * LANES] = (
                od[blk, :LANES] * _silu(gate) / od[blk, LANES:]).astype(_BF16)

    def out_part(kvh):
        cols = slice(kvh * SLABS_PER_KV * LANES, (kvh + 1) * SLABS_PER_KV * LANES)
        return _dot(y_ref[:, cols], wout_ref[cols, :])

    s0 = scores(units[0])
    project_kv()
    s1 = scores(units[1])
    project_q(0)
    p0 = softmax(units[0], s0)
    attend(units[0], p0)
    s2 = scores(units[2])
    project_q(1)
    p1 = softmax(units[1], s1)
    attend(units[1], p1)
    s3 = scores(units[3])
    project_q(2)
    p2 = softmax(units[2], s2)
    attend(units[2], p2)
    project_q(3)
    project_gate(0)
    acc = out_part(0)
    p3 = softmax(units[3], s3)
    attend(units[3], p3)
    project_gate(1)
    out = hp_ref[...] + acc + out_part(1)
    o_ref[...] = out * lax.rsqrt(jnp.mean(out * out, axis=-1, keepdims=True) + EPS) * fg_ref[...]
    project_gate(2)
    project_gate(3)


def _layer_b(h2d, seq, sinks, kv_g, b_g, w_kv, b_kv, w_in, b_q, w_out, f_g, cos, sin):
    tm = B_TOKENS_PER_STEP
    n_tiles = h2d.shape[0] // tm
    tiles_per_seq = seq // tm
    const = lambda t: (0, 0)
    cur = lambda t: (jnp.minimum(t, n_tiles - 1), 0)
    prv = lambda t: (jnp.maximum(t - 1, 0), 0)
    tab = lambda t: (jnp.minimum(t, n_tiles - 1) % tiles_per_seq, 0)
    planes = 2 * N_KV_HEADS
    return pl.pallas_call(
        functools.partial(_layer_b_kernel, tiles_per_seq=tiles_per_seq),
        out_shape=jax.ShapeDtypeStruct(h2d.shape, _F32),
        grid=(n_tiles + 1,),
        in_specs=[
            pl.BlockSpec(memory_space=pltpu.SMEM),
            pl.BlockSpec((tm, D_MODEL), cur),
            pl.BlockSpec((tm, D_MODEL), prv),
            pl.BlockSpec((1, D_MODEL), const),
            pl.BlockSpec((1, D_MODEL), const),
            pl.BlockSpec(memory_space=pl.ANY),
            pl.BlockSpec((1, 2 * KV_WIDTH), const),
            pl.BlockSpec(memory_space=pl.ANY),
            pl.BlockSpec((1, B_WIDTH), const),
            pl.BlockSpec(memory_space=pl.ANY),
            pl.BlockSpec((1, D_MODEL), const),
            pl.BlockSpec((tm, LANES), tab),
            pl.BlockSpec((tm, LANES), tab),
        ],
        out_specs=pl.BlockSpec((tm, D_MODEL), prv),
        scratch_shapes=[
            pltpu.VMEM((2, planes, LANES, CHUNK + tm), _BF16),
            pltpu.VMEM((2, planes, CHUNK + tm, 2 * LANES), _BF16),
            pltpu.VMEM((2, tm, B_WIDTH), _BF16),
            pltpu.VMEM((2, tm, B_WIDTH), _F32),
            pltpu.VMEM((tm, B_WIDTH), _BF16),
            pltpu.VMEM((D_MODEL, 2 * KV_WIDTH), _BF16),
            pltpu.VMEM((D_MODEL, 2 * B_WIDTH), _BF16),
            pltpu.VMEM((B_WIDTH, D_MODEL), _BF16),
        ],
        compiler_params=pltpu.CompilerParams(
            dimension_semantics=("arbitrary",), vmem_limit_bytes=VMEM_LIMIT_BYTES),
        name="swa_layer",
    )(sinks, h2d, h2d, kv_g, b_g, w_kv, b_kv, w_in, b_q, w_out, f_g, cos, sin)


def _rope_tables(seq):
    inv_freq = ROPE_THETA ** (-np.arange(0, HEAD_DIM, 2, dtype=np.float64) / HEAD_DIM)
    ang = np.arange(seq, dtype=np.float64)[:, None] * inv_freq[None, :]
    cos = np.cos(ang)
    sin = np.sin(ang)
    cos_slab = np.tile(np.concatenate([cos, cos], axis=-1), (1, HEADS_PER_SLAB))
    sin_slab = np.tile(np.concatenate([-sin, sin], axis=-1), (1, HEADS_PER_SLAB))
    assert cos_slab.shape == (seq, LANES)
    return jnp.asarray(cos_slab, dtype=_F32), jnp.asarray(sin_slab, dtype=_F32)


def kernel(x, a_norm_g, a_w_in, a_ln_g, a_ln_b, a_ws, a_bs, a_w_out, kv_norm_g, w_kv, b_kv,
           b_norm_g, b_w_in, b_bq, b_sinks, b_w_out, final_norm_g):
    batch, seq, d = x.shape
    assert d == D_MODEL
    for tile in (A_TOKENS_PER_STEP, B_TOKENS_PER_STEP):
        assert seq % tile == 0 and tile % CHUNK == 0
    assert a_w_in.shape[0] == 1 and b_w_in.shape[0] == 1
    row = lambda p: p.reshape(1, -1)

    x2 = x.reshape(batch * seq, d)
    h = _layer_a(x2, row(a_norm_g[0]), a_w_in[0], row(a_ln_g[0]), row(a_ln_b[0]),
                 a_ws[0], a_bs[0], a_w_out[0])
    cos, sin = _rope_tables(seq)
    out = _layer_b(h, seq, b_sinks[0], row(kv_norm_g), row(b_norm_g[0]),
                   w_kv, row(b_kv), b_w_in[0], row(b_bq[0]),
                   b_w_out[0], row(final_norm_g), cos, sin)
    return out.reshape(batch, seq, d)
```
